```python
import math
import jax, jax.numpy as jnp
from jax import lax
import numpy as np

D_MODEL = 1024
BATCH = 8
SEQ = 2048
DEPTH = 4
DEC_BATCH = 128
DEC_SEQ = 4
PAST_LEN = 2048
PAGE_SIZE = 128

PLE_DIM = 256
GROUP_W = D_MODEL // 4
CONV_W = 3
POOL_WINDOWS = (2, 4, 8, 16)
POOL_GW = GROUP_W // len(POOL_WINDOWS)
POOL_BUF = max(POOL_WINDOWS) - 1
DSA_HEADS = 4
DSA_HD = GROUP_W // DSA_HEADS
IDX_HEADS = 8
IDX_HD = 64
IDX_SCALE = IDX_HEADS ** -0.5 * IDX_HD ** -0.5
TOPK_MAX = 256
DIFF_HEADS = 4
DIFF_VD = GROUP_W // DIFF_HEADS
DIFF_QD = DIFF_VD // 2
D_FF = 2816
Q_BLOCK = 128
EPS = 1e-6
SPLIT_SIZES = (GROUP_W, GROUP_W, GROUP_W,
               GROUP_W,
               DSA_HEADS * DSA_HD, DSA_HD, DSA_HD,
               IDX_HEADS * IDX_HD, IDX_HEADS, IDX_HD,
               DIFF_HEADS * 2 * DIFF_QD, DIFF_HEADS * 2 * DIFF_QD, DIFF_HEADS * DIFF_VD)
D_IN = sum(SPLIT_SIZES)

kernel_name = 'hymba_style_hybrid_decode_step'


def rmsnorm(x, g):
    xf = x.astype(jnp.float32)
    y = xf * lax.rsqrt(jnp.mean(xf * xf, axis=-1, keepdims=True) + EPS)
    return (y * g.astype(jnp.float32)).astype(x.dtype)


def causal_dwconv(buf, x, w):
    ext = jnp.concatenate([buf.astype(x.dtype), x], axis=1)
    T = x.shape[1]
    y = sum(ext[:, k:k + T] * w[k] for k in range(CONV_W))
    return y, ext[:, -(CONV_W - 1):]


def multiscale_pool(buf, v, pos0, maps, scale):
    Bt, T, C = v.shape
    ext = jnp.concatenate([buf.astype(v.dtype), v], axis=1).astype(jnp.float32)
    cs = jnp.concatenate([jnp.zeros((Bt, 1, C), jnp.float32), jnp.cumsum(ext, axis=1)], axis=1)
    end = cs[:, POOL_BUF + 1:POOL_BUF + 1 + T]
    pos = pos0 + jnp.arange(T)
    outs = []
    for g, w in enumerate(POOL_WINDOWS):
        sl = slice(g * POOL_GW, (g + 1) * POOL_GW)
        start = cs[:, POOL_BUF + 1 - w:POOL_BUF + 1 - w + T, sl]
        cnt = jnp.minimum(w, pos + 1).astype(jnp.float32)[None, :, None]
        outs.append((end[..., sl] - start) / cnt - ext[:, POOL_BUF:, sl])
    d = jnp.stack(outs, axis=2).astype(v.dtype)
    y = jnp.einsum('btgc,gcd->btgd', d, maps).reshape(Bt, T, C) * scale
    return y, ext[:, -POOL_BUF:].astype(v.dtype)


def dsa_block(q, qi, wi, qpos, k, v, ki, n_sel):
    S = k.shape[1]
    causal = jnp.arange(S)[None, :] <= qpos[:, None]
    s = jnp.einsum('bqhd,bsd->bqhs', qi, ki).astype(jnp.float32)
    score = jnp.einsum('bqhs,bqh->bqs', jax.nn.relu(s), wi.astype(jnp.float32)) * IDX_SCALE
    score = jnp.where(causal[None], score, -jnp.inf)
    _, sel = lax.top_k(score, n_sel)
    gather = jax.vmap(lambda a, i: a[i])
    k_sel = gather(k, sel)
    v_sel = gather(v, sel)
    valid = sel <= qpos[None, :, None]
    logits = jnp.einsum('bqhd,bqkd->bhqk', q, k_sel).astype(jnp.float32) * DSA_HD ** -0.5
    logits = jnp.where(valid[:, None], logits, -jnp.inf)
    a = jax.nn.softmax(logits, axis=-1).astype(v.dtype)
    return jnp.einsum('bhqk,bqkd->bqhd', a, v_sel)


def diff_block(q, qpos, k, v, lam):
    S = k.shape[1]
    causal = jnp.arange(S)[None, :] <= qpos[:, None]
    def probs(qh, kh):
        l = jnp.einsum('bqhd,bshd->bhqs', qh, kh).astype(jnp.float32) * DIFF_QD ** -0.5
        return jax.nn.softmax(jnp.where(causal, l, -jnp.inf), axis=-1)
    a = probs(q[..., :DIFF_QD], k[..., :DIFF_QD]) - lam * probs(q[..., DIFF_QD:], k[..., DIFF_QD:])
    return jnp.einsum('bhqs,bshd->bqhd', a.astype(v.dtype), v)


def over_query_blocks(fn, q_args, qpos):
    T = qpos.shape[0]
    if T <= Q_BLOCK:
        return fn(*q_args, qpos)
    nb = T // Q_BLOCK
    def split(a):
        return jnp.moveaxis(a.reshape(a.shape[0], nb, Q_BLOCK, *a.shape[2:]), 1, 0)
    xs = tuple(split(a) for a in q_args) + (qpos.reshape(nb, Q_BLOCK),)
    out = lax.map(lambda b: fn(*b), xs)
    out = jnp.moveaxis(out, 0, 1)
    return out.reshape(out.shape[0], T, *out.shape[3:])


def gather_pages(cache, page_table, i):
    g = cache[page_table, i]
    return g.reshape(g.shape[0], g.shape[1] * g.shape[2], *g.shape[3:])


def trunk_layer(i, h, p, conv_buf, pool_buf, ffn_buf, past_c, past_d, W):
    Bt, T, _ = h.shape
    P = 0 if past_c is None else past_c[0].shape[1]
    qpos = P + jnp.arange(T)
    xn = rmsnorm(h, W['norm_mix'][i])
    z = xn @ W['w_in'][i]
    cuts = np.cumsum(SPLIT_SIZES)[:-1].tolist()
    (gb, gc, gh, pv, cq, ck, cv, cqi, cwi, cki, dq, dk, dv) = jnp.split(z, cuts, axis=-1)
    conv_out, new_conv = causal_dwconv(conv_buf, gc * gh, W['conv_a'][i])
    y_a = gb * conv_out
    y_b, new_pool = multiscale_pool(pool_buf, pv, P, W['pool_maps'][i], W['pool_scale'][i])
    cq = cq.reshape(Bt, T, DSA_HEADS, DSA_HD)
    cqi = cqi.reshape(Bt, T, IDX_HEADS, IDX_HD)
    if past_c is None:
        kc, vc, kic = ck, cv, cki
    else:
        kc = jnp.concatenate([past_c[0].astype(ck.dtype), ck], axis=1)
        vc = jnp.concatenate([past_c[1].astype(cv.dtype), cv], axis=1)
        kic = jnp.concatenate([past_c[2].astype(cki.dtype), cki], axis=1)
    n_sel = min(TOPK_MAX, (P + T) // 4)
    y_c = over_query_blocks(lambda q_, qi_, wi_, qp_: dsa_block(q_, qi_, wi_, qp_, kc, vc, kic, n_sel),
                            (cq, cqi, cwi), qpos).reshape(Bt, T, GROUP_W)
    dq = dq.reshape(Bt, T, DIFF_HEADS, 2 * DIFF_QD)
    dk = dk.reshape(Bt, T, DIFF_HEADS, 2 * DIFF_QD)
    dv = dv.reshape(Bt, T, DIFF_HEADS, DIFF_VD)
    if past_d is None:
        kd, vd = dk, dv
    else:
        kd = jnp.concatenate([past_d[0].astype(dk.dtype), dk], axis=1)
        vd = jnp.concatenate([past_d[1].astype(dv.dtype), dv], axis=1)
    lam_init = 0.8 - 0.6 * math.exp(-0.3 * i)
    lp = W['diff_lambda'][i].astype(jnp.float32)
    lam = jnp.exp(jnp.sum(lp[0] * lp[1])) - jnp.exp(jnp.sum(lp[2] * lp[3])) + lam_init
    o = over_query_blocks(lambda q_, qp_: diff_block(q_, qp_, kd, vd, lam), (dq,), qpos)
    y_d = (rmsnorm(o, W['diff_subln'][i]) * (1.0 - lam_init)).reshape(Bt, T, GROUP_W)
    h = h + jnp.concatenate([y_a, y_b, y_c, y_d], axis=-1) @ W['w_out'][i]
    xn2 = rmsnorm(h, W['norm_ffn'][i])
    g_conv, new_ffn = causal_dwconv(ffn_buf, xn2 @ W['w_gate'][i], W['conv_ffn'][i])
    h = h + (jax.nn.silu(g_conv) * (xn2 @ W['w_up'][i])) @ W['w_down'][i]
    gate = jax.nn.sigmoid(rmsnorm(h, W['norm_ple'][i]) @ W['w_ple_gate'][i])
    h = h + (p @ W['w_ple'][i]) * gate
    return h, (new_conv, new_pool, new_ffn, ck, cv, cki, dk, dv)


def setup_inputs(seed: int = 0) -> dict:
    key = jax.random.key(seed)
    ks = list(jax.random.split(key, 40))
    def nrm(shape, s=1.0):
        return jax.random.normal(ks.pop(), shape, jnp.float32) * s
    n_pages = PAST_LEN // PAGE_SIZE
    n_used = DEC_BATCH * n_pages
    n_phys = (n_used * 5) // 4
    perm = jax.random.permutation(ks.pop(), n_phys)
    page_table = perm[:n_used].reshape(DEC_BATCH, n_pages).astype(jnp.int32)
    return {
        'x_prompt': nrm((BATCH, SEQ, D_MODEL)),
        'x_sample': nrm((DEC_BATCH, DEC_SEQ, D_MODEL)),
        'state_conv_a': nrm((DEPTH, DEC_BATCH, CONV_W - 1, GROUP_W)),
        'state_pool': nrm((DEPTH, DEC_BATCH, POOL_BUF, GROUP_W)),
        'state_ffn': nrm((DEPTH, DEC_BATCH, CONV_W - 1, D_FF)),
        'cache_c_k': nrm((n_phys, DEPTH, PAGE_SIZE, DSA_HD)),
        'cache_c_v': nrm((n_phys, DEPTH, PAGE_SIZE, DSA_HD)),
        'cache_c_idx': nrm((n_phys, DEPTH, PAGE_SIZE, IDX_HD)),
        'cache_d_k': nrm((n_phys, DEPTH, PAGE_SIZE, DIFF_HEADS, 2 * DIFF_QD)),
        'cache_d_v': nrm((n_phys, DEPTH, PAGE_SIZE, DIFF_HEADS, DIFF_VD)),
        'page_table': page_table,
        'p_prompt': nrm((DEPTH, BATCH, SEQ, PLE_DIM)),
        'p_sample': nrm((DEPTH, DEC_BATCH, DEC_SEQ, PLE_DIM)),
        'norm_mix': 1.0 + nrm((DEPTH, D_MODEL), 0.05),
        'w_in': nrm((DEPTH, D_MODEL, D_IN), D_MODEL ** -0.5),
        'conv_a': nrm((DEPTH, CONV_W, GROUP_W), 0.5),
        'pool_maps': nrm((DEPTH, len(POOL_WINDOWS), POOL_GW, POOL_GW), POOL_GW ** -0.5),
        'pool_scale': 1.0 + nrm((DEPTH, GROUP_W), 0.1),
        'diff_lambda': nrm((DEPTH, 4, DIFF_QD), 0.1),
        'diff_subln': 1.0 + nrm((DEPTH, DIFF_VD), 0.05),
        'w_out': nrm((DEPTH, D_MODEL, D_MODEL), D_MODEL ** -0.5),
        'norm_ffn': 1.0 + nrm((DEPTH, D_MODEL), 0.05),
        'w_gate': nrm((DEPTH, D_MODEL, D_FF), D_MODEL ** -0.5),
        'w_up': nrm((DEPTH, D_MODEL, D_FF), D_MODEL ** -0.5),
        'conv_ffn': nrm((DEPTH, CONV_W, D_FF), 0.5),
        'w_down': nrm((DEPTH, D_FF, D_MODEL), D_FF ** -0.5),
        'norm_ple': 1.0 + nrm((DEPTH, D_MODEL), 0.05),
        'w_ple_gate': nrm((DEPTH, D_MODEL, D_MODEL), D_MODEL ** -0.5),
        'w_ple': nrm((DEPTH, PLE_DIM, D_MODEL), PLE_DIM ** -0.5),
        'norm_final': 1.0 + nrm((D_MODEL,), 0.05),
    }


def reference(x_prompt, x_sample, state_conv_a, state_pool, state_ffn, cache_c_k, cache_c_v,
              cache_c_idx, cache_d_k, cache_d_v, page_table, p_prompt, p_sample, norm_mix, w_in,
              conv_a, pool_maps, pool_scale, diff_lambda, diff_subln, w_out, norm_ffn, w_gate,
              w_up, conv_ffn, w_down, norm_ple, w_ple_gate, w_ple, norm_final):
    W = dict(norm_mix=norm_mix, w_in=w_in, conv_a=conv_a, pool_maps=pool_maps,
             pool_scale=pool_scale, diff_lambda=diff_lambda, diff_subln=diff_subln, w_out=w_out,
             norm_ffn=norm_ffn, w_gate=w_gate, w_up=w_up, conv_ffn=conv_ffn, w_down=w_down,
             norm_ple=norm_ple, w_ple_gate=w_ple_gate, w_ple=w_ple)
    h = x_prompt
    bp = h.shape[0]
    outs_p = []
    for i in range(DEPTH):
        zc = jnp.zeros((bp, CONV_W - 1, GROUP_W), h.dtype)
        zp = jnp.zeros((bp, POOL_BUF, GROUP_W), h.dtype)
        zf = jnp.zeros((bp, CONV_W - 1, D_FF), h.dtype)
        h, st = trunk_layer(i, h, p_prompt[i], zc, zp, zf, None, None, W)
        outs_p.append(st)
    y_prompt = rmsnorm(h, norm_final)
    h = x_sample
    outs_s = []
    for i in range(DEPTH):
        past_c = (gather_pages(cache_c_k, page_table, i), gather_pages(cache_c_v, page_table, i),
                  gather_pages(cache_c_idx, page_table, i))
        past_d = (gather_pages(cache_d_k, page_table, i), gather_pages(cache_d_v, page_table, i))
        h, st = trunk_layer(i, h, p_sample[i], state_conv_a[i], state_pool[i], state_ffn[i],
                            past_c, past_d, W)
        outs_s.append(st)
    y_sample = rmsnorm(h, norm_final)

    def col(outs, j, axis):
        return jnp.stack([o[j] for o in outs], axis=axis)
    conv_a_p, conv_a_s = col(outs_p, 0, 0), col(outs_s, 0, 0)
    pool_p, pool_s = col(outs_p, 1, 0), col(outs_s, 1, 0)
    ffn_p, ffn_s = col(outs_p, 2, 0), col(outs_s, 2, 0)
    c_k_p, c_k_s = col(outs_p, 3, 1), col(outs_s, 3, 1)
    c_v_p, c_v_s = col(outs_p, 4, 1), col(outs_s, 4, 1)
    c_idx_p, c_idx_s = col(outs_p, 5, 1), col(outs_s, 5, 1)
    d_k_p, d_k_s = col(outs_p, 6, 1), col(outs_s, 6, 1)
    d_v_p, d_v_s = col(outs_p, 7, 1), col(outs_s, 7, 1)
    return (y_prompt, y_sample, conv_a_p, conv_a_s, pool_p, pool_s, ffn_p, ffn_s,
            c_k_p, c_k_s, c_v_p, c_v_s, c_idx_p, c_idx_s, d_k_p, d_k_s, d_v_p, d_v_s)
```

```python
import functools
import math

import numpy as np
import jax
import jax.numpy as jnp
from jax import lax
from jax.experimental import pallas as pl
from jax.experimental.pallas import tpu as pltpu

F32 = jnp.float32
BF16 = jnp.bfloat16
I32 = jnp.int32

EPS = 1e-6
GROUP_W = 256
POOL_WINDOWS = (2, 4, 8, 16)
POOL_BUF = 15
DSA_HEADS = 4
DSA_HD = 64
IDX_HEADS = 8
IDX_HD = 64
IDX_SCALE = IDX_HEADS ** -0.5 * IDX_HD ** -0.5
TOPK_MAX = 256
DIFF_HEADS = 4
DIFF_VD = 64
DIFF_QD = 32
SPLIT_SIZES = (256, 256, 256, 256, 256, 64, 64, 512, 8, 64, 256, 256, 256)

LANES = 128
SUBLANES = 8
VMEM_LIMIT = 56 * 1024 * 1024

C_GB, C_GC, C_GH, C_PV = 0, 256, 512, 768
C_CQ, C_CQI, C_DQ, C_DK, C_DV = 1024, 1280, 1792, 2048, 2304
C_KK, C_II, C_VW = 2560, 2688, 2816
D_IN_P = 2944

KEY_BLK = 256
Q_BLK = 128
NEG = -1e30
INT_MIN = -2 ** 31

NT_DIMS = (((1,), (1,)), ((), ()))


def _dot(a, b):
    return jnp.dot(a, b, preferred_element_type=F32)


def _dot_nt(a, b):
    return lax.dot_general(a, b, NT_DIMS, preferred_element_type=F32)


def _rms(x, g):
    ms = jnp.mean(x * x, axis=-1, keepdims=True)
    return (x * lax.rsqrt(ms + EPS)) * g


def _sigmoid(x):
    return 1.0 / (1.0 + jnp.exp(-x))


def _float_key(x):
    b = lax.bitcast_convert_type(x, I32)
    return b ^ (lax.shift_right_arithmetic(b, 31) & 0x7FFFFFFF)


def _lambda_full(lam_ref, lam_init):
    lp = lam_ref[...]
    s1 = jnp.sum(lp[0:1] * lp[1:2], axis=-1, keepdims=True)
    s2 = jnp.sum(lp[2:3] * lp[3:4], axis=-1, keepdims=True)
    return jnp.exp(s1) - jnp.exp(s2) + lam_init


def _const_spec(shape):
    nd = len(shape)
    return pl.BlockSpec(shape, lambda *_: (0,) * nd)


def _k1p_kernel(x_ref, g_ref, w_ref, ca_ref, pm_ref, ps_ref,
                mixab_ref, cq_ref, cqi_ref, dq_ref, dk_ref, dv_ref, skk_ref, sii_ref, svw_ref,
                kkb_ref, iib_ref, dkb_ref, cvT_ref, cwiT_ref, dvT_ref, convst_ref, poolst_ref,
                extu_ref, extx_ref, *, TS):
    t = pl.program_id(1)

    @pl.when(t == 0)
    def _():
        extu_ref[0:8, :] = jnp.zeros((8, GROUP_W), F32)
        extx_ref[0:16, :] = jnp.zeros((16, GROUP_W), F32)

    xn = _rms(x_ref[0], g_ref[...])
    z = _dot(xn.astype(BF16), w_ref[...])

    u = z[:, C_GC:C_GC + 256] * z[:, C_GH:C_GH + 256]
    extu_ref[8:8 + TS, :] = u
    p1 = extu_ref[7:7 + TS, :]
    p2 = extu_ref[6:6 + TS, :]
    ca = ca_ref[...]
    conv = p2 * ca[0:1] + p1 * ca[1:2] + u * ca[2:3]
    mixab_ref[0, :, 0:256] = z[:, C_GB:C_GB + 256] * conv
    extu_ref[0:8, :] = u[TS - 8:TS]
    convst_ref[0] = u[TS - 8:TS]

    pv = z[:, C_PV:C_PV + 256]
    extx_ref[16:16 + TS, :] = pv

    def sh(j, c0):
        return extx_ref[16 - j:16 - j + TS, c0:c0 + LANES]

    s2 = sh(0, 0) + sh(1, 0)
    s4 = s2 + sh(2, 0) + sh(3, 0)
    s8 = sh(0, LANES)
    for j in range(1, 8):
        s8 = s8 + sh(j, LANES)
    s16 = s8
    for j in range(8, 16):
        s16 = s16 + sh(j, LANES)
    lo = lax.broadcasted_iota(I32, (TS, LANES), 1) < 64
    posp1 = (t * TS + 1 + lax.broadcasted_iota(I32, (TS, LANES), 0)).astype(F32)
    d0 = jnp.where(lo, s2, s4) / jnp.where(lo, jnp.minimum(posp1, 2.0), jnp.minimum(posp1, 4.0)) - pv[:, 0:LANES]
    d1 = jnp.where(lo, s8, s16) / jnp.where(lo, jnp.minimum(posp1, 8.0), jnp.minimum(posp1, 16.0)) - pv[:, LANES:]
    d = jnp.concatenate([d0, d1], axis=1)
    mixab_ref[0, :, 256:512] = _dot(d.astype(BF16), pm_ref[...]) * ps_ref[...]
    extx_ref[0:16, :] = pv[TS - 16:TS]
    poolst_ref[0] = pv[TS - 16:TS]

    cq_ref[0] = z[:, C_CQ:C_CQ + 256].astype(BF16)
    cqi_ref[0] = z[:, C_CQI:C_CQI + 512].astype(BF16)
    dq_ref[0] = z[:, C_DQ:C_DQ + 256].astype(BF16)
    dk = z[:, C_DK:C_DK + 256]
    dv = z[:, C_DV:C_DV + 256]
    skk = z[:, C_KK:C_KK + LANES]
    sii = z[:, C_II:C_II + LANES]
    svw = z[:, C_VW:C_VW + LANES]
    dk_ref[0] = dk
    dv_ref[0] = dv
    skk_ref[0] = skk
    sii_ref[0] = sii
    svw_ref[0] = svw
    nb = TS // KEY_BLK
    kkb_ref[0] = skk.astype(BF16).reshape(nb, KEY_BLK, LANES)
    iib_ref[0] = sii.astype(BF16).reshape(nb, KEY_BLK, LANES)
    dkb_ref[0] = dk.astype(BF16).reshape(nb, KEY_BLK, 256)
    svw_t = svw.T
    cwiT_ref[0] = svw_t[64:72]
    for j in range(nb):
        cvT_ref[0, j] = svw_t[0:64, j * KEY_BLK:(j + 1) * KEY_BLK].astype(BF16)
        dvT_ref[0, j] = dv[j * KEY_BLK:(j + 1) * KEY_BLK].T.astype(BF16)


def _k1p(x, g, w, ca, pm, ps, *, TS):
    B, T, D = x.shape
    nt = T // TS
    nb = TS // KEY_BLK
    nkb = T // KEY_BLK

    def row_spec(c):
        return pl.BlockSpec((1, TS, c), lambda b, t: (b, t, 0))

    def blk_spec(r, c):
        return pl.BlockSpec((1, nb, r, c), lambda b, t: (b, t, 0, 0))

    out_shape = (
        jax.ShapeDtypeStruct((B, T, 512), F32),
        jax.ShapeDtypeStruct((B, T, 256), BF16),
        jax.ShapeDtypeStruct((B, T, 512), BF16),
        jax.ShapeDtypeStruct((B, T, 256), BF16),
        jax.ShapeDtypeStruct((B, T, 256), F32),
        jax.ShapeDtypeStruct((B, T, 256), F32),
        jax.ShapeDtypeStruct((B, T, LANES), F32),
        jax.ShapeDtypeStruct((B, T, LANES), F32),
        jax.ShapeDtypeStruct((B, T, LANES), F32),
        jax.ShapeDtypeStruct((B, nkb, KEY_BLK, LANES), BF16),
        jax.ShapeDtypeStruct((B, nkb, KEY_BLK, LANES), BF16),
        jax.ShapeDtypeStruct((B, nkb, KEY_BLK, 256), BF16),
        jax.ShapeDtypeStruct((B, nkb, 64, KEY_BLK), BF16),
        jax.ShapeDtypeStruct((B, 8, T), F32),
        jax.ShapeDtypeStruct((B, nkb, 256, KEY_BLK), BF16),
        jax.ShapeDtypeStruct((B, 8, 256), F32),
        jax.ShapeDtypeStruct((B, 16, 256), F32),
    )
    out_specs = (
        row_spec(512), row_spec(256), row_spec(512), row_spec(256), row_spec(256), row_spec(256),
        row_spec(LANES), row_spec(LANES), row_spec(LANES),
        blk_spec(KEY_BLK, LANES), blk_spec(KEY_BLK, LANES), blk_spec(KEY_BLK, 256),
        blk_spec(64, KEY_BLK),
        pl.BlockSpec((1, 8, TS), lambda b, t: (b, 0, t)),
        blk_spec(256, KEY_BLK),
        pl.BlockSpec((1, 8, 256), lambda b, t: (b, 0, 0)),
        pl.BlockSpec((1, 16, 256), lambda b, t: (b, 0, 0)),
    )
    in_specs = [
        row_spec(D), _const_spec(g.shape), _const_spec(w.shape), _const_spec(ca.shape),
        _const_spec(pm.shape), _const_spec(ps.shape),
    ]
    return pl.pallas_call(
        functools.partial(_k1p_kernel, TS=TS),
        grid=(B, nt), in_specs=in_specs, out_specs=out_specs, out_shape=out_shape,
        scratch_shapes=[pltpu.VMEM((TS + 8, GROUP_W), F32), pltpu.VMEM((TS + 16, GROUP_W), F32)],
        compiler_params=pltpu.CompilerParams(dimension_semantics=("arbitrary", "arbitrary"),
                                             vmem_limit_bytes=VMEM_LIMIT),
        name="mixer_in_prompt",
    )(x, g, w, ca, pm, ps)


def _k2p_kernel(cq_ref, cqi_ref, cwiT_ref, dq_ref, kkb_ref, iib_ref, cvT_ref, dkb_ref, dvT_ref,
                lam_ref, subln_ref, out_ref, key_ref, *, n_sel, lam_init):
    qb = pl.program_id(1)
    q0 = qb * Q_BLK
    nkb = (q0 + Q_BLK + KEY_BLK - 1) // KEY_BLK
    qpos = q0 + lax.broadcasted_iota(I32, (KEY_BLK, Q_BLK), 1)
    krow = lax.broadcasted_iota(I32, (KEY_BLK, Q_BLK), 0)
    lane_q = lax.broadcasted_iota(I32, (Q_BLK, LANES), 1)
    lo_half = lane_q < 64

    def half_masked(tile, h):
        return jnp.where(lo_half if h % 2 == 0 else jnp.logical_not(lo_half), tile, jnp.zeros_like(tile))

    cqi = cqi_ref[0]
    wT = cwiT_ref[0]
    qi = [half_masked(cqi[:, (h // 2) * LANES:(h // 2 + 1) * LANES], h) for h in range(IDX_HEADS)]

    def idx_body(kb, carry):
        ki = iib_ref[0, kb]
        acc = jnp.zeros((KEY_BLK, Q_BLK), F32)
        for h in range(IDX_HEADS):
            s = _dot_nt(ki, qi[h])
            acc = acc + jnp.maximum(s, 0.0) * wT[h:h + 1]
        score = acc * IDX_SCALE + 0.0
        causal = (kb * KEY_BLK + krow) <= qpos
        key_ref[kb] = jnp.where(causal, _float_key(score), INT_MIN)
        return carry

    lax.fori_loop(0, nkb, idx_body, 0)

    def count(pred):
        def body(kb, acc):
            c = jnp.where(pred(kb, key_ref[kb]), 1, 0)
            return acc + jnp.sum(c.reshape(KEY_BLK // SUBLANES, SUBLANES, Q_BLK), axis=0)
        acc = lax.fori_loop(0, nkb, body, jnp.zeros((SUBLANES, Q_BLK), I32))
        return jnp.sum(acc, axis=0, keepdims=True)

    def search_body(it, base):
        cand = base + lax.shift_left(jnp.int32(1), 31 - it)
        cnt = count(lambda kb, k: k >= cand)
        return jnp.where(cnt >= n_sel, cand, base)

    thr = lax.fori_loop(0, 32, search_body, jnp.full((1, Q_BLK), INT_MIN, I32))

    cnt_gt = count(lambda kb, k: k > thr)
    cnt_eq = count(lambda kb, k: k == thr)
    need = n_sel - cnt_gt
    excess = jnp.logical_and(cnt_eq > need, thr > INT_MIN)
    any_excess = jnp.max(jnp.where(excess, 1, 0))

    @pl.when(any_excess > 0)
    def _():
        def tie_body(it, jp):
            c = jp + lax.shift_left(jnp.int32(1), 10 - it)
            f = count(lambda kb, k: jnp.logical_and(k == thr, (kb * KEY_BLK + krow) < c))
            return jnp.where(f < need, c, jp)
        jp = lax.fori_loop(0, 11, tie_body, jnp.zeros((1, Q_BLK), I32))

        def demote(kb, carry):
            k = key_ref[kb]
            drop = jnp.logical_and(jnp.logical_and(k == thr, (kb * KEY_BLK + krow) > jp), excess)
            key_ref[kb] = jnp.where(drop, thr - 1, k)
            return carry
        lax.fori_loop(0, nkb, demote, 0)

    thr_sel = jnp.maximum(thr, INT_MIN + 1)

    cq = cq_ref[0]
    qc = [half_masked(cq[:, (h // 2) * LANES:(h // 2 + 1) * LANES], h) for h in range(DSA_HEADS)]

    def dsa_body(kb, carry):
        kk = kkb_ref[0, kb]
        vT = cvT_ref[0, kb]
        sel = key_ref[kb] >= thr_sel
        new = []
        for h in range(DSA_HEADS):
            m, l, acc = carry[h]
            lg = jnp.where(sel, _dot_nt(kk, qc[h]) * (DSA_HD ** -0.5), NEG)
            m_new = jnp.maximum(m, jnp.max(lg, axis=0, keepdims=True))
            alpha = jnp.exp(m - m_new)
            p = jnp.exp(lg - m_new)
            l = l * alpha + jnp.sum(p, axis=0, keepdims=True)
            acc = acc * alpha + _dot(vT, p.astype(BF16))
            new.append((m_new, l, acc))
        return tuple(new)

    init = tuple((jnp.full((1, Q_BLK), NEG, F32), jnp.zeros((1, Q_BLK), F32), jnp.zeros((DSA_HD, Q_BLK), F32))
                 for _ in range(DSA_HEADS))
    res = lax.fori_loop(0, nkb, dsa_body, init)
    ycT = jnp.concatenate([acc / l for (_, l, acc) in res], axis=0)
    out_ref[0, :, 0:256] = ycT.T

    dq = dq_ref[0]
    lane256 = lax.broadcasted_iota(I32, (Q_BLK, 256), 1)
    qd = []
    for h in range(DIFF_HEADS):
        for mm in range(2):
            c0 = h * 64 + mm * DIFF_QD
            qd.append(jnp.where(jnp.logical_and(lane256 >= c0, lane256 < c0 + DIFF_QD), dq, jnp.zeros_like(dq)))

    def diff_body(kb, carry):
        kd = dkb_ref[0, kb]
        vT = dvT_ref[0, kb]
        causal = (kb * KEY_BLK + krow) <= qpos
        new = []
        for i in range(2 * DIFF_HEADS):
            h = i // 2
            m, l, acc = carry[i]
            lg = jnp.where(causal, _dot_nt(kd, qd[i]) * (DIFF_QD ** -0.5), NEG)
            m_new = jnp.maximum(m, jnp.max(lg, axis=0, keepdims=True))
            alpha = jnp.exp(m - m_new)
            p = jnp.exp(lg - m_new)
            l = l * alpha + jnp.sum(p, axis=0, keepdims=True)
            acc = acc * alpha + _dot(vT[h * 64:(h + 1) * 64], p.astype(BF16))
            new.append((m_new, l, acc))
        return tuple(new)

    init = tuple((jnp.full((1, Q_BLK), NEG, F32), jnp.zeros((1, Q_BLK), F32), jnp.zeros((DIFF_VD, Q_BLK), F32))
                 for _ in range(2 * DIFF_HEADS))
    res = lax.fori_loop(0, nkb, diff_body, init)
    lam = _lambda_full(lam_ref, lam_init)
    subln = subln_ref[...]
    outs = []
    for h in range(DIFF_HEADS):
        _, l1, a1 = res[2 * h]
        _, l2, a2 = res[2 * h + 1]
        o = a1 / l1 - lam * (a2 / l2)
        ms = jnp.mean(o * o, axis=0, keepdims=True)
        outs.append((o * lax.rsqrt(ms + EPS)) * subln * (1.0 - lam_init))
    out_ref[0, :, 256:512] = jnp.concatenate(outs, axis=0).T


def _k2p(cq, cqi, cwiT, dq, kkb, iib, cvT, dkb, dvT, lam, subln, *, n_sel, lam_init):
    B, T, _ = cq.shape
    nq = T // Q_BLK
    nkb = T // KEY_BLK

    def q_spec(c):
        return pl.BlockSpec((1, Q_BLK, c), lambda b, q: (b, q, 0))

    def seq_spec(r, c):
        return pl.BlockSpec((1, nkb, r, c), lambda b, q: (b, 0, 0, 0))

    in_specs = [
        q_spec(256), q_spec(512), pl.BlockSpec((1, 8, Q_BLK), lambda b, q: (b, 0, q)), q_spec(256),
        seq_spec(KEY_BLK, LANES), seq_spec(KEY_BLK, LANES), seq_spec(64, KEY_BLK),
        seq_spec(KEY_BLK, 256), seq_spec(256, KEY_BLK),
        _const_spec(lam.shape), _const_spec(subln.shape),
    ]
    return pl.pallas_call(
        functools.partial(_k2p_kernel, n_sel=n_sel, lam_init=lam_init),
        grid=(B, nq), in_specs=in_specs, out_specs=q_spec(512),
        out_shape=jax.ShapeDtypeStruct((B, T, 512), F32),
        scratch_shapes=[pltpu.VMEM((nkb, KEY_BLK, Q_BLK), I32)],
        compiler_params=pltpu.CompilerParams(dimension_semantics=("arbitrary", "arbitrary"),
                                             vmem_limit_bytes=VMEM_LIMIT),
        name="attention_prompt",
    )(cq, cqi, cwiT, dq, kkb, iib, cvT, dkb, dvT, lam, subln)


def _ple_tail(h2, p_bf, gple, wpg, wple, gfin, final_norm):
    gate = _sigmoid(_dot(_rms(h2, gple).astype(BF16), wpg))
    h3 = h2 + _dot(p_bf, wple) * gate
    if final_norm:
        h3 = _rms(h3, gfin)
    return h3


def _k3p_kernel(h_ref, ab_ref, cd_ref, p_ref, wout_ref, gffn_ref, wg_ref, wu_ref, cf_ref, wd_ref,
                gple_ref, wpg_ref, wple_ref, gfin_ref, hout_ref, ffnst_ref, extg_ref, *, TS, final_norm):
    t = pl.program_id(1)

    @pl.when(t == 0)
    def _():
        extg_ref[0:8, :] = jnp.zeros((8, extg_ref.shape[1]), F32)

    mix = jnp.concatenate([ab_ref[0], cd_ref[0]], axis=1).astype(BF16)
    h1 = h_ref[0] + _dot(mix, wout_ref[...])
    xn2 = _rms(h1, gffn_ref[...]).astype(BF16)
    g = _dot(xn2, wg_ref[...])
    extg_ref[8:8 + TS, :] = g
    p1 = extg_ref[7:7 + TS, :]
    p2 = extg_ref[6:6 + TS, :]
    cf = cf_ref[...]
    gc = p2 * cf[0:1] + p1 * cf[1:2] + g * cf[2:3]
    extg_ref[0:8, :] = g[TS - 8:TS]
    ffnst_ref[0] = g[TS - 8:TS]
    act = (gc * _sigmoid(gc)) * _dot(xn2, wu_ref[...])
    h2 = h1 + _dot(act.astype(BF16), wd_ref[...])
    hout_ref[0] = _ple_tail(h2, p_ref[0].astype(BF16), gple_ref[...], wpg_ref[...], wple_ref[...],
                            gfin_ref[...], final_norm)


def _k3p(h, ab, cd, p, wout, gffn, wg, wu, cf, wd, gple, wpg, wple, gfin, *, TS, final_norm):
    B, T, D = h.shape
    dff = wg.shape[1]
    nt = T // TS

    def row_spec(c):
        return pl.BlockSpec((1, TS, c), lambda b, t: (b, t, 0))

    def w_spec(a):
        nd = a.ndim
        return pl.BlockSpec(a.shape, lambda *_: (0,) * nd, pipeline_mode=pl.Buffered(1))

    in_specs = [row_spec(D), row_spec(512), row_spec(512), row_spec(p.shape[2]),
                w_spec(wout), w_spec(gffn), w_spec(wg), w_spec(wu), w_spec(cf), w_spec(wd),
                w_spec(gple), w_spec(wpg), w_spec(wple), w_spec(gfin)]
    out_specs = (row_spec(D), pl.BlockSpec((1, 8, dff), lambda b, t: (b, 0, 0)))
    out_shape = (jax.ShapeDtypeStruct((B, T, D), F32), jax.ShapeDtypeStruct((B, 8, dff), F32))
    return pl.pallas_call(
        functools.partial(_k3p_kernel, TS=TS, final_norm=final_norm),
        grid=(B, nt), in_specs=in_specs, out_specs=out_specs, out_shape=out_shape,
        scratch_shapes=[pltpu.VMEM((TS + 8, dff), F32)],
        compiler_params=pltpu.CompilerParams(dimension_semantics=("arbitrary", "arbitrary"),
                                             vmem_limit_bytes=VMEM_LIMIT),
        name="channel_mix_prompt",
    )(h, ab, cd, p, wout, gffn, wg, wu, cf, wd, gple, wpg, wple, gfin)


def _k1s_kernel(x_ref, g_ref, w_ref, ca_ref, pm_ref, ps_ref, sconv_ref, spool_ref,
                mixab_ref, cq_ref, cqi_ref, dq_ref, dk_ref, dv_ref, skk_ref, sii_ref, svw_ref,
                u_ref, pv_ref, *, DB, TD, pos0):
    xn = _rms(x_ref[...], g_ref[...])
    z = _dot(xn.astype(BF16), w_ref[...])

    def slab(t, c0, c1):
        return z[t * DB:(t + 1) * DB, c0:c1]

    ca = ca_ref[...]
    us = [slab(t, C_GC, C_GC + 256) * slab(t, C_GH, C_GH + 256) for t in range(TD)]
    extu = [sconv_ref[0], sconv_ref[1]] + us
    pvs = [slab(t, C_PV, C_PV + 256) for t in range(TD)]
    extx = [spool_ref[j] for j in range(POOL_BUF)] + pvs
    lo = lax.broadcasted_iota(I32, (DB, LANES), 1) < 64
    pm = pm_ref[...]
    ps = ps_ref[...]
    for t in range(TD):
        conv = extu[t] * ca[0:1] + extu[t + 1] * ca[1:2] + extu[t + 2] * ca[2:3]
        mixab_ref[t * DB:(t + 1) * DB, 0:256] = slab(t, C_GB, C_GB + 256) * conv
        u_ref[t] = us[t]
        pv_ref[t] = pvs[t]
        e = POOL_BUF + t

        def win(n, c0):
            s = extx[e][:, c0:c0 + LANES]
            for j in range(1, n):
                s = s + extx[e - j][:, c0:c0 + LANES]
            return s

        cnt = [float(min(w, pos0 + t + 1)) for w in POOL_WINDOWS]
        d0 = jnp.where(lo, win(2, 0) / cnt[0], win(4, 0) / cnt[1]) - pvs[t][:, 0:LANES]
        d1 = jnp.where(lo, win(8, LANES) / cnt[2], win(16, LANES) / cnt[3]) - pvs[t][:, LANES:]
        d = jnp.concatenate([d0, d1], axis=1)
        mixab_ref[t * DB:(t + 1) * DB, 256:512] = _dot(d.astype(BF16), pm) * ps

    cq_ref[...] = z[:, C_CQ:C_CQ + 256].astype(BF16)
    cqi_ref[...] = z[:, C_CQI:C_CQI + 512].astype(BF16)
    dq_ref[...] = z[:, C_DQ:C_DQ + 256].astype(BF16)
    dk_ref[...] = z[:, C_DK:C_DK + 256]
    dv_ref[...] = z[:, C_DV:C_DV + 256]
    skk_ref[...] = z[:, C_KK:C_KK + LANES]
    sii_ref[...] = z[:, C_II:C_II + LANES]
    svw_ref[...] = z[:, C_VW:C_VW + LANES]


def _k1s(x, g, w, ca, pm, ps, sconv, spool, *, DB, TD, pos0):
    R = x.shape[0]
    out_shape = (
        jax.ShapeDtypeStruct((R, 512), F32), jax.ShapeDtypeStruct((R, 256), BF16),
        jax.ShapeDtypeStruct((R, 512), BF16), jax.ShapeDtypeStruct((R, 256), BF16),
        jax.ShapeDtypeStruct((R, 256), F32), jax.ShapeDtypeStruct((R, 256), F32),
        jax.ShapeDtypeStruct((R, LANES), F32), jax.ShapeDtypeStruct((R, LANES), F32),
        jax.ShapeDtypeStruct((R, LANES), F32),
        jax.ShapeDtypeStruct((TD, DB, 256), F32), jax.ShapeDtypeStruct((TD, DB, 256), F32),
    )
    args = (x, g, w, ca, pm, ps, sconv, spool)
    return pl.pallas_call(
        functools.partial(_k1s_kernel, DB=DB, TD=TD, pos0=pos0),
        grid=(1,), in_specs=[_const_spec(a.shape) for a in args],
        out_specs=tuple(_const_spec(s.shape) for s in out_shape), out_shape=out_shape,
        compiler_params=pltpu.CompilerParams(dimension_semantics=("arbitrary",), vmem_limit_bytes=VMEM_LIMIT),
        name="mixer_in_sample",
    )(*args)


QP = 8


def _k2s_kernel(pt_ref, qi_ref, wi_ref, qc_ref, qd_ref, kin_ref, kn_ref, vn_ref, dkn_ref, dvn_ref,
                bias_ref, biasn_ref, lam_ref, subln_ref, *rest, NP, PS, TD, n_sel, lam_init):
    idx_refs = rest[0:NP]
    k_refs = rest[NP:2 * NP]
    v_refs = rest[2 * NP:3 * NP]
    dk_refs = rest[3 * NP:4 * NP]
    dv_refs = rest[4 * NP:5 * NP]
    out_ref, key_ref, clog_ref, dlog_ref = rest[5 * NP:]
    del pt_ref
    SC = NP * PS + LANES
    DPS = DIFF_HEADS * PS
    SD = NP * DPS + LANES

    qi = qi_ref[0]
    wi = wi_ref[0]
    col = lax.broadcasted_iota(I32, (QP, LANES), 1)
    qrow = lax.broadcasted_iota(I32, (QP, LANES), 0)
    new_ok = jnp.logical_and(col < TD, col <= qrow)
    for p in range(NP + 1):
        kpage = (idx_refs[p][0, 0] if p < NP else kin_ref[0]).astype(BF16)
        s = _dot_nt(qi, kpage)
        r = (jnp.maximum(s, 0.0) * wi).reshape(IDX_HEADS, QP, PS)
        score = jnp.sum(r, axis=0) * IDX_SCALE + 0.0
        key = _float_key(score)
        if p == NP:
            key = jnp.where(new_ok, key, INT_MIN)
        key_ref[:, p * PS:(p + 1) * PS] = key

    kcol = lax.broadcasted_iota(I32, (QP, SC), 1)

    def count(pred):
        c = jnp.where(pred(key_ref[...]), 1.0, 0.0)
        return jnp.sum(c, axis=1, keepdims=True)

    def search_body(it, base):
        cand = base + lax.shift_left(jnp.int32(1), 31 - it)
        cnt = count(lambda k: k >= cand)
        return jnp.where(cnt >= n_sel, cand, base)

    thr = lax.fori_loop(0, 32, search_body, jnp.full((QP, 1), INT_MIN, I32))
    cnt_gt = count(lambda k: k > thr)
    cnt_eq = count(lambda k: k == thr)
    need = n_sel - cnt_gt
    real_q = lax.broadcasted_iota(I32, (QP, 1), 0) < TD
    excess = jnp.logical_and(jnp.logical_and(cnt_eq > need, thr > INT_MIN), real_q)
    any_excess = jnp.max(jnp.where(excess, 1, 0))

    @pl.when(any_excess > 0)
    def _():
        def tie_body(it, jp):
            c = jp + lax.shift_left(jnp.int32(1), 12 - it)
            f = count(lambda k: jnp.logical_and(k == thr, kcol < c))
            return jnp.where(f < need, c, jp)
        jp = lax.fori_loop(0, 13, tie_body, jnp.zeros((QP, 1), I32))
        k = key_ref[...]
        drop = jnp.logical_and(jnp.logical_and(k == thr, kcol > jp), excess)
        key_ref[...] = jnp.where(drop, thr - 1, k)

    thr_sel = jnp.maximum(thr, INT_MIN + 1)
    sel = key_ref[...] >= thr_sel

    qc = qc_ref[0]
    for p in range(NP + 1):
        kpage = (k_refs[p][0, 0] if p < NP else kn_ref[0]).astype(BF16)
        clog_ref[:, p * PS:(p + 1) * PS] = _dot_nt(qc, kpage) * (DSA_HD ** -0.5)
    lg = jnp.where(sel[None], clog_ref[...].reshape(DSA_HEADS, QP, SC), NEG)
    lg = lg.reshape(DSA_HEADS * QP, SC)
    m = jnp.max(lg, axis=1, keepdims=True)
    pe = jnp.exp(lg - m)
    l = jnp.sum(pe, axis=1, keepdims=True)
    clog_ref[...] = pe
    acc = jnp.zeros((DSA_HEADS * QP, DSA_HD), F32)
    for p in range(NP + 1):
        vpage = (v_refs[p][0, 0] if p < NP else vn_ref[0]).astype(BF16)
        acc = acc + _dot(clog_ref[:, p * PS:(p + 1) * PS].astype(BF16), vpage)
    oc = acc / l
    for h in range(DSA_HEADS):
        out_ref[0, :, h * 64:(h + 1) * 64] = oc[h * QP:(h + 1) * QP]

    qd0 = qd_ref[0]
    lane64 = lax.broadcasted_iota(I32, qd0.shape, 1)
    qd = jnp.concatenate([jnp.where(lane64 < DIFF_QD, qd0, jnp.zeros_like(qd0)),
                          jnp.where(lane64 >= DIFF_QD, qd0, jnp.zeros_like(qd0))], axis=0)
    bias = bias_ref[...]
    bias_pg = jnp.concatenate([bias] * (DPS // LANES), axis=1)
    for p in range(NP):
        kd = dk_refs[p][0, 0].astype(BF16)
        dlog_ref[:, p * DPS:(p + 1) * DPS] = _dot_nt(qd, kd) * (DIFF_QD ** -0.5) + bias_pg
    dlog_ref[:, NP * DPS:] = _dot_nt(qd, dkn_ref[0].astype(BF16)) * (DIFF_QD ** -0.5) + biasn_ref[...]
    lgd = dlog_ref[...]
    md = jnp.max(lgd, axis=1, keepdims=True)
    ped = jnp.exp(lgd - md)
    ld = jnp.sum(ped, axis=1, keepdims=True)
    dlog_ref[...] = ped
    accd = jnp.zeros((2 * DIFF_HEADS * QP, DIFF_VD), F32)
    for p in range(NP):
        vd = dv_refs[p][0, 0].astype(BF16)
        accd = accd + _dot(dlog_ref[:, p * DPS:(p + 1) * DPS].astype(BF16), vd)
    accd = accd + _dot(dlog_ref[:, NP * DPS:].astype(BF16), dvn_ref[0].astype(BF16))
    od = accd / ld
    half = DIFF_HEADS * QP
    lam = _lambda_full(lam_ref, lam_init)
    o = od[0:half] - lam * od[half:]
    ms = jnp.mean(o * o, axis=-1, keepdims=True)
    yd = (o * lax.rsqrt(ms + EPS)) * subln_ref[...] * (1.0 - lam_init)
    for h in range(DIFF_HEADS):
        out_ref[0, :, 256 + h * 64:256 + (h + 1) * 64] = yd[h * QP:(h + 1) * QP]


def _k2s(page_table, qi, wi, qc, qd, kin, kn, vn, dkn, dvn, bias, biasn, lam, subln,
         c_idx, c_k, c_v, d_k, d_v, *, layer, TD, n_sel, lam_init):
    DB, NP = page_table.shape
    PS = c_k.shape[2]
    DPS = d_k.shape[2]

    def b_spec(a):
        return pl.BlockSpec((1,) + a.shape[1:], lambda b, pt: (b,) + (0,) * (a.ndim - 1))

    def c_spec(a):
        nd = a.ndim
        return pl.BlockSpec(a.shape, lambda b, pt: (0,) * nd)

    def page_spec(rows, p):
        return pl.BlockSpec((1, 1, rows, 64), lambda b, pt: (pt[b, p], layer, 0, 0))

    in_specs = [b_spec(qi), b_spec(wi), b_spec(qc), b_spec(qd), b_spec(kin), b_spec(kn), b_spec(vn),
                b_spec(dkn), b_spec(dvn), c_spec(bias), c_spec(biasn), c_spec(lam), c_spec(subln)]
    in_specs += [page_spec(PS, p) for p in range(NP)] * 3
    in_specs += [page_spec(DPS, p) for p in range(NP)] * 2
    SC = NP * PS + LANES
    SD = NP * DPS + LANES
    grid_spec = pltpu.PrefetchScalarGridSpec(
        num_scalar_prefetch=1, grid=(DB,), in_specs=in_specs,
        out_specs=pl.BlockSpec((1, QP, 512), lambda b, pt: (b, 0, 0)),
        scratch_shapes=[pltpu.VMEM((QP, SC), I32), pltpu.VMEM((DSA_HEADS * QP, SC), F32),
                        pltpu.VMEM((2 * DIFF_HEADS * QP, SD), F32)],
    )
    args = [page_table, qi, wi, qc, qd, kin, kn, vn, dkn, dvn, bias, biasn, lam, subln]
    args += [c_idx] * NP + [c_k] * NP + [c_v] * NP + [d_k] * NP + [d_v] * NP
    return pl.pallas_call(
        functools.partial(_k2s_kernel, NP=NP, PS=PS, TD=TD, n_sel=n_sel, lam_init=lam_init),
        grid_spec=grid_spec, out_shape=jax.ShapeDtypeStruct((DB, QP, 512), F32),
        compiler_params=pltpu.CompilerParams(dimension_semantics=("arbitrary",), vmem_limit_bytes=VMEM_LIMIT),
        name="attention_sample",
    )(*args)


FF_TILE = 256


def _k3s_kernel(h_ref, ab_ref, cd_ref, p_ref, wout_ref, gffn_ref, wg_ref, wu_ref, cf_ref, wd_ref,
                gple_ref, wpg_ref, wple_ref, gfin_ref, sffn_ref, hout_ref, gout_ref,
                h1_ref, xn2_ref, acc_ref, *, DB, TD, final_norm):
    f = pl.program_id(0)

    @pl.when(f == 0)
    def _():
        mix = jnp.concatenate([ab_ref[...], cd_ref[...]], axis=1).astype(BF16)
        h1 = h_ref[...] + _dot(mix, wout_ref[...])
        h1_ref[...] = h1
        xn2_ref[...] = _rms(h1, gffn_ref[...]).astype(BF16)
        acc_ref[...] = jnp.zeros(acc_ref.shape, F32)

    xn2 = xn2_ref[...]
    g = _dot(xn2, wg_ref[...])
    cf = cf_ref[...]
    gs = [sffn_ref[0], sffn_ref[1]] + [g[t * DB:(t + 1) * DB] for t in range(TD)]
    gc = jnp.concatenate([gs[t] * cf[0:1] + gs[t + 1] * cf[1:2] + gs[t + 2] * cf[2:3] for t in range(TD)], axis=0)
    act = (gc * _sigmoid(gc)) * _dot(xn2, wu_ref[...])
    acc_ref[...] += _dot(act.astype(BF16), wd_ref[...])
    gout_ref[0] = gs[TD]
    gout_ref[1] = gs[TD + 1]

    @pl.when(f == pl.num_programs(0) - 1)
    def _():
        h2 = h1_ref[...] + acc_ref[...]
        hout_ref[...] = _ple_tail(h2, p_ref[...].astype(BF16), gple_ref[...], wpg_ref[...], wple_ref[...],
                                  gfin_ref[...], final_norm)


def _k3s(h, ab, cd, p, wout, gffn, wg, wu, cf, wd, gple, wpg, wple, gfin, sffn, *, DB, TD, final_norm):
    R, D = h.shape
    dff = wg.shape[1]
    nf = dff // FF_TILE
    cs = _const_spec
    in_specs = [cs(h.shape), cs(ab.shape), cs(cd.shape), cs(p.shape), cs(wout.shape), cs(gffn.shape),
                pl.BlockSpec((D, FF_TILE), lambda f: (0, f)), pl.BlockSpec((D, FF_TILE), lambda f: (0, f)),
                pl.BlockSpec((3, FF_TILE), lambda f: (0, f)), pl.BlockSpec((FF_TILE, D), lambda f: (f, 0)),
                cs(gple.shape), cs(wpg.shape), cs(wple.shape), cs(gfin.shape),
                pl.BlockSpec((2, DB, FF_TILE), lambda f: (0, 0, f))]
    out_specs = (cs((R, D)), pl.BlockSpec((2, DB, FF_TILE), lambda f: (0, 0, f)))
    out_shape = (jax.ShapeDtypeStruct((R, D), F32), jax.ShapeDtypeStruct((2, DB, dff), F32))
    return pl.pallas_call(
        functools.partial(_k3s_kernel, DB=DB, TD=TD, final_norm=final_norm),
        grid=(nf,), in_specs=in_specs, out_specs=out_specs, out_shape=out_shape,
        scratch_shapes=[pltpu.VMEM((R, D), F32), pltpu.VMEM((R, D), BF16), pltpu.VMEM((R, D), F32)],
        compiler_params=pltpu.CompilerParams(dimension_semantics=("arbitrary",), vmem_limit_bytes=VMEM_LIMIT),
        name="channel_mix_sample",
    )(h, ab, cd, p, wout, gffn, wg, wu, cf, wd, gple, wpg, wple, gfin, sffn)


def _prep_w_in(w_in):
    cuts = np.cumsum(SPLIT_SIZES)[:-1].tolist()
    gb, gc, gh, pv, cq, ck, cv, cqi, cwi, cki, dq, dk, dv = jnp.split(w_in, cuts, axis=-1)
    pad = jnp.zeros(w_in.shape[:-1] + (LANES - 64 - IDX_HEADS,), w_in.dtype)
    cols = [gb, gc, gh, pv, cq, cqi, dq, dk, dv, ck, ck, cki, cki, cv, cwi, pad]
    return jnp.concatenate(cols, axis=-1).astype(BF16)


def _block_diag_maps(pool_maps):
    depth, G, c, _ = pool_maps.shape
    out = jnp.zeros((depth, G * c, G * c), pool_maps.dtype)
    for g in range(G):
        out = out.at[:, g * c:(g + 1) * c, g * c:(g + 1) * c].set(pool_maps[:, g])
    return out.astype(BF16)


def _sample_diff_bias(TD, PS):
    r = np.arange(2 * DIFF_HEADS * QP)
    row_h = (r % (DIFF_HEADS * QP)) // QP
    row_q = r % QP
    c = np.arange(LANES)
    col_h = c % DIFF_HEADS
    col_key = c // DIFF_HEADS
    same = row_h[:, None] == col_h[None, :]
    past = np.where(same, 0.0, NEG).astype(np.float32)
    new_ok = same & (col_key[None, :] < TD) & (col_key[None, :] <= row_q[:, None])
    new = np.where(new_ok, 0.0, NEG).astype(np.float32)
    return jnp.asarray(past), jnp.asarray(new)


def _heads_rows(a, DB, TD, nh):
    a = a.reshape(TD, DB, nh, 64).transpose(1, 2, 0, 3)
    a = jnp.pad(a, ((0, 0), (0, 0), (0, QP - TD), (0, 0)))
    return a.reshape(DB, nh * QP, 64)


def _new_page(a, DB, TD, rows_per_key, PS):
    a = a.reshape(TD, DB, rows_per_key, 64).transpose(1, 0, 2, 3).reshape(DB, TD * rows_per_key, 64)
    return jnp.pad(a, ((0, 0), (0, PS - TD * rows_per_key), (0, 0)))


def kernel(x_prompt, x_sample, state_conv_a, state_pool, state_ffn, cache_c_k, cache_c_v, cache_c_idx,
           cache_d_k, cache_d_v, page_table, p_prompt, p_sample, norm_mix, w_in, conv_a, pool_maps,
           pool_scale, diff_lambda, diff_subln, w_out, norm_ffn, w_gate, w_up, conv_ffn, w_down,
           norm_ple, w_ple_gate, w_ple, norm_final):
    depth = w_in.shape[0]
    B, T, D = x_prompt.shape
    DB, TD, _ = x_sample.shape
    NP = page_table.shape[1]
    PS = cache_c_k.shape[2]
    past = NP * PS
    dff = w_gate.shape[2]
    TS1 = min(512, T)
    TS3 = min(256, T)

    w_in_p = _prep_w_in(w_in)
    pm_bd = _block_diag_maps(pool_maps)
    w_out_b, w_gate_b, w_up_b, w_down_b = (w.astype(BF16) for w in (w_out, w_gate, w_up, w_down))
    w_pg_b, w_ple_b = w_ple_gate.astype(BF16), w_ple.astype(BF16)
    gfin = norm_final.reshape(1, D)
    cache_d_k4 = cache_d_k.reshape(cache_d_k.shape[0], depth, PS * DIFF_HEADS, 64)
    cache_d_v4 = cache_d_v.reshape(cache_d_v.shape[0], depth, PS * DIFF_HEADS, 64)
    bias_past, bias_new = _sample_diff_bias(TD, PS)

    def row(a, i):
        return a[i].reshape(1, -1)

    h = x_prompt
    st_p = []
    n_sel_p = min(TOPK_MAX, T // 4)
    for i in range(depth):
        lam_init = 0.8 - 0.6 * math.exp(-0.3 * i)
        (mixab, cq, cqi, dq, dk, dv, skk, sii, svw, kkb, iib, dkb, cvT, cwiT, dvT, convst, poolst) = _k1p(
            h, row(norm_mix, i), w_in_p[i], conv_a[i], pm_bd[i], row(pool_scale, i), TS=TS1)
        mixcd = _k2p(cq, cqi, cwiT, dq, kkb, iib, cvT, dkb, dvT, diff_lambda[i], diff_subln[i].reshape(-1, 1),
                     n_sel=n_sel_p, lam_init=lam_init)
        h, ffnst = _k3p(h, mixab, mixcd, p_prompt[i], w_out_b[i], row(norm_ffn, i), w_gate_b[i], w_up_b[i],
                        conv_ffn[i], w_down_b[i], row(norm_ple, i), w_pg_b[i], w_ple_b[i], gfin,
                        TS=TS3, final_norm=(i == depth - 1))
        st_p.append((convst[:, 6:8], poolst[:, 1:16], ffnst[:, 6:8], skk[..., :64], svw[..., :64], sii[..., :64],
                     dk.reshape(B, T, DIFF_HEADS, 64), dv.reshape(B, T, DIFF_HEADS, 64)))
    y_prompt = h

    hs = x_sample.transpose(1, 0, 2).reshape(TD * DB, D)
    st_s = []
    n_sel_s = min(TOPK_MAX, (past + TD) // 4)
    for i in range(depth):
        lam_init = 0.8 - 0.6 * math.exp(-0.3 * i)
        sconv = state_conv_a[i].transpose(1, 0, 2)
        spool = state_pool[i].transpose(1, 0, 2)
        sffn = state_ffn[i].transpose(1, 0, 2)
        (mixab, cq, cqi, dq, dk, dv, skk, sii, svw, u_new, pv_new) = _k1s(
            hs, row(norm_mix, i), w_in_p[i], conv_a[i], pm_bd[i], row(pool_scale, i), sconv, spool,
            DB=DB, TD=TD, pos0=past)
        qi = _heads_rows(cqi, DB, TD, IDX_HEADS)
        wi = svw[:, 64:64 + IDX_HEADS].reshape(TD, DB, IDX_HEADS).transpose(1, 2, 0)
        wi = jnp.pad(wi, ((0, 0), (0, 0), (0, QP - TD))).reshape(DB, IDX_HEADS * QP, 1)
        wi = jnp.broadcast_to(wi, (DB, IDX_HEADS * QP, LANES))
        qc = _heads_rows(cq, DB, TD, DSA_HEADS)
        qd = _heads_rows(dq, DB, TD, DIFF_HEADS)
        kin = _new_page(sii[:, :64], DB, TD, 1, PS)
        kn = _new_page(skk[:, :64], DB, TD, 1, PS)
        vn = _new_page(svw[:, :64], DB, TD, 1, PS)
        dkn = _new_page(dk, DB, TD, DIFF_HEADS, LANES)
        dvn = _new_page(dv, DB, TD, DIFF_HEADS, LANES)
        ycd = _k2s(page_table, qi, wi, qc, qd, kin, kn, vn, dkn, dvn, bias_past, bias_new,
                   diff_lambda[i], row(diff_subln, i), cache_c_idx, cache_c_k, cache_c_v, cache_d_k4, cache_d_v4,
                   layer=i, TD=TD, n_sel=n_sel_s, lam_init=lam_init)
        mixcd = ycd[:, :TD].transpose(1, 0, 2).reshape(TD * DB, 512)
        hs, g_new = _k3s(hs, mixab, mixcd, p_sample[i].transpose(1, 0, 2).reshape(TD * DB, -1), w_out_b[i],
                         row(norm_ffn, i), w_gate_b[i], w_up_b[i], conv_ffn[i], w_down_b[i], row(norm_ple, i),
                         w_pg_b[i], w_ple_b[i], gfin, sffn, DB=DB, TD=TD, final_norm=(i == depth - 1))

        def bm(a, width):
            return a[:, :width].reshape(TD, DB, width).transpose(1, 0, 2)

        new_conv = u_new[TD - 2:].transpose(1, 0, 2)
        new_pool = jnp.concatenate([state_pool[i], pv_new.transpose(1, 0, 2)], axis=1)[:, -POOL_BUF:]
        st_s.append((new_conv, new_pool, g_new.transpose(1, 0, 2), bm(skk, 64), bm(svw, 64), bm(sii, 64),
                     bm(dk, 256).reshape(DB, TD, DIFF_HEADS, 64), bm(dv, 256).reshape(DB, TD, DIFF_HEADS, 64)))
    y_sample = hs.reshape(TD, DB, D).transpose(1, 0, 2)

    def col(outs, j, axis):
        return jnp.stack([o[j] for o in outs], axis=axis)

    return (y_prompt, y_sample, col(st_p, 0, 0), col(st_s, 0, 0), col(st_p, 1, 0), col(st_s, 1, 0),
            col(st_p, 2, 0), col(st_s, 2, 0), col(st_p, 3, 1), col(st_s, 3, 1), col(st_p, 4, 1), col(st_s, 4, 1),
            col(st_p, 5, 1), col(st_s, 5, 1), col(st_p, 6, 1), col(st_s, 6, 1), col(st_p, 7, 1), col(st_s, 7, 1))
```

```python
import functools
import math

import numpy as np
import jax
import jax.numpy as jnp
from jax import lax
from jax.experimental import pallas as pl
from jax.experimental.pallas import tpu as pltpu

F32 = jnp.float32
BF16 = jnp.bfloat16
I32 = jnp.int32

EPS = 1e-6
GROUP_W = 256
POOL_WINDOWS = (2, 4, 8, 16)
POOL_BUF = 15
DSA_HEADS = 4
DSA_HD = 64
IDX_HEADS = 8
IDX_HD = 64
IDX_SCALE = IDX_HEADS ** -0.5 * IDX_HD ** -0.5
TOPK_MAX = 256
DIFF_HEADS = 4
DIFF_VD = 64
DIFF_QD = 32
SPLIT_SIZES = (256, 256, 256, 256, 256, 64, 64, 512, 8, 64, 256, 256, 256)

LANES = 128
SUBLANES = 8
VMEM_LIMIT = 56 * 1024 * 1024

C_GB, C_GC, C_GH, C_PV = 0, 256, 512, 768
C_CQ, C_CQI, C_DQ, C_DK, C_DV = 1024, 1280, 1792, 2048, 2304
C_KK, C_II, C_VW = 2560, 2688, 2816
D_IN_P = 2944

KEY_BLK = 256
Q_BLK = 128
V_AUG = 80
LOG2E = math.log2(math.e)
NEG = -1e30
INT_MIN = -2 ** 31

NT_DIMS = (((1,), (1,)), ((), ()))


def _dot(a, b):
    return jnp.dot(a, b, preferred_element_type=F32)


def _dot_nt(a, b):
    return lax.dot_general(a, b, NT_DIMS, preferred_element_type=F32)


def _rms(x, g):
    ms = jnp.mean(x * x, axis=-1, keepdims=True)
    return (x * lax.rsqrt(ms + EPS)) * g


def _sigmoid(x):
    return 1.0 / (1.0 + jnp.exp(-x))


def _float_key(x):
    b = lax.bitcast_convert_type(x, I32)
    return b ^ (lax.shift_right_arithmetic(b, 31) & 0x7FFFFFFF)


def _lambda_full(lam_ref, lam_init):
    lp = lam_ref[...]
    s1 = jnp.sum(lp[0:1] * lp[1:2], axis=-1, keepdims=True)
    s2 = jnp.sum(lp[2:3] * lp[3:4], axis=-1, keepdims=True)
    return jnp.exp(s1) - jnp.exp(s2) + lam_init


def _const_spec(shape):
    nd = len(shape)
    return pl.BlockSpec(shape, lambda *_: (0,) * nd)


def _k1p_kernel(x_ref, g_ref, w_ref, ca_ref, pm_ref, ps_ref,
                mixab_ref, cq_ref, cqi_ref, dq_ref, dk_ref, dv_ref, skk_ref, sii_ref, svw_ref,
                kkb_ref, iib_ref, dkb_ref, cvT_ref, cwiT_ref, dvT_ref, convst_ref, poolst_ref,
                extu_ref, extx_ref, *, TS):
    t = pl.program_id(1)

    @pl.when(t == 0)
    def _():
        extu_ref[0:8, :] = jnp.zeros((8, GROUP_W), F32)
        extx_ref[0:16, :] = jnp.zeros((16, GROUP_W), F32)

    xn = _rms(x_ref[0], g_ref[...])
    z = _dot(xn.astype(BF16), w_ref[...])

    u = z[:, C_GC:C_GC + 256] * z[:, C_GH:C_GH + 256]
    extu_ref[8:8 + TS, :] = u
    p1 = extu_ref[7:7 + TS, :]
    p2 = extu_ref[6:6 + TS, :]
    ca = ca_ref[...]
    conv = p2 * ca[0:1] + p1 * ca[1:2] + u * ca[2:3]
    mixab_ref[0, :, 0:256] = z[:, C_GB:C_GB + 256] * conv
    extu_ref[0:8, :] = u[TS - 8:TS]
    convst_ref[0] = u[TS - 8:TS]

    pv = z[:, C_PV:C_PV + 256]
    extx_ref[16:16 + TS, :] = pv

    def sh(j, c0):
        return extx_ref[16 - j:16 - j + TS, c0:c0 + LANES]

    s2 = sh(0, 0) + sh(1, 0)
    s4 = s2 + sh(2, 0) + sh(3, 0)
    s8 = sh(0, LANES)
    for j in range(1, 8):
        s8 = s8 + sh(j, LANES)
    s16 = s8
    for j in range(8, 16):
        s16 = s16 + sh(j, LANES)
    lo = lax.broadcasted_iota(I32, (TS, LANES), 1) < 64
    posp1 = (t * TS + 1 + lax.broadcasted_iota(I32, (TS, LANES), 0)).astype(F32)
    d0 = jnp.where(lo, s2, s4) / jnp.where(lo, jnp.minimum(posp1, 2.0), jnp.minimum(posp1, 4.0)) - pv[:, 0:LANES]
    d1 = jnp.where(lo, s8, s16) / jnp.where(lo, jnp.minimum(posp1, 8.0), jnp.minimum(posp1, 16.0)) - pv[:, LANES:]
    d = jnp.concatenate([d0, d1], axis=1)
    mixab_ref[0, :, 256:512] = _dot(d.astype(BF16), pm_ref[...]) * ps_ref[...]
    extx_ref[0:16, :] = pv[TS - 16:TS]
    poolst_ref[0] = pv[TS - 16:TS]

    cq_ref[0] = z[:, C_CQ:C_CQ + 256].astype(BF16)
    cqi_ref[0] = z[:, C_CQI:C_CQI + 512].astype(BF16)
    dq_ref[0] = z[:, C_DQ:C_DQ + 256].astype(BF16)
    dk = z[:, C_DK:C_DK + 256]
    dv = z[:, C_DV:C_DV + 256]
    skk = z[:, C_KK:C_KK + LANES]
    sii = z[:, C_II:C_II + LANES]
    svw = z[:, C_VW:C_VW + LANES]
    dk_ref[0] = dk
    dv_ref[0] = dv
    skk_ref[0] = skk
    sii_ref[0] = sii
    svw_ref[0] = svw
    nb = TS // KEY_BLK
    kkb_ref[0] = skk.astype(BF16).reshape(nb, KEY_BLK, LANES)
    iib_ref[0] = sii.astype(BF16).reshape(nb, KEY_BLK, LANES)
    dkb_ref[0] = dk.astype(BF16).reshape(nb, KEY_BLK, 256)
    svw_t = svw.T
    cwiT_ref[0] = svw_t[64:72]
    ones_rows = jnp.where(lax.broadcasted_iota(I32, (V_AUG - 64, KEY_BLK), 0) == 0, 1.0, 0.0).astype(BF16)
    for j in range(nb):
        cvT_ref[0, j] = jnp.concatenate(
            [svw_t[0:64, j * KEY_BLK:(j + 1) * KEY_BLK].astype(BF16), ones_rows], axis=0)
        dv_t = dv[j * KEY_BLK:(j + 1) * KEY_BLK].T.astype(BF16)
        for h in range(DIFF_HEADS):
            dvT_ref[0, j, h] = jnp.concatenate([dv_t[h * 64:(h + 1) * 64], ones_rows], axis=0)


def _k1p(x, g, w, ca, pm, ps, *, TS):
    B, T, D = x.shape
    nt = T // TS
    nb = TS // KEY_BLK
    nkb = T // KEY_BLK

    def row_spec(c):
        return pl.BlockSpec((1, TS, c), lambda b, t: (b, t, 0))

    def blk_spec(r, c):
        return pl.BlockSpec((1, nb, r, c), lambda b, t: (b, t, 0, 0))

    out_shape = (
        jax.ShapeDtypeStruct((B, T, 512), F32),
        jax.ShapeDtypeStruct((B, T, 256), BF16),
        jax.ShapeDtypeStruct((B, T, 512), BF16),
        jax.ShapeDtypeStruct((B, T, 256), BF16),
        jax.ShapeDtypeStruct((B, T, 256), F32),
        jax.ShapeDtypeStruct((B, T, 256), F32),
        jax.ShapeDtypeStruct((B, T, LANES), F32),
        jax.ShapeDtypeStruct((B, T, LANES), F32),
        jax.ShapeDtypeStruct((B, T, LANES), F32),
        jax.ShapeDtypeStruct((B, nkb, KEY_BLK, LANES), BF16),
        jax.ShapeDtypeStruct((B, nkb, KEY_BLK, LANES), BF16),
        jax.ShapeDtypeStruct((B, nkb, KEY_BLK, 256), BF16),
        jax.ShapeDtypeStruct((B, nkb, V_AUG, KEY_BLK), BF16),
        jax.ShapeDtypeStruct((B, 8, T), F32),
        jax.ShapeDtypeStruct((B, nkb, DIFF_HEADS, V_AUG, KEY_BLK), BF16),
        jax.ShapeDtypeStruct((B, 8, 256), F32),
        jax.ShapeDtypeStruct((B, 16, 256), F32),
    )
    out_specs = (
        row_spec(512), row_spec(256), row_spec(512), row_spec(256), row_spec(256), row_spec(256),
        row_spec(LANES), row_spec(LANES), row_spec(LANES),
        blk_spec(KEY_BLK, LANES), blk_spec(KEY_BLK, LANES), blk_spec(KEY_BLK, 256),
        blk_spec(V_AUG, KEY_BLK),
        pl.BlockSpec((1, 8, TS), lambda b, t: (b, 0, t)),
        pl.BlockSpec((1, nb, DIFF_HEADS, V_AUG, KEY_BLK), lambda b, t: (b, t, 0, 0, 0)),
        pl.BlockSpec((1, 8, 256), lambda b, t: (b, 0, 0)),
        pl.BlockSpec((1, 16, 256), lambda b, t: (b, 0, 0)),
    )
    in_specs = [
        row_spec(D), _const_spec(g.shape), _const_spec(w.shape), _const_spec(ca.shape),
        _const_spec(pm.shape), _const_spec(ps.shape),
    ]
    return pl.pallas_call(
        functools.partial(_k1p_kernel, TS=TS),
        grid=(B, nt), in_specs=in_specs, out_specs=out_specs, out_shape=out_shape,
        scratch_shapes=[pltpu.VMEM((TS + 8, GROUP_W), F32), pltpu.VMEM((TS + 16, GROUP_W), F32)],
        compiler_params=pltpu.CompilerParams(dimension_semantics=("arbitrary", "arbitrary"),
                                             vmem_limit_bytes=VMEM_LIMIT),
        name="mixer_in_prompt",
    )(x, g, w, ca, pm, ps)


def _k2p_kernel(cq_ref, cqi_ref, cwiT_ref, dq_ref, kkb_ref, iib_ref, cvT_ref, dkb_ref, dvT_ref,
                lam_ref, subln_ref, out_ref, key_ref, lgc_ref, lgd_ref, accc_ref, accd_ref, *, n_sel, lam_init):
    qb = pl.program_id(1)
    q0 = qb * Q_BLK
    nkb = (q0 + Q_BLK + KEY_BLK - 1) // KEY_BLK
    qpos = q0 + lax.broadcasted_iota(I32, (KEY_BLK, Q_BLK), 1)
    krow = lax.broadcasted_iota(I32, (KEY_BLK, Q_BLK), 0)
    lane_q = lax.broadcasted_iota(I32, (Q_BLK, LANES), 1)
    lo_half = lane_q < 64
    NG = KEY_BLK // SUBLANES

    def head_pair(tile):
        zero = jnp.zeros_like(tile)
        return jnp.concatenate([jnp.where(lo_half, tile, zero), jnp.where(lo_half, zero, tile)], axis=0)

    cqi = cqi_ref[0]
    wT = cwiT_ref[0]
    qi = [head_pair(cqi[:, j * LANES:(j + 1) * LANES]) for j in range(IDX_HEADS // 2)]

    def idx_body(kb, carry):
        ki = iib_ref[0, kb]
        acc = jnp.zeros((KEY_BLK, Q_BLK), F32)
        for j in range(IDX_HEADS // 2):
            s = _dot_nt(ki, qi[j])
            acc = acc + jnp.maximum(s[:, :Q_BLK], 0.0) * wT[2 * j:2 * j + 1]
            acc = acc + jnp.maximum(s[:, Q_BLK:], 0.0) * wT[2 * j + 1:2 * j + 2]
        score = acc * IDX_SCALE + 0.0
        causal = (kb * KEY_BLK + krow) <= qpos
        key_ref[kb] = jnp.where(causal, _float_key(score), INT_MIN)
        return carry

    lax.fori_loop(0, nkb, idx_body, 0)

    def count(pred):
        def body(kb, acc):
            c = jnp.where(pred(kb, key_ref[kb]), 1, 0)
            return acc + jnp.sum(c.reshape(KEY_BLK // SUBLANES, SUBLANES, Q_BLK), axis=0)
        acc = lax.fori_loop(0, nkb, body, jnp.zeros((SUBLANES, Q_BLK), I32))
        return jnp.sum(acc, axis=0, keepdims=True)

    def search_body(it, base):
        cand = base + lax.shift_left(jnp.int32(1), 31 - it)
        cnt = count(lambda kb, k: k >= cand)
        return jnp.where(cnt >= n_sel, cand, base)

    thr = lax.fori_loop(0, 32, search_body, jnp.full((1, Q_BLK), INT_MIN, I32))

    cnt_gt = count(lambda kb, k: k > thr)
    cnt_eq = count(lambda kb, k: k == thr)
    need = n_sel - cnt_gt
    excess = jnp.logical_and(cnt_eq > need, thr > INT_MIN)
    any_excess = jnp.max(jnp.where(excess, 1, 0))

    @pl.when(any_excess > 0)
    def _():
        def tie_body(it, jp):
            c = jp + lax.shift_left(jnp.int32(1), 10 - it)
            f = count(lambda kb, k: jnp.logical_and(k == thr, (kb * KEY_BLK + krow) < c))
            return jnp.where(f < need, c, jp)
        jp = lax.fori_loop(0, 11, tie_body, jnp.zeros((1, Q_BLK), I32))

        def demote(kb, carry):
            k = key_ref[kb]
            drop = jnp.logical_and(jnp.logical_and(k == thr, (kb * KEY_BLK + krow) > jp), excess)
            key_ref[kb] = jnp.where(drop, thr - 1, k)
            return carry
        lax.fori_loop(0, nkb, demote, 0)

    thr_sel = jnp.maximum(thr, INT_MIN + 1)

    cq = cq_ref[0] * (DSA_HD ** -0.5)
    qc = [head_pair(cq[:, j * LANES:(j + 1) * LANES]) for j in range(DSA_HEADS // 2)]

    def c1_body(kb, mx):
        kk = kkb_ref[0, kb]
        sel = key_ref[kb] >= thr_sel
        new = list(mx)
        for j in range(DSA_HEADS // 2):
            s = _dot_nt(kk, qc[j])
            for half in range(2):
                h = 2 * j + half
                lg = jnp.where(sel, s[:, half * Q_BLK:(half + 1) * Q_BLK], NEG)
                lgc_ref[h, kb] = lg
                new[h] = jnp.maximum(new[h], jnp.max(lg.reshape(NG, SUBLANES, Q_BLK), axis=0))
        return tuple(new)

    mx = lax.fori_loop(0, nkb, c1_body, tuple(jnp.full((SUBLANES, Q_BLK), NEG, F32) for _ in range(DSA_HEADS)))
    mc = [jnp.max(m, axis=0, keepdims=True) for m in mx]
    accc_ref[...] = jnp.zeros(accc_ref.shape, F32)

    def c2_body(kb, carry):
        vT = cvT_ref[0, kb]
        for j in range(DSA_HEADS // 2):
            p = jnp.concatenate([jnp.exp(lgc_ref[2 * j + half, kb] - mc[2 * j + half]) for half in range(2)], axis=1)
            accc_ref[j] += _dot(vT, p.astype(BF16))
        return carry

    lax.fori_loop(0, nkb, c2_body, 0)
    outs = []
    for h in range(DSA_HEADS):
        a = accc_ref[h // 2, :, (h % 2) * Q_BLK:(h % 2 + 1) * Q_BLK]
        outs.append(a[0:DSA_HD] / a[DSA_HD:DSA_HD + 1])
    out_ref[0, :, 0:256] = jnp.concatenate(outs, axis=0).T

    dq = dq_ref[0]
    lane256 = lax.broadcasted_iota(I32, (Q_BLK, 256), 1)

    def map_rows(h, mm):
        c0 = h * 64 + mm * DIFF_QD
        return jnp.where(jnp.logical_and(lane256 >= c0, lane256 < c0 + DIFF_QD), dq, jnp.zeros_like(dq))

    qd = [jnp.concatenate([map_rows(h, 0), map_rows(h, 1)], axis=0) for h in range(DIFF_HEADS)]
    c2 = (DIFF_QD ** -0.5) * LOG2E

    def d1_block(kb, mx, masked):
        kd = dkb_ref[0, kb]
        new = list(mx)
        for h in range(DIFF_HEADS):
            s = _dot_nt(kd, qd[h])
            if masked:
                causal = (kb * KEY_BLK + krow) <= qpos
                s = jnp.where(jnp.concatenate([causal, causal], axis=1), s, NEG)
            lgd_ref[h, kb] = s
            new[h] = jnp.maximum(new[h], jnp.max(s.reshape(NG, SUBLANES, 2 * Q_BLK), axis=0))
        return tuple(new)

    mx = lax.fori_loop(0, nkb - 1, lambda kb, m: d1_block(kb, m, False),
                       tuple(jnp.full((SUBLANES, 2 * Q_BLK), NEG, F32) for _ in range(DIFF_HEADS)))
    mx = d1_block(nkb - 1, mx, True)
    md = [jnp.max(m, axis=0, keepdims=True) * c2 for m in mx]
    accd_ref[...] = jnp.zeros(accd_ref.shape, F32)

    def d2_body(kb, carry):
        for h in range(DIFF_HEADS):
            p = jnp.exp2(lgd_ref[h, kb] * c2 - md[h])
            accd_ref[h] += _dot(dvT_ref[0, kb, h], p.astype(BF16))
        return carry

    lax.fori_loop(0, nkb, d2_body, 0)
    lam = _lambda_full(lam_ref, lam_init)
    subln = subln_ref[...]
    outs = []
    for h in range(DIFF_HEADS):
        a1 = accd_ref[h, :, 0:Q_BLK]
        a2 = accd_ref[h, :, Q_BLK:2 * Q_BLK]
        o = a1[0:DIFF_VD] / a1[DIFF_VD:DIFF_VD + 1] - lam * (a2[0:DIFF_VD] / a2[DIFF_VD:DIFF_VD + 1])
        ms = jnp.mean(o * o, axis=0, keepdims=True)
        outs.append((o * lax.rsqrt(ms + EPS)) * subln * (1.0 - lam_init))
    out_ref[0, :, 256:512] = jnp.concatenate(outs, axis=0).T


def _k2p(cq, cqi, cwiT, dq, kkb, iib, cvT, dkb, dvT, lam, subln, *, n_sel, lam_init):
    B, T, _ = cq.shape
    nq = T // Q_BLK
    nkb = T // KEY_BLK

    def q_spec(c):
        return pl.BlockSpec((1, Q_BLK, c), lambda b, q: (b, q, 0))

    def seq_spec(r, c):
        return pl.BlockSpec((1, nkb, r, c), lambda b, q: (b, 0, 0, 0))

    in_specs = [
        q_spec(256), q_spec(512), pl.BlockSpec((1, 8, Q_BLK), lambda b, q: (b, 0, q)), q_spec(256),
        seq_spec(KEY_BLK, LANES), seq_spec(KEY_BLK, LANES), seq_spec(V_AUG, KEY_BLK),
        seq_spec(KEY_BLK, 256),
        pl.BlockSpec((1, nkb, DIFF_HEADS, V_AUG, KEY_BLK), lambda b, q: (b, 0, 0, 0, 0)),
        _const_spec(lam.shape), _const_spec(subln.shape),
    ]
    return pl.pallas_call(
        functools.partial(_k2p_kernel, n_sel=n_sel, lam_init=lam_init),
        grid=(B, nq), in_specs=in_specs, out_specs=q_spec(512),
        out_shape=jax.ShapeDtypeStruct((B, T, 512), F32),
        scratch_shapes=[pltpu.VMEM((nkb, KEY_BLK, Q_BLK), I32),
                        pltpu.VMEM((DSA_HEADS, nkb, KEY_BLK, Q_BLK), F32),
                        pltpu.VMEM((DIFF_HEADS, nkb, KEY_BLK, 2 * Q_BLK), F32),
                        pltpu.VMEM((DSA_HEADS // 2, V_AUG, 2 * Q_BLK), F32),
                        pltpu.VMEM((DIFF_HEADS, V_AUG, 2 * Q_BLK), F32)],
        compiler_params=pltpu.CompilerParams(dimension_semantics=("arbitrary", "arbitrary"),
                                             vmem_limit_bytes=VMEM_LIMIT),
        name="attention_prompt",
    )(cq, cqi, cwiT, dq, kkb, iib, cvT, dkb, dvT, lam, subln)


def _ple_tail(h2, p_bf, gple, wpg, wple, gfin, final_norm):
    gate = _sigmoid(_dot(_rms(h2, gple).astype(BF16), wpg))
    h3 = h2 + _dot(p_bf, wple) * gate
    if final_norm:
        h3 = _rms(h3, gfin)
    return h3


def _k3p_kernel(h_ref, ab_ref, cd_ref, p_ref, wout_ref, gffn_ref, wg_ref, wu_ref, cf_ref, wd_ref,
                gple_ref, wpg_ref, wple_ref, gfin_ref, hout_ref, ffnst_ref, extg_ref, *, TS, final_norm):
    t = pl.program_id(1)

    @pl.when(t == 0)
    def _():
        extg_ref[0:8, :] = jnp.zeros((8, extg_ref.shape[1]), F32)

    mix = jnp.concatenate([ab_ref[0], cd_ref[0]], axis=1).astype(BF16)
    h1 = h_ref[0] + _dot(mix, wout_ref[...])
    xn2 = _rms(h1, gffn_ref[...]).astype(BF16)
    g = _dot(xn2, wg_ref[...])
    extg_ref[8:8 + TS, :] = g
    p1 = extg_ref[7:7 + TS, :]
    p2 = extg_ref[6:6 + TS, :]
    cf = cf_ref[...]
    gc = p2 * cf[0:1] + p1 * cf[1:2] + g * cf[2:3]
    extg_ref[0:8, :] = g[TS - 8:TS]
    ffnst_ref[0] = g[TS - 8:TS]
    act = (gc * _sigmoid(gc)) * _dot(xn2, wu_ref[...])
    h2 = h1 + _dot(act.astype(BF16), wd_ref[...])
    hout_ref[0] = _ple_tail(h2, p_ref[0].astype(BF16), gple_ref[...], wpg_ref[...], wple_ref[...],
                            gfin_ref[...], final_norm)


def _k3p(h, ab, cd, p, wout, gffn, wg, wu, cf, wd, gple, wpg, wple, gfin, *, TS, final_norm):
    B, T, D = h.shape
    dff = wg.shape[1]
    nt = T // TS

    def row_spec(c):
        return pl.BlockSpec((1, TS, c), lambda b, t: (b, t, 0))

    def w_spec(a):
        nd = a.ndim
        return pl.BlockSpec(a.shape, lambda *_: (0,) * nd, pipeline_mode=pl.Buffered(1))

    in_specs = [row_spec(D), row_spec(512), row_spec(512), row_spec(p.shape[2]),
                w_spec(wout), w_spec(gffn), w_spec(wg), w_spec(wu), w_spec(cf), w_spec(wd),
                w_spec(gple), w_spec(wpg), w_spec(wple), w_spec(gfin)]
    out_specs = (row_spec(D), pl.BlockSpec((1, 8, dff), lambda b, t: (b, 0, 0)))
    out_shape = (jax.ShapeDtypeStruct((B, T, D), F32), jax.ShapeDtypeStruct((B, 8, dff), F32))
    return pl.pallas_call(
        functools.partial(_k3p_kernel, TS=TS, final_norm=final_norm),
        grid=(B, nt), in_specs=in_specs, out_specs=out_specs, out_shape=out_shape,
        scratch_shapes=[pltpu.VMEM((TS + 8, dff), F32)],
        compiler_params=pltpu.CompilerParams(dimension_semantics=("arbitrary", "arbitrary"),
                                             vmem_limit_bytes=VMEM_LIMIT),
        name="channel_mix_prompt",
    )(h, ab, cd, p, wout, gffn, wg, wu, cf, wd, gple, wpg, wple, gfin)


def _k1s_kernel(x_ref, g_ref, w_ref, ca_ref, pm_ref, ps_ref, sconv_ref, spool_ref,
                mixab_ref, cq_ref, cqi_ref, dq_ref, dk_ref, dv_ref, skk_ref, sii_ref, svw_ref,
                u_ref, pv_ref, *, DB, TD, pos0):
    xn = _rms(x_ref[...], g_ref[...])
    z = _dot(xn.astype(BF16), w_ref[...])

    def slab(t, c0, c1):
        return z[t * DB:(t + 1) * DB, c0:c1]

    ca = ca_ref[...]
    us = [slab(t, C_GC, C_GC + 256) * slab(t, C_GH, C_GH + 256) for t in range(TD)]
    extu = [sconv_ref[0], sconv_ref[1]] + us
    pvs = [slab(t, C_PV, C_PV + 256) for t in range(TD)]
    extx = [spool_ref[j] for j in range(POOL_BUF)] + pvs
    lo = lax.broadcasted_iota(I32, (DB, LANES), 1) < 64
    pm = pm_ref[...]
    ps = ps_ref[...]
    for t in range(TD):
        conv = extu[t] * ca[0:1] + extu[t + 1] * ca[1:2] + extu[t + 2] * ca[2:3]
        mixab_ref[t * DB:(t + 1) * DB, 0:256] = slab(t, C_GB, C_GB + 256) * conv
        u_ref[t] = us[t]
        pv_ref[t] = pvs[t]
        e = POOL_BUF + t

        def win(n, c0):
            s = extx[e][:, c0:c0 + LANES]
            for j in range(1, n):
                s = s + extx[e - j][:, c0:c0 + LANES]
            return s

        cnt = [float(min(w, pos0 + t + 1)) for w in POOL_WINDOWS]
        d0 = jnp.where(lo, win(2, 0) / cnt[0], win(4, 0) / cnt[1]) - pvs[t][:, 0:LANES]
        d1 = jnp.where(lo, win(8, LANES) / cnt[2], win(16, LANES) / cnt[3]) - pvs[t][:, LANES:]
        d = jnp.concatenate([d0, d1], axis=1)
        mixab_ref[t * DB:(t + 1) * DB, 256:512] = _dot(d.astype(BF16), pm) * ps

    cq_ref[...] = z[:, C_CQ:C_CQ + 256].astype(BF16)
    cqi_ref[...] = z[:, C_CQI:C_CQI + 512].astype(BF16)
    dq_ref[...] = z[:, C_DQ:C_DQ + 256].astype(BF16)
    dk_ref[...] = z[:, C_DK:C_DK + 256]
    dv_ref[...] = z[:, C_DV:C_DV + 256]
    skk_ref[...] = z[:, C_KK:C_KK + LANES]
    sii_ref[...] = z[:, C_II:C_II + LANES]
    svw_ref[...] = z[:, C_VW:C_VW + LANES]


def _k1s(x, g, w, ca, pm, ps, sconv, spool, *, DB, TD, pos0):
    R = x.shape[0]
    out_shape = (
        jax.ShapeDtypeStruct((R, 512), F32), jax.ShapeDtypeStruct((R, 256), BF16),
        jax.ShapeDtypeStruct((R, 512), BF16), jax.ShapeDtypeStruct((R, 256), BF16),
        jax.ShapeDtypeStruct((R, 256), F32), jax.ShapeDtypeStruct((R, 256), F32),
        jax.ShapeDtypeStruct((R, LANES), F32), jax.ShapeDtypeStruct((R, LANES), F32),
        jax.ShapeDtypeStruct((R, LANES), F32),
        jax.ShapeDtypeStruct((TD, DB, 256), F32), jax.ShapeDtypeStruct((TD, DB, 256), F32),
    )
    args = (x, g, w, ca, pm, ps, sconv, spool)
    return pl.pallas_call(
        functools.partial(_k1s_kernel, DB=DB, TD=TD, pos0=pos0),
        grid=(1,), in_specs=[_const_spec(a.shape) for a in args],
        out_specs=tuple(_const_spec(s.shape) for s in out_shape), out_shape=out_shape,
        compiler_params=pltpu.CompilerParams(dimension_semantics=("arbitrary",), vmem_limit_bytes=VMEM_LIMIT),
        name="mixer_in_sample",
    )(*args)


QP = 8


def _wrap32(v):
    return ((v + 2 ** 31) % 2 ** 32) - 2 ** 31


def _k2s_kernel(pt_ref, qi_ref, wi_ref, qc_ref, qd_ref, kin_ref, kn_ref, vn_ref, dkn_ref, dvn_ref,
                lam_ref, subln_ref, *rest, NP, PS, TD, n_sel, lam_init):
    G = QP // TD
    n = G * NP
    idx_refs, k_refs, v_refs, dk_refs, dv_refs = (rest[i * n:(i + 1) * n] for i in range(5))
    out_ref, key_ref, clog_ref, dlog_ref = rest[5 * n:]
    del pt_ref
    SC = (NP + 1) * PS

    def own_half(x):
        elem0 = (lax.broadcasted_iota(I32, (x.shape[0], PS), 0) % QP) < TD
        return jnp.where(elem0, x[:, :PS], x[:, PS:])

    def split_half(x):
        elem0 = (lax.broadcasted_iota(I32, x.shape, 0) % QP) < TD
        zero = jnp.zeros_like(x)
        return jnp.concatenate([jnp.where(elem0, x, zero), jnp.where(elem0, zero, x)], axis=1)

    def pages(refs, new_ref, p, rows):
        tiles = [(refs[g * NP + p][0, 0] if p < NP else new_ref[0, g]).reshape(rows, PS).astype(BF16)
                 for g in range(G)]
        return jnp.concatenate(tiles, axis=1)

    qi = qi_ref[0]
    wi = wi_ref[0]
    col = lax.broadcasted_iota(I32, (QP, PS), 1)
    qt = lax.broadcasted_iota(I32, (QP, PS), 0) % TD
    new_ok = jnp.logical_and(col < TD, col <= qt)
    for p in range(NP + 1):
        s = own_half(_dot(qi, pages(idx_refs, kin_ref, p, IDX_HD)))
        r = (jnp.maximum(s, 0.0) * wi).reshape(IDX_HEADS, QP, PS)
        score = jnp.sum(r, axis=0) * IDX_SCALE + 0.0
        key = _float_key(score)
        if p == NP:
            key = jnp.where(new_ok, key, INT_MIN)
        key_ref[:, p * PS:(p + 1) * PS] = key

    kcol = lax.broadcasted_iota(I32, (QP, SC), 1)

    def count(pred):
        c = jnp.where(pred(key_ref[...]), 1.0, 0.0)
        return jnp.sum(c, axis=1, keepdims=True)

    thr = jnp.full((QP, 1), INT_MIN, I32)
    for step in range(16):
        sh = 30 - 2 * step
        passed = jnp.zeros((QP, 1), I32)
        for j in (1, 2, 3):
            cnt = count(lambda k, c=thr + _wrap32(j << sh): k >= c)
            passed = passed + jnp.where(cnt >= n_sel, 1, 0)
        thr = thr + lax.shift_left(passed, sh)
    cnt_gt = count(lambda k: k > thr)
    cnt_eq = count(lambda k: k == thr)
    need = n_sel - cnt_gt
    excess = jnp.logical_and(cnt_eq > need, thr > INT_MIN)
    any_excess = jnp.max(jnp.where(excess, 1, 0))

    @pl.when(any_excess > 0)
    def _():
        def tie_body(it, jp):
            c = jp + lax.shift_left(jnp.int32(1), 12 - it)
            f = count(lambda k: jnp.logical_and(k == thr, kcol < c))
            return jnp.where(f < need, c, jp)
        jp = lax.fori_loop(0, 13, tie_body, jnp.zeros((QP, 1), I32))
        k = key_ref[...]
        drop = jnp.logical_and(jnp.logical_and(k == thr, kcol > jp), excess)
        key_ref[...] = jnp.where(drop, thr - 1, k)

    thr_sel = jnp.maximum(thr, INT_MIN + 1)
    sel = key_ref[...] >= thr_sel

    qc = qc_ref[0] * (DSA_HD ** -0.5)
    for p in range(NP + 1):
        clog_ref[:, p * PS:(p + 1) * PS] = own_half(_dot(qc, pages(k_refs, kn_ref, p, DSA_HD)))
    lg = jnp.where(sel[None], clog_ref[...].reshape(DSA_HEADS, QP, SC), NEG)
    lg = lg.reshape(DSA_HEADS * QP, SC)
    m = jnp.max(lg, axis=1, keepdims=True)
    pe = jnp.exp(lg - m)
    l = jnp.sum(pe, axis=1, keepdims=True)
    clog_ref[...] = pe
    acc = jnp.zeros((DSA_HEADS * QP, DSA_HD), F32)
    for p in range(NP + 1):
        pp = split_half(clog_ref[:, p * PS:(p + 1) * PS]).astype(BF16)
        acc = acc + _dot_nt(pp, pages(v_refs, vn_ref, p, DSA_HD))
    oc = acc / l
    for h in range(DSA_HEADS):
        out_ref[0, :, h * 64:(h + 1) * 64] = oc[h * QP:(h + 1) * QP]

    qd0 = qd_ref[0]
    rhead = lax.broadcasted_iota(I32, qd0.shape, 0) // QP
    lane = lax.broadcasted_iota(I32, qd0.shape, 1)
    zero = jnp.zeros_like(qd0)
    qd = jnp.concatenate(
        [jnp.where(jnp.logical_and(lane >= rhead * 64 + mm * DIFF_QD, lane < rhead * 64 + (mm + 1) * DIFF_QD),
                   qd0, zero) for mm in range(2)], axis=0)
    HD = DIFF_HEADS * DIFF_VD
    nrow = 2 * DIFF_HEADS * QP
    for p in range(NP + 1):
        s = own_half(_dot(qd, pages(dk_refs, dkn_ref, p, HD))) * (DIFF_QD ** -0.5)
        if p == NP:
            ncol = lax.broadcasted_iota(I32, (nrow, PS), 1)
            nqt = lax.broadcasted_iota(I32, (nrow, PS), 0) % TD
            s = jnp.where(jnp.logical_and(ncol < TD, ncol <= nqt), s, NEG)
        dlog_ref[:, p * PS:(p + 1) * PS] = s
    lgd = dlog_ref[...]
    md = jnp.max(lgd, axis=1, keepdims=True)
    ped = jnp.exp(lgd - md)
    ld = jnp.sum(ped, axis=1, keepdims=True)
    dlog_ref[...] = ped
    accd = jnp.zeros((nrow, HD), F32)
    for p in range(NP + 1):
        pp = split_half(dlog_ref[:, p * PS:(p + 1) * PS]).astype(BF16)
        accd = accd + _dot_nt(pp, pages(dv_refs, dvn_ref, p, HD))
    od = accd / ld
    half = DIFF_HEADS * QP
    lam = _lambda_full(lam_ref, lam_init)
    subln = subln_ref[...]
    for h in range(DIFF_HEADS):
        o = (od[h * QP:(h + 1) * QP, h * 64:(h + 1) * 64]
             - lam * od[half + h * QP:half + (h + 1) * QP, h * 64:(h + 1) * 64])
        ms = jnp.mean(o * o, axis=-1, keepdims=True)
        out_ref[0, :, 256 + h * 64:256 + (h + 1) * 64] = (o * lax.rsqrt(ms + EPS)) * subln * (1.0 - lam_init)


def _k2s(page_table, qi, wi, qc, qd, kin, kn, vn, dkn, dvn, lam, subln,
         c_idx, c_k, c_v, d_k, d_v, *, layer, TD, n_sel, lam_init):
    DB, NP = page_table.shape
    PS = c_k.shape[3]
    G = QP // TD
    NS = DB // G

    def b_spec(a):
        return pl.BlockSpec((1,) + a.shape[1:], lambda b, pt: (b,) + (0,) * (a.ndim - 1))

    def c_spec(a):
        nd = a.ndim
        return pl.BlockSpec(a.shape, lambda b, pt: (0,) * nd)

    def cpage_spec(g, p):
        return pl.BlockSpec((1, 1, 64, PS), lambda b, pt: (pt[G * b + g, p], layer, 0, 0))

    def dpage_spec(g, p):
        return pl.BlockSpec((1, 1, DIFF_HEADS, 64, PS), lambda b, pt: (pt[G * b + g, p], layer, 0, 0, 0))

    gp = [(g, p) for g in range(G) for p in range(NP)]
    in_specs = [b_spec(qi), b_spec(wi), b_spec(qc), b_spec(qd), b_spec(kin), b_spec(kn), b_spec(vn),
                b_spec(dkn), b_spec(dvn), c_spec(lam), c_spec(subln)]
    for _ in range(3):
        in_specs += [cpage_spec(g, p) for g, p in gp]
    for _ in range(2):
        in_specs += [dpage_spec(g, p) for g, p in gp]
    SC = (NP + 1) * PS
    grid_spec = pltpu.PrefetchScalarGridSpec(
        num_scalar_prefetch=1, grid=(NS,), in_specs=in_specs,
        out_specs=pl.BlockSpec((1, QP, 512), lambda b, pt: (b, 0, 0)),
        scratch_shapes=[pltpu.VMEM((QP, SC), I32), pltpu.VMEM((DSA_HEADS * QP, SC), F32),
                        pltpu.VMEM((2 * DIFF_HEADS * QP, SC), F32)],
    )
    args = [page_table, qi, wi, qc, qd, kin, kn, vn, dkn, dvn, lam, subln]
    args += [c_idx] * (G * NP) + [c_k] * (G * NP) + [c_v] * (G * NP) + [d_k] * (G * NP) + [d_v] * (G * NP)
    return pl.pallas_call(
        functools.partial(_k2s_kernel, NP=NP, PS=PS, TD=TD, n_sel=n_sel, lam_init=lam_init),
        grid_spec=grid_spec, out_shape=jax.ShapeDtypeStruct((NS, QP, 512), F32),
        compiler_params=pltpu.CompilerParams(dimension_semantics=("arbitrary",), vmem_limit_bytes=VMEM_LIMIT),
        name="attention_sample",
    )(*args)


FF_TILE = 256


def _k3s_kernel(h_ref, ab_ref, cd_ref, p_ref, wout_ref, gffn_ref, wg_ref, wu_ref, cf_ref, wd_ref,
                gple_ref, wpg_ref, wple_ref, gfin_ref, sffn_ref, hout_ref, gout_ref,
                h1_ref, xn2_ref, acc_ref, *, DB, TD, final_norm):
    f = pl.program_id(0)

    @pl.when(f == 0)
    def _():
        mix = jnp.concatenate([ab_ref[...], cd_ref[...]], axis=1).astype(BF16)
        h1 = h_ref[...] + _dot(mix, wout_ref[...])
        h1_ref[...] = h1
        xn2_ref[...] = _rms(h1, gffn_ref[...]).astype(BF16)
        acc_ref[...] = jnp.zeros(acc_ref.shape, F32)

    xn2 = xn2_ref[...]
    g = _dot(xn2, wg_ref[...])
    cf = cf_ref[...]
    gs = [sffn_ref[0], sffn_ref[1]] + [g[t * DB:(t + 1) * DB] for t in range(TD)]
    gc = jnp.concatenate([gs[t] * cf[0:1] + gs[t + 1] * cf[1:2] + gs[t + 2] * cf[2:3] for t in range(TD)], axis=0)
    act = (gc * _sigmoid(gc)) * _dot(xn2, wu_ref[...])
    acc_ref[...] += _dot(act.astype(BF16), wd_ref[...])
    gout_ref[0] = gs[TD]
    gout_ref[1] = gs[TD + 1]

    @pl.when(f == pl.num_programs(0) - 1)
    def _():
        h2 = h1_ref[...] + acc_ref[...]
        hout_ref[...] = _ple_tail(h2, p_ref[...].astype(BF16), gple_ref[...], wpg_ref[...], wple_ref[...],
                                  gfin_ref[...], final_norm)


def _k3s(h, ab, cd, p, wout, gffn, wg, wu, cf, wd, gple, wpg, wple, gfin, sffn, *, DB, TD, final_norm):
    R, D = h.shape
    dff = wg.shape[1]
    nf = dff // FF_TILE
    cs = _const_spec
    in_specs = [cs(h.shape), cs(ab.shape), cs(cd.shape), cs(p.shape), cs(wout.shape), cs(gffn.shape),
                pl.BlockSpec((D, FF_TILE), lambda f: (0, f)), pl.BlockSpec((D, FF_TILE), lambda f: (0, f)),
                pl.BlockSpec((3, FF_TILE), lambda f: (0, f)), pl.BlockSpec((FF_TILE, D), lambda f: (f, 0)),
                cs(gple.shape), cs(wpg.shape), cs(wple.shape), cs(gfin.shape),
                pl.BlockSpec((2, DB, FF_TILE), lambda f: (0, 0, f))]
    out_specs = (cs((R, D)), pl.BlockSpec((2, DB, FF_TILE), lambda f: (0, 0, f)))
    out_shape = (jax.ShapeDtypeStruct((R, D), F32), jax.ShapeDtypeStruct((2, DB, dff), F32))
    return pl.pallas_call(
        functools.partial(_k3s_kernel, DB=DB, TD=TD, final_norm=final_norm),
        grid=(nf,), in_specs=in_specs, out_specs=out_specs, out_shape=out_shape,
        scratch_shapes=[pltpu.VMEM((R, D), F32), pltpu.VMEM((R, D), BF16), pltpu.VMEM((R, D), F32)],
        compiler_params=pltpu.CompilerParams(dimension_semantics=("arbitrary",), vmem_limit_bytes=VMEM_LIMIT),
        name="channel_mix_sample",
    )(h, ab, cd, p, wout, gffn, wg, wu, cf, wd, gple, wpg, wple, gfin, sffn)


def _prep_w_in(w_in):
    cuts = np.cumsum(SPLIT_SIZES)[:-1].tolist()
    gb, gc, gh, pv, cq, ck, cv, cqi, cwi, cki, dq, dk, dv = jnp.split(w_in, cuts, axis=-1)
    pad = jnp.zeros(w_in.shape[:-1] + (LANES - 64 - IDX_HEADS,), w_in.dtype)
    cols = [gb, gc, gh, pv, cq, cqi, dq, dk, dv, ck, ck, cki, cki, cv, cwi, pad]
    return jnp.concatenate(cols, axis=-1).astype(BF16)


def _block_diag_maps(pool_maps):
    depth, G, c, _ = pool_maps.shape
    out = jnp.zeros((depth, G * c, G * c), pool_maps.dtype)
    for g in range(G):
        out = out.at[:, g * c:(g + 1) * c, g * c:(g + 1) * c].set(pool_maps[:, g])
    return out.astype(BF16)


def _heads_rows(a, NS, G, TD, nh):
    a = a.reshape(TD, NS, G, nh, 64).transpose(1, 3, 2, 0, 4)
    return a.reshape(NS, nh * QP, 64)


def _new_page(a, NS, G, TD, PS):
    c = a.shape[1]
    a = a.reshape(TD, NS, G, c).transpose(1, 2, 3, 0)
    return jnp.pad(a, ((0, 0), (0, 0), (0, 0), (0, PS - TD)))


def kernel(x_prompt, x_sample, state_conv_a, state_pool, state_ffn, cache_c_k, cache_c_v, cache_c_idx,
           cache_d_k, cache_d_v, page_table, p_prompt, p_sample, norm_mix, w_in, conv_a, pool_maps,
           pool_scale, diff_lambda, diff_subln, w_out, norm_ffn, w_gate, w_up, conv_ffn, w_down,
           norm_ple, w_ple_gate, w_ple, norm_final):
    depth = w_in.shape[0]
    B, T, D = x_prompt.shape
    DB, TD, _ = x_sample.shape
    NP = page_table.shape[1]
    PS = cache_c_k.shape[2]
    past = NP * PS
    dff = w_gate.shape[2]
    TS1 = min(512, T)
    TS3 = min(256, T)

    w_in_p = _prep_w_in(w_in)
    pm_bd = _block_diag_maps(pool_maps)
    w_out_b, w_gate_b, w_up_b, w_down_b = (w.astype(BF16) for w in (w_out, w_gate, w_up, w_down))
    w_pg_b, w_ple_b = w_ple_gate.astype(BF16), w_ple.astype(BF16)
    gfin = norm_final.reshape(1, D)
    c_idx_t, c_k_t, c_v_t = (c.transpose(0, 1, 3, 2) for c in (cache_c_idx, cache_c_k, cache_c_v))
    d_k_t, d_v_t = (c.transpose(0, 1, 3, 4, 2) for c in (cache_d_k, cache_d_v))
    assert QP % TD == 0 and DB % (QP // TD) == 0 and TD >= 2
    G = QP // TD
    NS = DB // G

    def row(a, i):
        return a[i].reshape(1, -1)

    h = x_prompt
    st_p = []
    n_sel_p = min(TOPK_MAX, T // 4)
    for i in range(depth):
        lam_init = 0.8 - 0.6 * math.exp(-0.3 * i)
        (mixab, cq, cqi, dq, dk, dv, skk, sii, svw, kkb, iib, dkb, cvT, cwiT, dvT, convst, poolst) = _k1p(
            h, row(norm_mix, i), w_in_p[i], conv_a[i], pm_bd[i], row(pool_scale, i), TS=TS1)
        mixcd = _k2p(cq, cqi, cwiT, dq, kkb, iib, cvT, dkb, dvT, diff_lambda[i], diff_subln[i].reshape(-1, 1),
                     n_sel=n_sel_p, lam_init=lam_init)
        h, ffnst = _k3p(h, mixab, mixcd, p_prompt[i], w_out_b[i], row(norm_ffn, i), w_gate_b[i], w_up_b[i],
                        conv_ffn[i], w_down_b[i], row(norm_ple, i), w_pg_b[i], w_ple_b[i], gfin,
                        TS=TS3, final_norm=(i == depth - 1))
        st_p.append((convst[:, 6:8], poolst[:, 1:16], ffnst[:, 6:8], skk[..., :64], svw[..., :64], sii[..., :64],
                     dk.reshape(B, T, DIFF_HEADS, 64), dv.reshape(B, T, DIFF_HEADS, 64)))
    y_prompt = h

    hs = x_sample.transpose(1, 0, 2).reshape(TD * DB, D)
    st_s = []
    n_sel_s = min(TOPK_MAX, (past + TD) // 4)
    for i in range(depth):
        lam_init = 0.8 - 0.6 * math.exp(-0.3 * i)
        sconv = state_conv_a[i].transpose(1, 0, 2)
        spool = state_pool[i].transpose(1, 0, 2)
        sffn = state_ffn[i].transpose(1, 0, 2)
        (mixab, cq, cqi, dq, dk, dv, skk, sii, svw, u_new, pv_new) = _k1s(
            hs, row(norm_mix, i), w_in_p[i], conv_a[i], pm_bd[i], row(pool_scale, i), sconv, spool,
            DB=DB, TD=TD, pos0=past)
        qi = _heads_rows(cqi, NS, G, TD, IDX_HEADS)
        wi = svw[:, 64:64 + IDX_HEADS].reshape(TD, NS, G, IDX_HEADS).transpose(1, 3, 2, 0)
        wi = jnp.broadcast_to(wi.reshape(NS, IDX_HEADS * QP, 1), (NS, IDX_HEADS * QP, LANES))
        qc = _heads_rows(cq, NS, G, TD, DSA_HEADS)
        qd = dq.reshape(TD, NS, G, 256).transpose(1, 2, 0, 3).reshape(NS, 1, QP, 256)
        qd = jnp.broadcast_to(qd, (NS, DIFF_HEADS, QP, 256)).reshape(NS, DIFF_HEADS * QP, 256)
        kin = _new_page(sii[:, :64], NS, G, TD, PS)
        kn = _new_page(skk[:, :64], NS, G, TD, PS)
        vn = _new_page(svw[:, :64], NS, G, TD, PS)
        dkn = _new_page(dk, NS, G, TD, PS)
        dvn = _new_page(dv, NS, G, TD, PS)
        ycd = _k2s(page_table, qi, wi, qc, qd, kin, kn, vn, dkn, dvn, diff_lambda[i], row(diff_subln, i),
                   c_idx_t, c_k_t, c_v_t, d_k_t, d_v_t, layer=i, TD=TD, n_sel=n_sel_s, lam_init=lam_init)
        mixcd = ycd.reshape(NS, G, TD, 512).transpose(2, 0, 1, 3).reshape(TD * DB, 512)
        hs, g_new = _k3s(hs, mixab, mixcd, p_sample[i].transpose(1, 0, 2).reshape(TD * DB, -1), w_out_b[i],
                         row(norm_ffn, i), w_gate_b[i], w_up_b[i], conv_ffn[i], w_down_b[i], row(norm_ple, i),
                         w_pg_b[i], w_ple_b[i], gfin, sffn, DB=DB, TD=TD, final_norm=(i == depth - 1))

        def bm(a, width):
            return a[:, :width].reshape(TD, DB, width).transpose(1, 0, 2)

        new_conv = u_new[TD - 2:].transpose(1, 0, 2)
        new_pool = jnp.concatenate([state_pool[i], pv_new.transpose(1, 0, 2)], axis=1)[:, -POOL_BUF:]
        st_s.append((new_conv, new_pool, g_new.transpose(1, 0, 2), bm(skk, 64), bm(svw, 64), bm(sii, 64),
                     bm(dk, 256).reshape(DB, TD, DIFF_HEADS, 64), bm(dv, 256).reshape(DB, TD, DIFF_HEADS, 64)))
    y_sample = hs.reshape(TD, DB, D).transpose(1, 0, 2)

    def col(outs, j, axis):
        return jnp.stack([o[j] for o in outs], axis=axis)

    return (y_prompt, y_sample, col(st_p, 0, 0), col(st_s, 0, 0), col(st_p, 1, 0), col(st_s, 1, 0),
            col(st_p, 2, 0), col(st_s, 2, 0), col(st_p, 3, 1), col(st_s, 3, 1), col(st_p, 4, 1), col(st_s, 4, 1),
            col(st_p, 5, 1), col(st_s, 5, 1), col(st_p, 6, 1), col(st_s, 6, 1), col(st_p, 7, 1), col(st_s, 7, 1))
```

```python
import functools
import math

import numpy as np
import jax
import jax.numpy as jnp
from jax import lax
from jax.experimental import pallas as pl
from jax.experimental.pallas import tpu as pltpu

F32 = jnp.float32
BF16 = jnp.bfloat16
I32 = jnp.int32

EPS = 1e-6
GROUP_W = 256
POOL_WINDOWS = (2, 4, 8, 16)
POOL_BUF = 15
DSA_HEADS = 4
DSA_HD = 64
IDX_HEADS = 8
IDX_HD = 64
IDX_SCALE = IDX_HEADS ** -0.5 * IDX_HD ** -0.5
TOPK_MAX = 256
DIFF_HEADS = 4
DIFF_VD = 64
DIFF_QD = 32
SPLIT_SIZES = (256, 256, 256, 256, 256, 64, 64, 512, 8, 64, 256, 256, 256)

LANES = 128
SUBLANES = 8
VMEM_LIMIT = 56 * 1024 * 1024

C_GB, C_GC, C_GH, C_PV = 0, 256, 512, 768
C_CQ, C_CQI, C_DQ, C_DK, C_DV = 1024, 1280, 1792, 2048, 2304
C_KK, C_II, C_VW = 2560, 2688, 2816
D_IN_P = 2944

KEY_BLK = 256
Q_BLK = 128
V_AUG = 80
LOG2E = math.log2(math.e)
NEG = -1e30
INT_MIN = -2 ** 31

NT_DIMS = (((1,), (1,)), ((), ()))


def _dot(a, b):
    return jnp.dot(a, b, preferred_element_type=F32)


def _dot_nt(a, b):
    return lax.dot_general(a, b, NT_DIMS, preferred_element_type=F32)


def _rms(x, g):
    ms = jnp.mean(x * x, axis=-1, keepdims=True)
    return (x * lax.rsqrt(ms + EPS)) * g


def _sigmoid(x):
    return 1.0 / (1.0 + jnp.exp(-x))


def _float_key(x):
    b = lax.bitcast_convert_type(x, I32)
    return b ^ (lax.shift_right_arithmetic(b, 31) & 0x7FFFFFFF)


def _lambda_full(lam_ref, lam_init):
    lp = lam_ref[...]
    s1 = jnp.sum(lp[0:1] * lp[1:2], axis=-1, keepdims=True)
    s2 = jnp.sum(lp[2:3] * lp[3:4], axis=-1, keepdims=True)
    return jnp.exp(s1) - jnp.exp(s2) + lam_init


def _const_spec(shape):
    nd = len(shape)
    return pl.BlockSpec(shape, lambda *_: (0,) * nd)


def _k1p_kernel(x_ref, g_ref, w_ref, ca_ref, pm_ref, ps_ref, _a0, _a1, _a2, _a3, _a4,
                mixab_ref, cq_ref, cqi_ref, dq_ref,
                kkb_ref, iib_ref, dkb_ref, cvT_ref, cwiT_ref, dvT_ref, convst_ref, poolst_ref,
                ckT_ref, cvTf_ref, ckiT_ref, dkT_ref, dvTf_ref,
                extu_ref, extx_ref, *, TS):
    t = pl.program_id(1)

    @pl.when(t == 0)
    def _():
        extu_ref[0:8, :] = jnp.zeros((8, GROUP_W), F32)
        extx_ref[0:16, :] = jnp.zeros((16, GROUP_W), F32)

    xn = _rms(x_ref[0], g_ref[...])
    z = _dot(xn.astype(BF16), w_ref[...])

    u = z[:, C_GC:C_GC + 256] * z[:, C_GH:C_GH + 256]
    extu_ref[8:8 + TS, :] = u
    p1 = extu_ref[7:7 + TS, :]
    p2 = extu_ref[6:6 + TS, :]
    ca = ca_ref[...]
    conv = p2 * ca[0:1] + p1 * ca[1:2] + u * ca[2:3]
    mixab_ref[0, :, 0:256] = z[:, C_GB:C_GB + 256] * conv
    extu_ref[0:8, :] = u[TS - 8:TS]
    convst_ref[0] = u[TS - 8:TS]

    pv = z[:, C_PV:C_PV + 256]
    extx_ref[16:16 + TS, :] = pv

    def sh(j, c0):
        return extx_ref[16 - j:16 - j + TS, c0:c0 + LANES]

    s2 = sh(0, 0) + sh(1, 0)
    s4 = s2 + sh(2, 0) + sh(3, 0)
    s8 = sh(0, LANES)
    for j in range(1, 8):
        s8 = s8 + sh(j, LANES)
    s16 = s8
    for j in range(8, 16):
        s16 = s16 + sh(j, LANES)
    lo = lax.broadcasted_iota(I32, (TS, LANES), 1) < 64
    posp1 = (t * TS + 1 + lax.broadcasted_iota(I32, (TS, LANES), 0)).astype(F32)
    d0 = jnp.where(lo, s2, s4) / jnp.where(lo, jnp.minimum(posp1, 2.0), jnp.minimum(posp1, 4.0)) - pv[:, 0:LANES]
    d1 = jnp.where(lo, s8, s16) / jnp.where(lo, jnp.minimum(posp1, 8.0), jnp.minimum(posp1, 16.0)) - pv[:, LANES:]
    d = jnp.concatenate([d0, d1], axis=1)
    mixab_ref[0, :, 256:512] = _dot(d.astype(BF16), pm_ref[...]) * ps_ref[...]
    extx_ref[0:16, :] = pv[TS - 16:TS]
    poolst_ref[0] = pv[TS - 16:TS]

    cq_ref[0] = z[:, C_CQ:C_CQ + 256].astype(BF16)
    cqi_ref[0] = z[:, C_CQI:C_CQI + 512].astype(BF16)
    dq_ref[0] = z[:, C_DQ:C_DQ + 256].astype(BF16)
    dk = z[:, C_DK:C_DK + 256]
    dv = z[:, C_DV:C_DV + 256]
    skk = z[:, C_KK:C_KK + LANES]
    sii = z[:, C_II:C_II + LANES]
    svw = z[:, C_VW:C_VW + LANES]
    nb = TS // KEY_BLK
    kkb_ref[0] = skk.astype(BF16).reshape(nb, KEY_BLK, LANES)
    iib_ref[0] = sii.astype(BF16).reshape(nb, KEY_BLK, LANES)
    dkb_ref[0] = dk.astype(BF16).reshape(nb, KEY_BLK, 256)
    svw_t = svw.T
    dv_t = dv.T
    ckT_ref[0, 0] = skk.T[0:64]
    ckiT_ref[0, 0] = sii.T[0:64]
    cvTf_ref[0, 0] = svw_t[0:64]
    dkT_ref[0, 0] = dk.T
    dvTf_ref[0, 0] = dv_t
    cwiT_ref[0] = svw_t[64:72]
    ones_rows = jnp.where(lax.broadcasted_iota(I32, (V_AUG - 64, KEY_BLK), 0) == 0, 1.0, 0.0).astype(BF16)
    for j in range(nb):
        cvT_ref[0, j] = jnp.concatenate(
            [svw_t[0:64, j * KEY_BLK:(j + 1) * KEY_BLK].astype(BF16), ones_rows], axis=0)
        for h in range(DIFF_HEADS):
            dvT_ref[0, j, h] = jnp.concatenate(
                [dv_t[h * 64:(h + 1) * 64, j * KEY_BLK:(j + 1) * KEY_BLK].astype(BF16), ones_rows], axis=0)


def _k1p(x, g, w, ca, pm, ps, kv_out, *, TS, layer):
    B, T, D = x.shape
    nt = T // TS
    nb = TS // KEY_BLK
    nkb = T // KEY_BLK

    def row_spec(c):
        return pl.BlockSpec((1, TS, c), lambda b, t: (b, t, 0))

    def blk_spec(r, c):
        return pl.BlockSpec((1, nb, r, c), lambda b, t: (b, t, 0, 0))

    out_shape = (
        jax.ShapeDtypeStruct((B, T, 512), F32),
        jax.ShapeDtypeStruct((B, T, 256), BF16),
        jax.ShapeDtypeStruct((B, T, 512), BF16),
        jax.ShapeDtypeStruct((B, T, 256), BF16),
        jax.ShapeDtypeStruct((B, nkb, KEY_BLK, LANES), BF16),
        jax.ShapeDtypeStruct((B, nkb, KEY_BLK, LANES), BF16),
        jax.ShapeDtypeStruct((B, nkb, KEY_BLK, 256), BF16),
        jax.ShapeDtypeStruct((B, nkb, V_AUG, KEY_BLK), BF16),
        jax.ShapeDtypeStruct((B, 8, T), F32),
        jax.ShapeDtypeStruct((B, nkb, DIFF_HEADS, V_AUG, KEY_BLK), BF16),
        jax.ShapeDtypeStruct((B, 8, 256), F32),
        jax.ShapeDtypeStruct((B, 16, 256), F32),
    ) + tuple(jax.ShapeDtypeStruct(a.shape, a.dtype) for a in kv_out)

    def kv_spec(a):
        return pl.BlockSpec((1, 1, a.shape[2], TS), lambda b, t: (b, layer, 0, t))

    out_specs = (
        row_spec(512), row_spec(256), row_spec(512), row_spec(256),
        blk_spec(KEY_BLK, LANES), blk_spec(KEY_BLK, LANES), blk_spec(KEY_BLK, 256),
        blk_spec(V_AUG, KEY_BLK),
        pl.BlockSpec((1, 8, TS), lambda b, t: (b, 0, t)),
        pl.BlockSpec((1, nb, DIFF_HEADS, V_AUG, KEY_BLK), lambda b, t: (b, t, 0, 0, 0)),
        pl.BlockSpec((1, 8, 256), lambda b, t: (b, 0, 0)),
        pl.BlockSpec((1, 16, 256), lambda b, t: (b, 0, 0)),
    ) + tuple(kv_spec(a) for a in kv_out)
    in_specs = [
        row_spec(D), _const_spec(g.shape), _const_spec(w.shape), _const_spec(ca.shape),
        _const_spec(pm.shape), _const_spec(ps.shape),
    ] + [pl.BlockSpec(memory_space=pl.ANY)] * len(kv_out)
    n_in, n_out = 6, len(out_shape) - len(kv_out)
    return pl.pallas_call(
        functools.partial(_k1p_kernel, TS=TS),
        grid=(B, nt), in_specs=in_specs, out_specs=out_specs, out_shape=out_shape,
        input_output_aliases={n_in + k: n_out + k for k in range(len(kv_out))},
        scratch_shapes=[pltpu.VMEM((TS + 8, GROUP_W), F32), pltpu.VMEM((TS + 16, GROUP_W), F32)],
        compiler_params=pltpu.CompilerParams(dimension_semantics=("arbitrary", "arbitrary"),
                                             vmem_limit_bytes=VMEM_LIMIT),
        name="mixer_in_prompt",
    )(x, g, w, ca, pm, ps, *kv_out)


def _k2p_kernel(cq_ref, cqi_ref, cwiT_ref, dq_ref, kkb_ref, iib_ref, cvT_ref, dkb_ref, dvT_ref,
                lam_ref, subln_ref, out_ref, key_ref, lgc_ref, lgd_ref, accc_ref, accd_ref, mxc_ref,
                *, n_sel, lam_init):
    qb = pl.program_id(1)
    q0 = qb * Q_BLK
    nkb = (q0 + Q_BLK + KEY_BLK - 1) // KEY_BLK
    qpos = q0 + lax.broadcasted_iota(I32, (KEY_BLK, Q_BLK), 1)
    krow = lax.broadcasted_iota(I32, (KEY_BLK, Q_BLK), 0)
    lane_q = lax.broadcasted_iota(I32, (Q_BLK, LANES), 1)
    lo_half = lane_q < 64
    NG = KEY_BLK // SUBLANES

    def head_pair(tile):
        zero = jnp.zeros_like(tile)
        return jnp.concatenate([jnp.where(lo_half, tile, zero), jnp.where(lo_half, zero, tile)], axis=0)

    cqi = cqi_ref[0]
    wT = cwiT_ref[0]
    qi = [head_pair(cqi[:, j * LANES:(j + 1) * LANES]) for j in range(IDX_HEADS // 2)]
    cq = cq_ref[0] * (DSA_HD ** -0.5)
    qc = [head_pair(cq[:, j * LANES:(j + 1) * LANES]) for j in range(DSA_HEADS // 2)]
    dq = dq_ref[0]
    lane256 = lax.broadcasted_iota(I32, (Q_BLK, 256), 1)

    def map_rows(h, mm):
        c0 = h * 64 + mm * DIFF_QD
        return jnp.where(jnp.logical_and(lane256 >= c0, lane256 < c0 + DIFF_QD), dq, jnp.zeros_like(dq))

    qd = [jnp.concatenate([map_rows(h, 0), map_rows(h, 1)], axis=0) for h in range(DIFF_HEADS)]

    def pass_a(kb, mx, masked):
        ki = iib_ref[0, kb]
        acc = jnp.zeros((KEY_BLK, Q_BLK), F32)
        for j in range(IDX_HEADS // 2):
            s = _dot_nt(ki, qi[j])
            acc = acc + jnp.maximum(s[:, :Q_BLK], 0.0) * wT[2 * j:2 * j + 1]
            acc = acc + jnp.maximum(s[:, Q_BLK:], 0.0) * wT[2 * j + 1:2 * j + 2]
        key = _float_key(acc * IDX_SCALE + 0.0)
        if masked:
            causal = (kb * KEY_BLK + krow) <= qpos
            key = jnp.where(causal, key, INT_MIN)
        key_ref[kb] = key
        kk = kkb_ref[0, kb]
        for j in range(DSA_HEADS // 2):
            lgc_ref[j, kb] = _dot_nt(kk, qc[j])
        kd = dkb_ref[0, kb]
        new = list(mx)
        for h in range(DIFF_HEADS):
            s = _dot_nt(kd, qd[h])
            if masked:
                s = jnp.where(jnp.concatenate([causal, causal], axis=1), s, NEG)
            lgd_ref[h, kb] = s
            new[h] = jnp.maximum(new[h], jnp.max(s.reshape(NG, SUBLANES, 2 * Q_BLK), axis=0))
        return tuple(new)

    mx = lax.fori_loop(0, nkb - 1, lambda kb, m: pass_a(kb, m, False),
                       tuple(jnp.full((SUBLANES, 2 * Q_BLK), NEG, F32) for _ in range(DIFF_HEADS)))
    mx = pass_a(nkb - 1, mx, True)

    @pl.when(nkb % 2 == 1)
    def _():
        key_ref[nkb] = jnp.full((KEY_BLK, Q_BLK), INT_MIN, I32)

    def count(pred):
        def body(i, acc):
            for u in range(2):
                kb = 2 * i + u
                c = jnp.where(pred(kb, key_ref[kb]), 1, 0)
                acc = acc + jnp.sum(c.reshape(NG, SUBLANES, Q_BLK), axis=0)
            return acc
        acc = lax.fori_loop(0, (nkb + 1) // 2, body, jnp.zeros((SUBLANES, Q_BLK), I32))
        return jnp.sum(acc, axis=0, keepdims=True)

    def search_body(it, base):
        cand = base + lax.shift_left(jnp.int32(1), 31 - it)
        cnt = count(lambda kb, k: k >= cand)
        return jnp.where(cnt >= n_sel, cand, base)

    thr = lax.fori_loop(0, 32, search_body, jnp.full((1, Q_BLK), INT_MIN, I32))

    cnt_gt = count(lambda kb, k: k > thr)
    cnt_eq = count(lambda kb, k: k == thr)
    need = n_sel - cnt_gt
    excess = jnp.logical_and(cnt_eq > need, thr > INT_MIN)
    any_excess = jnp.max(jnp.where(excess, 1, 0))

    @pl.when(any_excess > 0)
    def _():
        def tie_body(it, jp):
            c = jp + lax.shift_left(jnp.int32(1), 10 - it)
            f = count(lambda kb, k: jnp.logical_and(k == thr, (kb * KEY_BLK + krow) < c))
            return jnp.where(f < need, c, jp)
        jp = lax.fori_loop(0, 11, tie_body, jnp.zeros((1, Q_BLK), I32))

        def demote(kb, carry):
            k = key_ref[kb]
            drop = jnp.logical_and(jnp.logical_and(k == thr, (kb * KEY_BLK + krow) > jp), excess)
            key_ref[kb] = jnp.where(drop, thr - 1, k)
            return carry
        lax.fori_loop(0, nkb, demote, 0)

    thr_sel = jnp.maximum(thr, INT_MIN + 1)

    def masked_c(j, kb):
        sel = key_ref[kb] >= thr_sel
        lg = lgc_ref[j, kb]
        return jnp.concatenate([jnp.where(sel, lg[:, :Q_BLK], NEG), jnp.where(sel, lg[:, Q_BLK:], NEG)], axis=1)

    c2 = (DIFF_QD ** -0.5) * LOG2E
    md = [jnp.max(m, axis=0, keepdims=True) * c2 for m in mx]
    accd_ref[...] = jnp.zeros(accd_ref.shape, F32)

    mxc_ref[...] = jnp.full(mxc_ref.shape, NEG, F32)

    def over_blocks(step):
        def pair(i, carry):
            step((2 * i, 2 * i + 1))
            return carry
        lax.fori_loop(0, nkb // 2, pair, 0)

        @pl.when(nkb % 2 == 1)
        def _():
            step((nkb - 1,))

    def pass_b(kbs):
        for j in range(DSA_HEADS // 2):
            m = mxc_ref[j]
            for kb in kbs:
                m = jnp.maximum(m, jnp.max(masked_c(j, kb).reshape(NG, SUBLANES, 2 * Q_BLK), axis=0))
            mxc_ref[j] = m
        for h in range(DIFF_HEADS):
            p = jnp.concatenate([jnp.exp2(lgd_ref[h, kb] * c2 - md[h]).astype(BF16) for kb in kbs], axis=0)
            vT = jnp.concatenate([dvT_ref[0, kb, h] for kb in kbs], axis=1)
            accd_ref[h] += _dot(vT, p)

    over_blocks(pass_b)
    mc = [jnp.max(mxc_ref[j], axis=0, keepdims=True) for j in range(DSA_HEADS // 2)]

    accc_ref[...] = jnp.zeros(accc_ref.shape, F32)

    def pass_c(kbs):
        vT = jnp.concatenate([cvT_ref[0, kb] for kb in kbs], axis=1)
        for j in range(DSA_HEADS // 2):
            p = jnp.concatenate([jnp.exp(masked_c(j, kb) - mc[j]).astype(BF16) for kb in kbs], axis=0)
            accc_ref[j] += _dot(vT, p)

    over_blocks(pass_c)
    outs = []
    for h in range(DSA_HEADS):
        a = accc_ref[h // 2, :, (h % 2) * Q_BLK:(h % 2 + 1) * Q_BLK]
        outs.append(a[0:DSA_HD] / a[DSA_HD:DSA_HD + 1])
    out_ref[0, :, 0:256] = jnp.concatenate(outs, axis=0).T

    lam = _lambda_full(lam_ref, lam_init)
    subln = subln_ref[...]
    outs = []
    for h in range(DIFF_HEADS):
        a1 = accd_ref[h, :, 0:Q_BLK]
        a2 = accd_ref[h, :, Q_BLK:2 * Q_BLK]
        o = a1[0:DIFF_VD] / a1[DIFF_VD:DIFF_VD + 1] - lam * (a2[0:DIFF_VD] / a2[DIFF_VD:DIFF_VD + 1])
        ms = jnp.mean(o * o, axis=0, keepdims=True)
        outs.append((o * lax.rsqrt(ms + EPS)) * subln * (1.0 - lam_init))
    out_ref[0, :, 256:512] = jnp.concatenate(outs, axis=0).T


def _k2p(cq, cqi, cwiT, dq, kkb, iib, cvT, dkb, dvT, lam, subln, *, n_sel, lam_init):
    B, T, _ = cq.shape
    nq = T // Q_BLK
    nkb = T // KEY_BLK

    def q_spec(c):
        return pl.BlockSpec((1, Q_BLK, c), lambda b, q: (b, q, 0))

    def seq_spec(r, c):
        return pl.BlockSpec((1, nkb, r, c), lambda b, q: (b, 0, 0, 0))

    in_specs = [
        q_spec(256), q_spec(512), pl.BlockSpec((1, 8, Q_BLK), lambda b, q: (b, 0, q)), q_spec(256),
        seq_spec(KEY_BLK, LANES), seq_spec(KEY_BLK, LANES), seq_spec(V_AUG, KEY_BLK),
        seq_spec(KEY_BLK, 256),
        pl.BlockSpec((1, nkb, DIFF_HEADS, V_AUG, KEY_BLK), lambda b, q: (b, 0, 0, 0, 0)),
        _const_spec(lam.shape), _const_spec(subln.shape),
    ]
    return pl.pallas_call(
        functools.partial(_k2p_kernel, n_sel=n_sel, lam_init=lam_init),
        grid=(B, nq), in_specs=in_specs, out_specs=q_spec(512),
        out_shape=jax.ShapeDtypeStruct((B, T, 512), F32),
        scratch_shapes=[pltpu.VMEM((nkb + 1, KEY_BLK, Q_BLK), I32),
                        pltpu.VMEM((DSA_HEADS // 2, nkb, KEY_BLK, 2 * Q_BLK), F32),
                        pltpu.VMEM((DIFF_HEADS, nkb, KEY_BLK, 2 * Q_BLK), F32),
                        pltpu.VMEM((DSA_HEADS // 2, V_AUG, 2 * Q_BLK), F32),
                        pltpu.VMEM((DIFF_HEADS, V_AUG, 2 * Q_BLK), F32),
                        pltpu.VMEM((DSA_HEADS // 2, SUBLANES, 2 * Q_BLK), F32)],
        compiler_params=pltpu.CompilerParams(dimension_semantics=("arbitrary", "arbitrary"),
                                             vmem_limit_bytes=VMEM_LIMIT),
        name="attention_prompt",
    )(cq, cqi, cwiT, dq, kkb, iib, cvT, dkb, dvT, lam, subln)


def _ple_tail(h2, p_bf, gple, wpg, wple, gfin, final_norm):
    gate = _sigmoid(_dot(_rms(h2, gple).astype(BF16), wpg))
    h3 = h2 + _dot(p_bf, wple) * gate
    if final_norm:
        h3 = _rms(h3, gfin)
    return h3


def _k3p_kernel(h_ref, ab_ref, cd_ref, p_ref, wout_ref, gffn_ref, wg_ref, wu_ref, cf_ref, wd_ref,
                gple_ref, wpg_ref, wple_ref, gfin_ref, hout_ref, ffnst_ref, extg_ref, *, TS, final_norm):
    t = pl.program_id(1)

    @pl.when(t == 0)
    def _():
        extg_ref[0:8, :] = jnp.zeros((8, extg_ref.shape[1]), F32)

    mix = jnp.concatenate([ab_ref[0], cd_ref[0]], axis=1).astype(BF16)
    h1 = h_ref[0] + _dot(mix, wout_ref[...])
    xn2 = _rms(h1, gffn_ref[...]).astype(BF16)
    g = _dot(xn2, wg_ref[...])
    extg_ref[8:8 + TS, :] = g
    p1 = extg_ref[7:7 + TS, :]
    p2 = extg_ref[6:6 + TS, :]
    cf = cf_ref[...]
    gc = p2 * cf[0:1] + p1 * cf[1:2] + g * cf[2:3]
    extg_ref[0:8, :] = g[TS - 8:TS]
    ffnst_ref[0] = g[TS - 8:TS]
    act = (gc * _sigmoid(gc)) * _dot(xn2, wu_ref[...])
    h2 = h1 + _dot(act.astype(BF16), wd_ref[...])
    hout_ref[0] = _ple_tail(h2, p_ref[0].astype(BF16), gple_ref[...], wpg_ref[...], wple_ref[...],
                            gfin_ref[...], final_norm)


def _k3p(h, ab, cd, p, wout, gffn, wg, wu, cf, wd, gple, wpg, wple, gfin, *, TS, final_norm):
    B, T, D = h.shape
    dff = wg.shape[1]
    nt = T // TS

    def row_spec(c):
        return pl.BlockSpec((1, TS, c), lambda b, t: (b, t, 0))

    def w_spec(a):
        nd = a.ndim
        return pl.BlockSpec(a.shape, lambda *_: (0,) * nd, pipeline_mode=pl.Buffered(1))

    in_specs = [row_spec(D), row_spec(512), row_spec(512), row_spec(p.shape[2]),
                w_spec(wout), w_spec(gffn), w_spec(wg), w_spec(wu), w_spec(cf), w_spec(wd),
                w_spec(gple), w_spec(wpg), w_spec(wple), w_spec(gfin)]
    out_specs = (row_spec(D), pl.BlockSpec((1, 8, dff), lambda b, t: (b, 0, 0)))
    out_shape = (jax.ShapeDtypeStruct((B, T, D), F32), jax.ShapeDtypeStruct((B, 8, dff), F32))
    return pl.pallas_call(
        functools.partial(_k3p_kernel, TS=TS, final_norm=final_norm),
        grid=(B, nt), in_specs=in_specs, out_specs=out_specs, out_shape=out_shape,
        scratch_shapes=[pltpu.VMEM((TS + 8, dff), F32)],
        compiler_params=pltpu.CompilerParams(dimension_semantics=("arbitrary", "arbitrary"),
                                             vmem_limit_bytes=VMEM_LIMIT),
        name="channel_mix_prompt",
    )(h, ab, cd, p, wout, gffn, wg, wu, cf, wd, gple, wpg, wple, gfin)


def _k1s_kernel(x_ref, g_ref, w_ref, ca_ref, pm_ref, ps_ref, sconv_ref, spool_ref,
                mixab_ref, cq_ref, cqi_ref, dq_ref, dk_ref, dv_ref, skk_ref, sii_ref, svw_ref,
                u_ref, pv_ref, *, DB, TD, pos0):
    xn = _rms(x_ref[...], g_ref[...])
    z = _dot(xn.astype(BF16), w_ref[...])

    def slab(t, c0, c1):
        return z[t * DB:(t + 1) * DB, c0:c1]

    ca = ca_ref[...]
    us = [slab(t, C_GC, C_GC + 256) * slab(t, C_GH, C_GH + 256) for t in range(TD)]
    extu = [sconv_ref[0], sconv_ref[1]] + us
    pvs = [slab(t, C_PV, C_PV + 256) for t in range(TD)]
    extx = [spool_ref[j] for j in range(POOL_BUF)] + pvs
    lo = lax.broadcasted_iota(I32, (DB, LANES), 1) < 64
    pm = pm_ref[...]
    ps = ps_ref[...]
    for t in range(TD):
        conv = extu[t] * ca[0:1] + extu[t + 1] * ca[1:2] + extu[t + 2] * ca[2:3]
        mixab_ref[t * DB:(t + 1) * DB, 0:256] = slab(t, C_GB, C_GB + 256) * conv
        u_ref[t] = us[t]
        pv_ref[t] = pvs[t]
        e = POOL_BUF + t

        def win(n, c0):
            s = extx[e][:, c0:c0 + LANES]
            for j in range(1, n):
                s = s + extx[e - j][:, c0:c0 + LANES]
            return s

        cnt = [float(min(w, pos0 + t + 1)) for w in POOL_WINDOWS]
        d0 = jnp.where(lo, win(2, 0) / cnt[0], win(4, 0) / cnt[1]) - pvs[t][:, 0:LANES]
        d1 = jnp.where(lo, win(8, LANES) / cnt[2], win(16, LANES) / cnt[3]) - pvs[t][:, LANES:]
        d = jnp.concatenate([d0, d1], axis=1)
        mixab_ref[t * DB:(t + 1) * DB, 256:512] = _dot(d.astype(BF16), pm) * ps

    cq_ref[...] = z[:, C_CQ:C_CQ + 256].astype(BF16)
    cqi_ref[...] = z[:, C_CQI:C_CQI + 512].astype(BF16)
    dq_ref[...] = z[:, C_DQ:C_DQ + 256].astype(BF16)
    dk_ref[...] = z[:, C_DK:C_DK + 256]
    dv_ref[...] = z[:, C_DV:C_DV + 256]
    skk_ref[...] = z[:, C_KK:C_KK + LANES]
    sii_ref[...] = z[:, C_II:C_II + LANES]
    svw_ref[...] = z[:, C_VW:C_VW + LANES]


def _k1s(x, g, w, ca, pm, ps, sconv, spool, *, DB, TD, pos0):
    R = x.shape[0]
    out_shape = (
        jax.ShapeDtypeStruct((R, 512), F32), jax.ShapeDtypeStruct((R, 256), BF16),
        jax.ShapeDtypeStruct((R, 512), BF16), jax.ShapeDtypeStruct((R, 256), BF16),
        jax.ShapeDtypeStruct((R, 256), F32), jax.ShapeDtypeStruct((R, 256), F32),
        jax.ShapeDtypeStruct((R, LANES), F32), jax.ShapeDtypeStruct((R, LANES), F32),
        jax.ShapeDtypeStruct((R, LANES), F32),
        jax.ShapeDtypeStruct((TD, DB, 256), F32), jax.ShapeDtypeStruct((TD, DB, 256), F32),
    )
    args = (x, g, w, ca, pm, ps, sconv, spool)
    return pl.pallas_call(
        functools.partial(_k1s_kernel, DB=DB, TD=TD, pos0=pos0),
        grid=(1,), in_specs=[_const_spec(a.shape) for a in args],
        out_specs=tuple(_const_spec(s.shape) for s in out_shape), out_shape=out_shape,
        compiler_params=pltpu.CompilerParams(dimension_semantics=("arbitrary",), vmem_limit_bytes=VMEM_LIMIT),
        name="mixer_in_sample",
    )(*args)


QP = 8


def _wrap32(v):
    return ((v + 2 ** 31) % 2 ** 32) - 2 ** 31


def _k2s_kernel(pt_ref, qi_ref, wi_ref, qc_ref, qd_ref, kin_ref, kn_ref, vn_ref, dkn_ref, dvn_ref,
                lam_ref, subln_ref, *rest, NP, PS, TD, n_sel, lam_init):
    G = QP // TD
    n = G * NP
    idx_refs, k_refs, v_refs, dk_refs, dv_refs = (rest[i * n:(i + 1) * n] for i in range(5))
    out_ref, key_ref, clog_ref, dlog_ref = rest[5 * n:]
    del pt_ref
    SC = (NP + 1) * PS

    def own_half(x):
        elem0 = (lax.broadcasted_iota(I32, (x.shape[0], PS), 0) % QP) < TD
        return jnp.where(elem0, x[:, :PS], x[:, PS:])

    def split_half(x):
        elem0 = (lax.broadcasted_iota(I32, x.shape, 0) % QP) < TD
        zero = jnp.zeros_like(x)
        return jnp.concatenate([jnp.where(elem0, x, zero), jnp.where(elem0, zero, x)], axis=1)

    def pages(refs, new_ref, p, rows):
        tiles = [(refs[g * NP + p][0, 0] if p < NP else new_ref[0, g]).reshape(rows, PS).astype(BF16)
                 for g in range(G)]
        return jnp.concatenate(tiles, axis=1)

    qi = qi_ref[0]
    wi = wi_ref[0]
    col = lax.broadcasted_iota(I32, (QP, PS), 1)
    qt = lax.broadcasted_iota(I32, (QP, PS), 0) % TD
    new_ok = jnp.logical_and(col < TD, col <= qt)
    for p in range(NP + 1):
        s = own_half(_dot(qi, pages(idx_refs, kin_ref, p, IDX_HD)))
        r = (jnp.maximum(s, 0.0) * wi).reshape(IDX_HEADS, QP, PS)
        score = jnp.sum(r, axis=0) * IDX_SCALE + 0.0
        key = _float_key(score)
        if p == NP:
            key = jnp.where(new_ok, key, INT_MIN)
        key_ref[:, p * PS:(p + 1) * PS] = key

    kcol = lax.broadcasted_iota(I32, (QP, SC), 1)

    def count(pred):
        c = jnp.where(pred(key_ref[...]), 1.0, 0.0)
        return jnp.sum(c, axis=1, keepdims=True)

    thr = jnp.full((QP, 1), INT_MIN, I32)
    for step in range(16):
        sh = 30 - 2 * step
        passed = jnp.zeros((QP, 1), I32)
        for j in (1, 2, 3):
            cnt = count(lambda k, c=thr + _wrap32(j << sh): k >= c)
            passed = passed + jnp.where(cnt >= n_sel, 1, 0)
        thr = thr + lax.shift_left(passed, sh)
    cnt_gt = count(lambda k: k > thr)
    cnt_eq = count(lambda k: k == thr)
    need = n_sel - cnt_gt
    excess = jnp.logical_and(cnt_eq > need, thr > INT_MIN)
    any_excess = jnp.max(jnp.where(excess, 1, 0))

    @pl.when(any_excess > 0)
    def _():
        def tie_body(it, jp):
            c = jp + lax.shift_left(jnp.int32(1), 12 - it)
            f = count(lambda k: jnp.logical_and(k == thr, kcol < c))
            return jnp.where(f < need, c, jp)
        jp = lax.fori_loop(0, 13, tie_body, jnp.zeros((QP, 1), I32))
        k = key_ref[...]
        drop = jnp.logical_and(jnp.logical_and(k == thr, kcol > jp), excess)
        key_ref[...] = jnp.where(drop, thr - 1, k)

    thr_sel = jnp.maximum(thr, INT_MIN + 1)
    sel = key_ref[...] >= thr_sel

    qc = qc_ref[0] * (DSA_HD ** -0.5)
    for p in range(NP + 1):
        clog_ref[:, p * PS:(p + 1) * PS] = own_half(_dot(qc, pages(k_refs, kn_ref, p, DSA_HD)))
    lg = jnp.where(sel[None], clog_ref[...].reshape(DSA_HEADS, QP, SC), NEG)
    lg = lg.reshape(DSA_HEADS * QP, SC)
    m = jnp.max(lg, axis=1, keepdims=True)
    pe = jnp.exp(lg - m)
    l = jnp.sum(pe, axis=1, keepdims=True)
    clog_ref[...] = pe
    acc = jnp.zeros((DSA_HEADS * QP, DSA_HD), F32)
    for p in range(NP + 1):
        pp = split_half(clog_ref[:, p * PS:(p + 1) * PS]).astype(BF16)
        acc = acc + _dot_nt(pp, pages(v_refs, vn_ref, p, DSA_HD))
    oc = acc / l
    for h in range(DSA_HEADS):
        out_ref[0, :, h * 64:(h + 1) * 64] = oc[h * QP:(h + 1) * QP]

    qd0 = qd_ref[0]
    rhead = lax.broadcasted_iota(I32, qd0.shape, 0) // QP
    lane = lax.broadcasted_iota(I32, qd0.shape, 1)
    zero = jnp.zeros_like(qd0)
    qd = jnp.concatenate(
        [jnp.where(jnp.logical_and(lane >= rhead * 64 + mm * DIFF_QD, lane < rhead * 64 + (mm + 1) * DIFF_QD),
                   qd0, zero) for mm in range(2)], axis=0)
    HD = DIFF_HEADS * DIFF_VD
    nrow = 2 * DIFF_HEADS * QP
    for p in range(NP + 1):
        s = own_half(_dot(qd, pages(dk_refs, dkn_ref, p, HD))) * (DIFF_QD ** -0.5)
        if p == NP:
            ncol = lax.broadcasted_iota(I32, (nrow, PS), 1)
            nqt = lax.broadcasted_iota(I32, (nrow, PS), 0) % TD
            s = jnp.where(jnp.logical_and(ncol < TD, ncol <= nqt), s, NEG)
        dlog_ref[:, p * PS:(p + 1) * PS] = s
    lgd = dlog_ref[...]
    md = jnp.max(lgd, axis=1, keepdims=True)
    ped = jnp.exp(lgd - md)
    ld = jnp.sum(ped, axis=1, keepdims=True)
    dlog_ref[...] = ped
    accd = jnp.zeros((nrow, HD), F32)
    for p in range(NP + 1):
        pp = split_half(dlog_ref[:, p * PS:(p + 1) * PS]).astype(BF16)
        accd = accd + _dot_nt(pp, pages(dv_refs, dvn_ref, p, HD))
    od = accd / ld
    half = DIFF_HEADS * QP
    lam = _lambda_full(lam_ref, lam_init)
    subln = subln_ref[...]
    for h in range(DIFF_HEADS):
        o = (od[h * QP:(h + 1) * QP, h * 64:(h + 1) * 64]
             - lam * od[half + h * QP:half + (h + 1) * QP, h * 64:(h + 1) * 64])
        ms = jnp.mean(o * o, axis=-1, keepdims=True)
        out_ref[0, :, 256 + h * 64:256 + (h + 1) * 64] = (o * lax.rsqrt(ms + EPS)) * subln * (1.0 - lam_init)


def _k2s(page_table, qi, wi, qc, qd, kin, kn, vn, dkn, dvn, lam, subln,
         c_idx, c_k, c_v, d_k, d_v, *, layer, TD, n_sel, lam_init):
    DB, NP = page_table.shape
    PS = c_k.shape[3]
    G = QP // TD
    NS = DB // G

    def b_spec(a):
        return pl.BlockSpec((1,) + a.shape[1:], lambda b, pt: (b,) + (0,) * (a.ndim - 1))

    def c_spec(a):
        nd = a.ndim
        return pl.BlockSpec(a.shape, lambda b, pt: (0,) * nd)

    def cpage_spec(g, p):
        return pl.BlockSpec((1, 1, 64, PS), lambda b, pt: (pt[G * b + g, p], layer, 0, 0))

    def dpage_spec(g, p):
        return pl.BlockSpec((1, 1, DIFF_HEADS, 64, PS), lambda b, pt: (pt[G * b + g, p], layer, 0, 0, 0))

    gp = [(g, p) for g in range(G) for p in range(NP)]
    in_specs = [b_spec(qi), b_spec(wi), b_spec(qc), b_spec(qd), b_spec(kin), b_spec(kn), b_spec(vn),
                b_spec(dkn), b_spec(dvn), c_spec(lam), c_spec(subln)]
    for _ in range(3):
        in_specs += [cpage_spec(g, p) for g, p in gp]
    for _ in range(2):
        in_specs += [dpage_spec(g, p) for g, p in gp]
    SC = (NP + 1) * PS
    grid_spec = pltpu.PrefetchScalarGridSpec(
        num_scalar_prefetch=1, grid=(NS,), in_specs=in_specs,
        out_specs=pl.BlockSpec((1, QP, 512), lambda b, pt: (b, 0, 0)),
        scratch_shapes=[pltpu.VMEM((QP, SC), I32), pltpu.VMEM((DSA_HEADS * QP, SC), F32),
                        pltpu.VMEM((2 * DIFF_HEADS * QP, SC), F32)],
    )
    args = [page_table, qi, wi, qc, qd, kin, kn, vn, dkn, dvn, lam, subln]
    args += [c_idx] * (G * NP) + [c_k] * (G * NP) + [c_v] * (G * NP) + [d_k] * (G * NP) + [d_v] * (G * NP)
    return pl.pallas_call(
        functools.partial(_k2s_kernel, NP=NP, PS=PS, TD=TD, n_sel=n_sel, lam_init=lam_init),
        grid_spec=grid_spec, out_shape=jax.ShapeDtypeStruct((NS, QP, 512), F32),
        compiler_params=pltpu.CompilerParams(dimension_semantics=("arbitrary",), vmem_limit_bytes=VMEM_LIMIT),
        name="attention_sample",
    )(*args)


FF_TILE = 256


def _k3s_kernel(h_ref, ab_ref, cd_ref, p_ref, wout_ref, gffn_ref, wg_ref, wu_ref, cf_ref, wd_ref,
                gple_ref, wpg_ref, wple_ref, gfin_ref, sffn_ref, hout_ref, gout_ref,
                h1_ref, xn2_ref, acc_ref, *, DB, TD, final_norm):
    f = pl.program_id(0)

    @pl.when(f == 0)
    def _():
        mix = jnp.concatenate([ab_ref[...], cd_ref[...]], axis=1).astype(BF16)
        h1 = h_ref[...] + _dot(mix, wout_ref[...])
        h1_ref[...] = h1
        xn2_ref[...] = _rms(h1, gffn_ref[...]).astype(BF16)
        acc_ref[...] = jnp.zeros(acc_ref.shape, F32)

    xn2 = xn2_ref[...]
    g = _dot(xn2, wg_ref[...])
    cf = cf_ref[...]
    gs = [sffn_ref[0], sffn_ref[1]] + [g[t * DB:(t + 1) * DB] for t in range(TD)]
    gc = jnp.concatenate([gs[t] * cf[0:1] + gs[t + 1] * cf[1:2] + gs[t + 2] * cf[2:3] for t in range(TD)], axis=0)
    act = (gc * _sigmoid(gc)) * _dot(xn2, wu_ref[...])
    acc_ref[...] += _dot(act.astype(BF16), wd_ref[...])
    gout_ref[0] = gs[TD]
    gout_ref[1] = gs[TD + 1]

    @pl.when(f == pl.num_programs(0) - 1)
    def _():
        h2 = h1_ref[...] + acc_ref[...]
        hout_ref[...] = _ple_tail(h2, p_ref[...].astype(BF16), gple_ref[...], wpg_ref[...], wple_ref[...],
                                  gfin_ref[...], final_norm)


def _k3s(h, ab, cd, p, wout, gffn, wg, wu, cf, wd, gple, wpg, wple, gfin, sffn, *, DB, TD, final_norm):
    R, D = h.shape
    dff = wg.shape[1]
    nf = dff // FF_TILE
    cs = _const_spec
    in_specs = [cs(h.shape), cs(ab.shape), cs(cd.shape), cs(p.shape), cs(wout.shape), cs(gffn.shape),
                pl.BlockSpec((D, FF_TILE), lambda f: (0, f)), pl.BlockSpec((D, FF_TILE), lambda f: (0, f)),
                pl.BlockSpec((3, FF_TILE), lambda f: (0, f)), pl.BlockSpec((FF_TILE, D), lambda f: (f, 0)),
                cs(gple.shape), cs(wpg.shape), cs(wple.shape), cs(gfin.shape),
                pl.BlockSpec((2, DB, FF_TILE), lambda f: (0, 0, f))]
    out_specs = (cs((R, D)), pl.BlockSpec((2, DB, FF_TILE), lambda f: (0, 0, f)))
    out_shape = (jax.ShapeDtypeStruct((R, D), F32), jax.ShapeDtypeStruct((2, DB, dff), F32))
    return pl.pallas_call(
        functools.partial(_k3s_kernel, DB=DB, TD=TD, final_norm=final_norm),
        grid=(nf,), in_specs=in_specs, out_specs=out_specs, out_shape=out_shape,
        scratch_shapes=[pltpu.VMEM((R, D), F32), pltpu.VMEM((R, D), BF16), pltpu.VMEM((R, D), F32)],
        compiler_params=pltpu.CompilerParams(dimension_semantics=("arbitrary",), vmem_limit_bytes=VMEM_LIMIT),
        name="channel_mix_sample",
    )(h, ab, cd, p, wout, gffn, wg, wu, cf, wd, gple, wpg, wple, gfin, sffn)


def _prep_w_in(w_in):
    cuts = np.cumsum(SPLIT_SIZES)[:-1].tolist()
    gb, gc, gh, pv, cq, ck, cv, cqi, cwi, cki, dq, dk, dv = jnp.split(w_in, cuts, axis=-1)
    pad = jnp.zeros(w_in.shape[:-1] + (LANES - 64 - IDX_HEADS,), w_in.dtype)
    cols = [gb, gc, gh, pv, cq, cqi, dq, dk, dv, ck, ck, cki, cki, cv, cwi, pad]
    return jnp.concatenate(cols, axis=-1).astype(BF16)


def _block_diag_maps(pool_maps):
    depth, G, c, _ = pool_maps.shape
    out = jnp.zeros((depth, G * c, G * c), pool_maps.dtype)
    for g in range(G):
        out = out.at[:, g * c:(g + 1) * c, g * c:(g + 1) * c].set(pool_maps[:, g])
    return out.astype(BF16)


def _heads_rows(a, NS, G, TD, nh):
    a = a.reshape(TD, NS, G, nh, 64).transpose(1, 3, 2, 0, 4)
    return a.reshape(NS, nh * QP, 64)


def _new_page(a, NS, G, TD, PS):
    c = a.shape[1]
    a = a.reshape(TD, NS, G, c).transpose(1, 2, 3, 0)
    return jnp.pad(a, ((0, 0), (0, 0), (0, 0), (0, PS - TD)))


def kernel(x_prompt, x_sample, state_conv_a, state_pool, state_ffn, cache_c_k, cache_c_v, cache_c_idx,
           cache_d_k, cache_d_v, page_table, p_prompt, p_sample, norm_mix, w_in, conv_a, pool_maps,
           pool_scale, diff_lambda, diff_subln, w_out, norm_ffn, w_gate, w_up, conv_ffn, w_down,
           norm_ple, w_ple_gate, w_ple, norm_final):
    depth = w_in.shape[0]
    B, T, D = x_prompt.shape
    DB, TD, _ = x_sample.shape
    NP = page_table.shape[1]
    PS = cache_c_k.shape[2]
    past = NP * PS
    dff = w_gate.shape[2]
    TS1 = min(512, T)
    TS3 = min(256, T)

    w_in_p = _prep_w_in(w_in)
    pm_bd = _block_diag_maps(pool_maps)
    w_out_b, w_gate_b, w_up_b, w_down_b = (w.astype(BF16) for w in (w_out, w_gate, w_up, w_down))
    w_pg_b, w_ple_b = w_ple_gate.astype(BF16), w_ple.astype(BF16)
    gfin = norm_final.reshape(1, D)
    c_idx_t, c_k_t, c_v_t = (c.transpose(0, 1, 3, 2) for c in (cache_c_idx, cache_c_k, cache_c_v))
    d_k_t, d_v_t = (c.transpose(0, 1, 3, 4, 2) for c in (cache_d_k, cache_d_v))
    assert QP % TD == 0 and DB % (QP // TD) == 0 and TD >= 2
    G = QP // TD
    NS = DB // G

    def row(a, i):
        return a[i].reshape(1, -1)

    h = x_prompt
    st_p = []
    n_sel_p = min(TOPK_MAX, T // 4)
    kv_p = tuple(jnp.zeros((B, depth, c, T), F32) for c in (64, 64, 64, 256, 256))
    for i in range(depth):
        lam_init = 0.8 - 0.6 * math.exp(-0.3 * i)
        res = _k1p(h, row(norm_mix, i), w_in_p[i], conv_a[i], pm_bd[i], row(pool_scale, i), kv_p,
                   TS=TS1, layer=i)
        (mixab, cq, cqi, dq, kkb, iib, dkb, cvT, cwiT, dvT, convst, poolst) = res[:12]
        kv_p = res[12:]
        mixcd = _k2p(cq, cqi, cwiT, dq, kkb, iib, cvT, dkb, dvT, diff_lambda[i], diff_subln[i].reshape(-1, 1),
                     n_sel=n_sel_p, lam_init=lam_init)
        h, ffnst = _k3p(h, mixab, mixcd, p_prompt[i], w_out_b[i], row(norm_ffn, i), w_gate_b[i], w_up_b[i],
                        conv_ffn[i], w_down_b[i], row(norm_ple, i), w_pg_b[i], w_ple_b[i], gfin,
                        TS=TS3, final_norm=(i == depth - 1))
        st_p.append((convst[:, 6:8], poolst[:, 1:16], ffnst[:, 6:8]))
    y_prompt = h
    c_k_p, c_v_p, c_idx_p = (a.transpose(0, 1, 3, 2) for a in kv_p[:3])
    d_k_p, d_v_p = (a.reshape(B, depth, DIFF_HEADS, 64, T).transpose(0, 1, 4, 2, 3) for a in kv_p[3:])

    hs = x_sample.transpose(1, 0, 2).reshape(TD * DB, D)
    st_s = []
    n_sel_s = min(TOPK_MAX, (past + TD) // 4)
    for i in range(depth):
        lam_init = 0.8 - 0.6 * math.exp(-0.3 * i)
        sconv = state_conv_a[i].transpose(1, 0, 2)
        spool = state_pool[i].transpose(1, 0, 2)
        sffn = state_ffn[i].transpose(1, 0, 2)
        (mixab, cq, cqi, dq, dk, dv, skk, sii, svw, u_new, pv_new) = _k1s(
            hs, row(norm_mix, i), w_in_p[i], conv_a[i], pm_bd[i], row(pool_scale, i), sconv, spool,
            DB=DB, TD=TD, pos0=past)
        qi = _heads_rows(cqi, NS, G, TD, IDX_HEADS)
        wi = svw[:, 64:64 + IDX_HEADS].reshape(TD, NS, G, IDX_HEADS).transpose(1, 3, 2, 0)
        wi = jnp.broadcast_to(wi.reshape(NS, IDX_HEADS * QP, 1), (NS, IDX_HEADS * QP, LANES))
        qc = _heads_rows(cq, NS, G, TD, DSA_HEADS)
        qd = dq.reshape(TD, NS, G, 256).transpose(1, 2, 0, 3).reshape(NS, 1, QP, 256)
        qd = jnp.broadcast_to(qd, (NS, DIFF_HEADS, QP, 256)).reshape(NS, DIFF_HEADS * QP, 256)
        kin = _new_page(sii[:, :64], NS, G, TD, PS)
        kn = _new_page(skk[:, :64], NS, G, TD, PS)
        vn = _new_page(svw[:, :64], NS, G, TD, PS)
        dkn = _new_page(dk, NS, G, TD, PS)
        dvn = _new_page(dv, NS, G, TD, PS)
        ycd = _k2s(page_table, qi, wi, qc, qd, kin, kn, vn, dkn, dvn, diff_lambda[i], row(diff_subln, i),
                   c_idx_t, c_k_t, c_v_t, d_k_t, d_v_t, layer=i, TD=TD, n_sel=n_sel_s, lam_init=lam_init)
        mixcd = ycd.reshape(NS, G, TD, 512).transpose(2, 0, 1, 3).reshape(TD * DB, 512)
        hs, g_new = _k3s(hs, mixab, mixcd, p_sample[i].transpose(1, 0, 2).reshape(TD * DB, -1), w_out_b[i],
                         row(norm_ffn, i), w_gate_b[i], w_up_b[i], conv_ffn[i], w_down_b[i], row(norm_ple, i),
                         w_pg_b[i], w_ple_b[i], gfin, sffn, DB=DB, TD=TD, final_norm=(i == depth - 1))

        def bm(a, width):
            return a[:, :width].reshape(TD, DB, width).transpose(1, 0, 2)

        new_conv = u_new[TD - 2:].transpose(1, 0, 2)
        new_pool = jnp.concatenate([state_pool[i], pv_new.transpose(1, 0, 2)], axis=1)[:, -POOL_BUF:]
        st_s.append((new_conv, new_pool, g_new.transpose(1, 0, 2), bm(skk, 64), bm(svw, 64), bm(sii, 64),
                     bm(dk, 256).reshape(DB, TD, DIFF_HEADS, 64), bm(dv, 256).reshape(DB, TD, DIFF_HEADS, 64)))
    y_sample = hs.reshape(TD, DB, D).transpose(1, 0, 2)

    def col(outs, j, axis):
        return jnp.stack([o[j] for o in outs], axis=axis)

    return (y_prompt, y_sample, col(st_p, 0, 0), col(st_s, 0, 0), col(st_p, 1, 0), col(st_s, 1, 0),
            col(st_p, 2, 0), col(st_s, 2, 0), c_k_p, col(st_s, 3, 1), c_v_p, col(st_s, 4, 1),
            c_idx_p, col(st_s, 5, 1), d_k_p, col(st_s, 6, 1), d_v_p, col(st_s, 7, 1))
```

```python
import functools
import math

import numpy as np
import jax
import jax.numpy as jnp
from jax import lax
from jax.experimental import pallas as pl
from jax.experimental.pallas import tpu as pltpu

F32 = jnp.float32
BF16 = jnp.bfloat16
I32 = jnp.int32

EPS = 1e-6
GROUP_W = 256
POOL_WINDOWS = (2, 4, 8, 16)
POOL_BUF = 15
DSA_HEADS = 4
DSA_HD = 64
IDX_HEADS = 8
IDX_HD = 64
IDX_SCALE = IDX_HEADS ** -0.5 * IDX_HD ** -0.5
TOPK_MAX = 256
DIFF_HEADS = 4
DIFF_VD = 64
DIFF_QD = 32
SPLIT_SIZES = (256, 256, 256, 256, 256, 64, 64, 512, 8, 64, 256, 256, 256)

LANES = 128
SUBLANES = 8
VMEM_LIMIT = 56 * 1024 * 1024

C_GB, C_GC, C_GH, C_PV = 0, 256, 512, 768
C_CQ, C_CQI, C_DQ, C_DK, C_DV = 1024, 1280, 1792, 2048, 2304
C_KK, C_II, C_VW = 2560, 2688, 2816
D_IN_P = 2944

KEY_BLK = 256
Q_BLK = 128
V_AUG = 80
LOG2E = math.log2(math.e)
NEG = -1e30
INT_MIN = -2 ** 31

NT_DIMS = (((1,), (1,)), ((), ()))


def _dot(a, b):
    return jnp.dot(a, b, preferred_element_type=F32)


def _dot_nt(a, b):
    return lax.dot_general(a, b, NT_DIMS, preferred_element_type=F32)


def _rms(x, g):
    ms = jnp.mean(x * x, axis=-1, keepdims=True)
    return (x * lax.rsqrt(ms + EPS)) * g


def _sigmoid(x):
    return 1.0 / (1.0 + jnp.exp(-x))


def _float_key(x):
    b = lax.bitcast_convert_type(x, I32)
    return b ^ (lax.shift_right_arithmetic(b, 31) & 0x7FFFFFFF)


def _lambda_full(lam_ref, lam_init):
    lp = lam_ref[...]
    s1 = jnp.sum(lp[0:1] * lp[1:2], axis=-1, keepdims=True)
    s2 = jnp.sum(lp[2:3] * lp[3:4], axis=-1, keepdims=True)
    return jnp.exp(s1) - jnp.exp(s2) + lam_init


def _const_spec(shape):
    nd = len(shape)
    return pl.BlockSpec(shape, lambda *_: (0,) * nd)


def _k1p_kernel(x_ref, g_ref, w_ref, ca_ref, pm_ref, ps_ref, _a0, _a1, _a2, _a3, _a4,
                mixab_ref, cq_ref, cqi_ref, dq_ref,
                kkb_ref, iib_ref, dkb_ref, cvT_ref, cwiT_ref, dvT_ref, convst_ref, poolst_ref,
                ckT_ref, cvTf_ref, ckiT_ref, dkT_ref, dvTf_ref,
                extu_ref, extx_ref, *, TS):
    t = pl.program_id(1)

    @pl.when(t == 0)
    def _():
        extu_ref[0:8, :] = jnp.zeros((8, GROUP_W), F32)
        extx_ref[0:16, :] = jnp.zeros((16, GROUP_W), F32)

    xn = _rms(x_ref[0], g_ref[...])
    z = _dot(xn.astype(BF16), w_ref[...])

    u = z[:, C_GC:C_GC + 256] * z[:, C_GH:C_GH + 256]
    extu_ref[8:8 + TS, :] = u
    p1 = extu_ref[7:7 + TS, :]
    p2 = extu_ref[6:6 + TS, :]
    ca = ca_ref[...]
    conv = p2 * ca[0:1] + p1 * ca[1:2] + u * ca[2:3]
    mixab_ref[0, :, 0:256] = z[:, C_GB:C_GB + 256] * conv
    extu_ref[0:8, :] = u[TS - 8:TS]
    convst_ref[0] = u[TS - 8:TS]

    pv = z[:, C_PV:C_PV + 256]
    extx_ref[16:16 + TS, :] = pv

    def sh(j, c0):
        return extx_ref[16 - j:16 - j + TS, c0:c0 + LANES]

    s2 = sh(0, 0) + sh(1, 0)
    s4 = s2 + sh(2, 0) + sh(3, 0)
    s8 = sh(0, LANES)
    for j in range(1, 8):
        s8 = s8 + sh(j, LANES)
    s16 = s8
    for j in range(8, 16):
        s16 = s16 + sh(j, LANES)
    lo = lax.broadcasted_iota(I32, (TS, LANES), 1) < 64
    posp1 = (t * TS + 1 + lax.broadcasted_iota(I32, (TS, LANES), 0)).astype(F32)
    d0 = jnp.where(lo, s2, s4) / jnp.where(lo, jnp.minimum(posp1, 2.0), jnp.minimum(posp1, 4.0)) - pv[:, 0:LANES]
    d1 = jnp.where(lo, s8, s16) / jnp.where(lo, jnp.minimum(posp1, 8.0), jnp.minimum(posp1, 16.0)) - pv[:, LANES:]
    d = jnp.concatenate([d0, d1], axis=1)
    mixab_ref[0, :, 256:512] = _dot(d.astype(BF16), pm_ref[...]) * ps_ref[...]
    extx_ref[0:16, :] = pv[TS - 16:TS]
    poolst_ref[0] = pv[TS - 16:TS]

    cq_ref[0] = z[:, C_CQ:C_CQ + 256].astype(BF16)
    cqi_ref[0] = z[:, C_CQI:C_CQI + 512].astype(BF16)
    dq_ref[0] = z[:, C_DQ:C_DQ + 256].astype(BF16)
    dk = z[:, C_DK:C_DK + 256]
    dv = z[:, C_DV:C_DV + 256]
    skk = z[:, C_KK:C_KK + LANES]
    sii = z[:, C_II:C_II + LANES]
    svw = z[:, C_VW:C_VW + LANES]
    nb = TS // KEY_BLK
    kkb_ref[0] = skk.astype(BF16).reshape(nb, KEY_BLK, LANES)
    iib_ref[0] = sii.astype(BF16).reshape(nb, KEY_BLK, LANES)
    dkb_ref[0] = dk.astype(BF16).reshape(nb, KEY_BLK, 256)
    svw_t = svw.T
    dv_t = dv.T
    ckT_ref[0, 0] = skk.T[0:64]
    ckiT_ref[0, 0] = sii.T[0:64]
    cvTf_ref[0, 0] = svw_t[0:64]
    dkT_ref[0, 0] = dk.T
    dvTf_ref[0, 0] = dv_t
    cwiT_ref[0] = svw_t[64:72]
    ones_rows = jnp.where(lax.broadcasted_iota(I32, (V_AUG - 64, KEY_BLK), 0) == 0, 1.0, 0.0).astype(BF16)
    for j in range(nb):
        cvT_ref[0, j] = jnp.concatenate(
            [svw_t[0:64, j * KEY_BLK:(j + 1) * KEY_BLK].astype(BF16), ones_rows], axis=0)
        for h in range(DIFF_HEADS):
            dvT_ref[0, j, h] = jnp.concatenate(
                [dv_t[h * 64:(h + 1) * 64, j * KEY_BLK:(j + 1) * KEY_BLK].astype(BF16), ones_rows], axis=0)


def _k1p(x, g, w, ca, pm, ps, kv_out, *, TS, layer):
    B, T, D = x.shape
    nt = T // TS
    nb = TS // KEY_BLK
    nkb = T // KEY_BLK

    def row_spec(c):
        return pl.BlockSpec((1, TS, c), lambda b, t: (b, t, 0))

    def blk_spec(r, c):
        return pl.BlockSpec((1, nb, r, c), lambda b, t: (b, t, 0, 0))

    out_shape = (
        jax.ShapeDtypeStruct((B, T, 512), F32),
        jax.ShapeDtypeStruct((B, T, 256), BF16),
        jax.ShapeDtypeStruct((B, T, 512), BF16),
        jax.ShapeDtypeStruct((B, T, 256), BF16),
        jax.ShapeDtypeStruct((B, nkb, KEY_BLK, LANES), BF16),
        jax.ShapeDtypeStruct((B, nkb, KEY_BLK, LANES), BF16),
        jax.ShapeDtypeStruct((B, nkb, KEY_BLK, 256), BF16),
        jax.ShapeDtypeStruct((B, nkb, V_AUG, KEY_BLK), BF16),
        jax.ShapeDtypeStruct((B, 8, T), F32),
        jax.ShapeDtypeStruct((B, nkb, DIFF_HEADS, V_AUG, KEY_BLK), BF16),
        jax.ShapeDtypeStruct((B, 8, 256), F32),
        jax.ShapeDtypeStruct((B, 16, 256), F32),
    ) + tuple(jax.ShapeDtypeStruct(a.shape, a.dtype) for a in kv_out)

    def kv_spec(a):
        return pl.BlockSpec((1, 1, a.shape[2], TS), lambda b, t: (b, layer, 0, t))

    out_specs = (
        row_spec(512), row_spec(256), row_spec(512), row_spec(256),
        blk_spec(KEY_BLK, LANES), blk_spec(KEY_BLK, LANES), blk_spec(KEY_BLK, 256),
        blk_spec(V_AUG, KEY_BLK),
        pl.BlockSpec((1, 8, TS), lambda b, t: (b, 0, t)),
        pl.BlockSpec((1, nb, DIFF_HEADS, V_AUG, KEY_BLK), lambda b, t: (b, t, 0, 0, 0)),
        pl.BlockSpec((1, 8, 256), lambda b, t: (b, 0, 0)),
        pl.BlockSpec((1, 16, 256), lambda b, t: (b, 0, 0)),
    ) + tuple(kv_spec(a) for a in kv_out)
    in_specs = [
        row_spec(D), _const_spec(g.shape), _const_spec(w.shape), _const_spec(ca.shape),
        _const_spec(pm.shape), _const_spec(ps.shape),
    ] + [pl.BlockSpec(memory_space=pl.ANY)] * len(kv_out)
    n_in, n_out = 6, len(out_shape) - len(kv_out)
    return pl.pallas_call(
        functools.partial(_k1p_kernel, TS=TS),
        grid=(B, nt), in_specs=in_specs, out_specs=out_specs, out_shape=out_shape,
        input_output_aliases={n_in + k: n_out + k for k in range(len(kv_out))},
        scratch_shapes=[pltpu.VMEM((TS + 8, GROUP_W), F32), pltpu.VMEM((TS + 16, GROUP_W), F32)],
        compiler_params=pltpu.CompilerParams(dimension_semantics=("arbitrary", "arbitrary"),
                                             vmem_limit_bytes=VMEM_LIMIT),
        name="mixer_in_prompt",
    )(x, g, w, ca, pm, ps, *kv_out)


def _k2p_kernel(cq_ref, cqi_ref, cwiT_ref, dq_ref, kkb_ref, iib_ref, cvT_ref, dkb_ref, dvT_ref,
                lam_ref, subln_ref, out_ref, key_ref, lgc_ref, lgd_ref, accc_ref, accd_ref, mxc_ref, thr_ref,
                *, n_sel, lam_init):
    qb = pl.program_id(1)
    q0 = qb * Q_BLK
    nkb = (q0 + Q_BLK + KEY_BLK - 1) // KEY_BLK
    qpos = q0 + lax.broadcasted_iota(I32, (KEY_BLK, Q_BLK), 1)
    krow = lax.broadcasted_iota(I32, (KEY_BLK, Q_BLK), 0)
    lane_q = lax.broadcasted_iota(I32, (Q_BLK, LANES), 1)
    lo_half = lane_q < 64
    NG = KEY_BLK // SUBLANES

    def head_pair(tile):
        zero = jnp.zeros_like(tile)
        return jnp.concatenate([jnp.where(lo_half, tile, zero), jnp.where(lo_half, zero, tile)], axis=0)

    cqi = cqi_ref[0]
    wT = cwiT_ref[0]
    qi = [head_pair(cqi[:, j * LANES:(j + 1) * LANES]) for j in range(IDX_HEADS // 2)]
    cq = cq_ref[0] * (DSA_HD ** -0.5)
    qc = [head_pair(cq[:, j * LANES:(j + 1) * LANES]) for j in range(DSA_HEADS // 2)]
    dq = dq_ref[0]
    lane256 = lax.broadcasted_iota(I32, (Q_BLK, 256), 1)

    def map_rows(h, mm):
        c0 = h * 64 + mm * DIFF_QD
        return jnp.where(jnp.logical_and(lane256 >= c0, lane256 < c0 + DIFF_QD), dq, jnp.zeros_like(dq))

    qd = [jnp.concatenate([map_rows(h, 0), map_rows(h, 1)], axis=0) for h in range(DIFF_HEADS)]

    def pass_a(kb, mx, masked):
        ki = iib_ref[0, kb]
        acc = jnp.zeros((KEY_BLK, Q_BLK), F32)
        for j in range(IDX_HEADS // 2):
            s = _dot_nt(ki, qi[j])
            acc = acc + jnp.maximum(s[:, :Q_BLK], 0.0) * wT[2 * j:2 * j + 1]
            acc = acc + jnp.maximum(s[:, Q_BLK:], 0.0) * wT[2 * j + 1:2 * j + 2]
        key = _float_key(acc * IDX_SCALE + 0.0)
        if masked:
            causal = (kb * KEY_BLK + krow) <= qpos
            key = jnp.where(causal, key, INT_MIN)
        key_ref[kb] = key
        kk = kkb_ref[0, kb]
        for j in range(DSA_HEADS // 2):
            lgc_ref[j, kb] = _dot_nt(kk, qc[j])
        kd = dkb_ref[0, kb]
        new = list(mx)
        for h in range(DIFF_HEADS):
            s = _dot_nt(kd, qd[h])
            if masked:
                s = jnp.where(jnp.concatenate([causal, causal], axis=1), s, NEG)
            lgd_ref[h, kb] = s
            new[h] = jnp.maximum(new[h], jnp.max(s.reshape(NG, SUBLANES, 2 * Q_BLK), axis=0))
        return tuple(new)

    n_open = nkb - 1
    mx = lax.fori_loop(0, n_open // 2, lambda i, m: pass_a(2 * i + 1, pass_a(2 * i, m, False), False),
                       tuple(jnp.full((SUBLANES, 2 * Q_BLK), NEG, F32) for _ in range(DIFF_HEADS)))
    mx = lax.cond(n_open % 2 == 1, lambda m: pass_a(n_open - 1, m, False), lambda m: m, mx)
    mx = pass_a(nkb - 1, mx, True)

    @pl.when(nkb % 2 == 1)
    def _():
        key_ref[nkb] = jnp.full((KEY_BLK, Q_BLK), INT_MIN, I32)

    def count(pred):
        def body(i, acc):
            for u in range(2):
                kb = 2 * i + u
                c = jnp.where(pred(kb, key_ref[kb]), 1, 0)
                acc = acc + jnp.sum(c.reshape(NG, SUBLANES, Q_BLK), axis=0)
            return acc
        acc = lax.fori_loop(0, (nkb + 1) // 2, body, jnp.zeros((SUBLANES, Q_BLK), I32))
        return jnp.sum(acc, axis=0, keepdims=True)

    def search(npairs):
        def step(it, base):
            cand = base + lax.shift_left(jnp.int32(1), 31 - it)
            acc = jnp.zeros((SUBLANES, Q_BLK), I32)
            for kb in range(2 * npairs):
                acc = acc + jnp.sum(jnp.where(key_ref[kb] >= cand, 1, 0).reshape(NG, SUBLANES, Q_BLK), axis=0)
            cnt = jnp.sum(acc, axis=0, keepdims=True)
            return jnp.where(cnt >= n_sel, cand, base)
        return lax.fori_loop(0, 32, step, jnp.full((1, Q_BLK), INT_MIN, I32))

    for npairs in range(1, key_ref.shape[0] // 2 + 1):
        @pl.when((nkb + 1) // 2 == npairs)
        def _(npairs=npairs):
            thr_ref[...] = search(npairs)

    thr = thr_ref[...]

    cnt_gt = count(lambda kb, k: k > thr)
    cnt_eq = count(lambda kb, k: k == thr)
    need = n_sel - cnt_gt
    excess = jnp.logical_and(cnt_eq > need, thr > INT_MIN)
    any_excess = jnp.max(jnp.where(excess, 1, 0))

    @pl.when(any_excess > 0)
    def _():
        def tie_body(it, jp):
            c = jp + lax.shift_left(jnp.int32(1), 10 - it)
            f = count(lambda kb, k: jnp.logical_and(k == thr, (kb * KEY_BLK + krow) < c))
            return jnp.where(f < need, c, jp)
        jp = lax.fori_loop(0, 11, tie_body, jnp.zeros((1, Q_BLK), I32))

        def demote(kb, carry):
            k = key_ref[kb]
            drop = jnp.logical_and(jnp.logical_and(k == thr, (kb * KEY_BLK + krow) > jp), excess)
            key_ref[kb] = jnp.where(drop, thr - 1, k)
            return carry
        lax.fori_loop(0, nkb, demote, 0)

    thr_sel = jnp.maximum(thr, INT_MIN + 1)

    def masked_c(j, kb):
        sel = key_ref[kb] >= thr_sel
        lg = lgc_ref[j, kb]
        return jnp.concatenate([jnp.where(sel, lg[:, :Q_BLK], NEG), jnp.where(sel, lg[:, Q_BLK:], NEG)], axis=1)

    c2 = (DIFF_QD ** -0.5) * LOG2E
    md = [jnp.max(m, axis=0, keepdims=True) * c2 for m in mx]
    accd_ref[...] = jnp.zeros(accd_ref.shape, F32)

    mxc_ref[...] = jnp.full(mxc_ref.shape, NEG, F32)

    def over_blocks(step):
        def pair(i, carry):
            step((2 * i, 2 * i + 1))
            return carry
        lax.fori_loop(0, nkb // 2, pair, 0)

        @pl.when(nkb % 2 == 1)
        def _():
            step((nkb - 1,))

    def pass_b(kbs):
        for j in range(DSA_HEADS // 2):
            m = mxc_ref[j]
            for kb in kbs:
                m = jnp.maximum(m, jnp.max(masked_c(j, kb).reshape(NG, SUBLANES, 2 * Q_BLK), axis=0))
            mxc_ref[j] = m
        for h in range(DIFF_HEADS):
            p = jnp.concatenate([jnp.exp2(lgd_ref[h, kb] * c2 - md[h]).astype(BF16) for kb in kbs], axis=0)
            vT = jnp.concatenate([dvT_ref[0, kb, h] for kb in kbs], axis=1)
            accd_ref[h] += _dot(vT, p)

    over_blocks(pass_b)
    mc = [jnp.max(mxc_ref[j], axis=0, keepdims=True) for j in range(DSA_HEADS // 2)]

    accc_ref[...] = jnp.zeros(accc_ref.shape, F32)

    def pass_c(kbs):
        vT = jnp.concatenate([cvT_ref[0, kb] for kb in kbs], axis=1)
        for j in range(DSA_HEADS // 2):
            p = jnp.concatenate([jnp.exp(masked_c(j, kb) - mc[j]).astype(BF16) for kb in kbs], axis=0)
            accc_ref[j] += _dot(vT, p)

    over_blocks(pass_c)
    outs = []
    for h in range(DSA_HEADS):
        a = accc_ref[h // 2, :, (h % 2) * Q_BLK:(h % 2 + 1) * Q_BLK]
        outs.append(a[0:DSA_HD] / a[DSA_HD:DSA_HD + 1])
    out_ref[0, :, 0:256] = jnp.concatenate(outs, axis=0).T

    lam = _lambda_full(lam_ref, lam_init)
    subln = subln_ref[...]
    outs = []
    for h in range(DIFF_HEADS):
        a1 = accd_ref[h, :, 0:Q_BLK]
        a2 = accd_ref[h, :, Q_BLK:2 * Q_BLK]
        o = a1[0:DIFF_VD] / a1[DIFF_VD:DIFF_VD + 1] - lam * (a2[0:DIFF_VD] / a2[DIFF_VD:DIFF_VD + 1])
        ms = jnp.mean(o * o, axis=0, keepdims=True)
        outs.append((o * lax.rsqrt(ms + EPS)) * subln * (1.0 - lam_init))
    out_ref[0, :, 256:512] = jnp.concatenate(outs, axis=0).T


def _k2p(cq, cqi, cwiT, dq, kkb, iib, cvT, dkb, dvT, lam, subln, *, n_sel, lam_init):
    B, T, _ = cq.shape
    nq = T // Q_BLK
    nkb = T // KEY_BLK

    def q_spec(c):
        return pl.BlockSpec((1, Q_BLK, c), lambda b, q: (b, q, 0))

    def seq_spec(r, c):
        return pl.BlockSpec((1, nkb, r, c), lambda b, q: (b, 0, 0, 0))

    in_specs = [
        q_spec(256), q_spec(512), pl.BlockSpec((1, 8, Q_BLK), lambda b, q: (b, 0, q)), q_spec(256),
        seq_spec(KEY_BLK, LANES), seq_spec(KEY_BLK, LANES), seq_spec(V_AUG, KEY_BLK),
        seq_spec(KEY_BLK, 256),
        pl.BlockSpec((1, nkb, DIFF_HEADS, V_AUG, KEY_BLK), lambda b, q: (b, 0, 0, 0, 0)),
        _const_spec(lam.shape), _const_spec(subln.shape),
    ]
    return pl.pallas_call(
        functools.partial(_k2p_kernel, n_sel=n_sel, lam_init=lam_init),
        grid=(B, nq), in_specs=in_specs, out_specs=q_spec(512),
        out_shape=jax.ShapeDtypeStruct((B, T, 512), F32),
        scratch_shapes=[pltpu.VMEM((nkb + 1, KEY_BLK, Q_BLK), I32),
                        pltpu.VMEM((DSA_HEADS // 2, nkb, KEY_BLK, 2 * Q_BLK), F32),
                        pltpu.VMEM((DIFF_HEADS, nkb, KEY_BLK, 2 * Q_BLK), F32),
                        pltpu.VMEM((DSA_HEADS // 2, V_AUG, 2 * Q_BLK), F32),
                        pltpu.VMEM((DIFF_HEADS, V_AUG, 2 * Q_BLK), F32),
                        pltpu.VMEM((DSA_HEADS // 2, SUBLANES, 2 * Q_BLK), F32),
                        pltpu.VMEM((1, Q_BLK), I32)],
        compiler_params=pltpu.CompilerParams(dimension_semantics=("arbitrary", "arbitrary"),
                                             vmem_limit_bytes=VMEM_LIMIT),
        name="attention_prompt",
    )(cq, cqi, cwiT, dq, kkb, iib, cvT, dkb, dvT, lam, subln)


def _ple_tail(h2, p_bf, gple, wpg, wple, gfin, final_norm):
    gate = _sigmoid(_dot(_rms(h2, gple).astype(BF16), wpg))
    h3 = h2 + _dot(p_bf, wple) * gate
    if final_norm:
        h3 = _rms(h3, gfin)
    return h3


def _k3p_kernel(h_ref, ab_ref, cd_ref, p_ref, wout_ref, gffn_ref, wg_ref, wu_ref, cf_ref, wd_ref,
                gple_ref, wpg_ref, wple_ref, gfin_ref, hout_ref, ffnst_ref, extg_ref, *, TS, final_norm):
    t = pl.program_id(1)

    @pl.when(t == 0)
    def _():
        extg_ref[0:8, :] = jnp.zeros((8, extg_ref.shape[1]), F32)

    mix = jnp.concatenate([ab_ref[0], cd_ref[0]], axis=1).astype(BF16)
    h1 = h_ref[0] + _dot(mix, wout_ref[...])
    xn2 = _rms(h1, gffn_ref[...]).astype(BF16)
    g = _dot(xn2, wg_ref[...])
    extg_ref[8:8 + TS, :] = g
    p1 = extg_ref[7:7 + TS, :]
    p2 = extg_ref[6:6 + TS, :]
    cf = cf_ref[...]
    gc = p2 * cf[0:1] + p1 * cf[1:2] + g * cf[2:3]
    extg_ref[0:8, :] = g[TS - 8:TS]
    ffnst_ref[0] = g[TS - 8:TS]
    act = (gc * _sigmoid(gc)) * _dot(xn2, wu_ref[...])
    h2 = h1 + _dot(act.astype(BF16), wd_ref[...])
    hout_ref[0] = _ple_tail(h2, p_ref[0].astype(BF16), gple_ref[...], wpg_ref[...], wple_ref[...],
                            gfin_ref[...], final_norm)


def _k3p(h, ab, cd, p, wout, gffn, wg, wu, cf, wd, gple, wpg, wple, gfin, *, TS, final_norm):
    B, T, D = h.shape
    dff = wg.shape[1]
    nt = T // TS

    def row_spec(c):
        return pl.BlockSpec((1, TS, c), lambda b, t: (b, t, 0))

    def w_spec(a):
        nd = a.ndim
        return pl.BlockSpec(a.shape, lambda *_: (0,) * nd, pipeline_mode=pl.Buffered(1))

    in_specs = [row_spec(D), row_spec(512), row_spec(512), row_spec(p.shape[2]),
                w_spec(wout), w_spec(gffn), w_spec(wg), w_spec(wu), w_spec(cf), w_spec(wd),
                w_spec(gple), w_spec(wpg), w_spec(wple), w_spec(gfin)]
    out_specs = (row_spec(D), pl.BlockSpec((1, 8, dff), lambda b, t: (b, 0, 0)))
    out_shape = (jax.ShapeDtypeStruct((B, T, D), F32), jax.ShapeDtypeStruct((B, 8, dff), F32))
    return pl.pallas_call(
        functools.partial(_k3p_kernel, TS=TS, final_norm=final_norm),
        grid=(B, nt), in_specs=in_specs, out_specs=out_specs, out_shape=out_shape,
        scratch_shapes=[pltpu.VMEM((TS + 8, dff), F32)],
        compiler_params=pltpu.CompilerParams(dimension_semantics=("arbitrary", "arbitrary"),
                                             vmem_limit_bytes=VMEM_LIMIT),
        name="channel_mix_prompt",
    )(h, ab, cd, p, wout, gffn, wg, wu, cf, wd, gple, wpg, wple, gfin)


def _k1s_kernel(x_ref, g_ref, w_ref, ca_ref, pm_ref, ps_ref, sconv_ref, spool_ref,
                mixab_ref, cq_ref, cqi_ref, dq_ref, dk_ref, dv_ref, skk_ref, sii_ref, svw_ref,
                u_ref, pv_ref, *, DB, TD, pos0):
    xn = _rms(x_ref[...], g_ref[...])
    z = _dot(xn.astype(BF16), w_ref[...])

    def slab(t, c0, c1):
        return z[t * DB:(t + 1) * DB, c0:c1]

    ca = ca_ref[...]
    us = [slab(t, C_GC, C_GC + 256) * slab(t, C_GH, C_GH + 256) for t in range(TD)]
    extu = [sconv_ref[0], sconv_ref[1]] + us
    pvs = [slab(t, C_PV, C_PV + 256) for t in range(TD)]
    extx = [spool_ref[j] for j in range(POOL_BUF)] + pvs
    lo = lax.broadcasted_iota(I32, (DB, LANES), 1) < 64
    pm = pm_ref[...]
    ps = ps_ref[...]
    for t in range(TD):
        conv = extu[t] * ca[0:1] + extu[t + 1] * ca[1:2] + extu[t + 2] * ca[2:3]
        mixab_ref[t * DB:(t + 1) * DB, 0:256] = slab(t, C_GB, C_GB + 256) * conv
        u_ref[t] = us[t]
        pv_ref[t] = pvs[t]
        e = POOL_BUF + t

        def win(n, c0):
            s = extx[e][:, c0:c0 + LANES]
            for j in range(1, n):
                s = s + extx[e - j][:, c0:c0 + LANES]
            return s

        cnt = [float(min(w, pos0 + t + 1)) for w in POOL_WINDOWS]
        d0 = jnp.where(lo, win(2, 0) / cnt[0], win(4, 0) / cnt[1]) - pvs[t][:, 0:LANES]
        d1 = jnp.where(lo, win(8, LANES) / cnt[2], win(16, LANES) / cnt[3]) - pvs[t][:, LANES:]
        d = jnp.concatenate([d0, d1], axis=1)
        mixab_ref[t * DB:(t + 1) * DB, 256:512] = _dot(d.astype(BF16), pm) * ps

    cq_ref[...] = z[:, C_CQ:C_CQ + 256].astype(BF16)
    cqi_ref[...] = z[:, C_CQI:C_CQI + 512].astype(BF16)
    dq_ref[...] = z[:, C_DQ:C_DQ + 256].astype(BF16)
    dk_ref[...] = z[:, C_DK:C_DK + 256]
    dv_ref[...] = z[:, C_DV:C_DV + 256]
    skk_ref[...] = z[:, C_KK:C_KK + LANES]
    sii_ref[...] = z[:, C_II:C_II + LANES]
    svw_ref[...] = z[:, C_VW:C_VW + LANES]


def _k1s(x, g, w, ca, pm, ps, sconv, spool, *, DB, TD, pos0):
    R = x.shape[0]
    out_shape = (
        jax.ShapeDtypeStruct((R, 512), F32), jax.ShapeDtypeStruct((R, 256), BF16),
        jax.ShapeDtypeStruct((R, 512), BF16), jax.ShapeDtypeStruct((R, 256), BF16),
        jax.ShapeDtypeStruct((R, 256), F32), jax.ShapeDtypeStruct((R, 256), F32),
        jax.ShapeDtypeStruct((R, LANES), F32), jax.ShapeDtypeStruct((R, LANES), F32),
        jax.ShapeDtypeStruct((R, LANES), F32),
        jax.ShapeDtypeStruct((TD, DB, 256), F32), jax.ShapeDtypeStruct((TD, DB, 256), F32),
    )
    args = (x, g, w, ca, pm, ps, sconv, spool)
    return pl.pallas_call(
        functools.partial(_k1s_kernel, DB=DB, TD=TD, pos0=pos0),
        grid=(1,), in_specs=[_const_spec(a.shape) for a in args],
        out_specs=tuple(_const_spec(s.shape) for s in out_shape), out_shape=out_shape,
        compiler_params=pltpu.CompilerParams(dimension_semantics=("arbitrary",), vmem_limit_bytes=VMEM_LIMIT),
        name="mixer_in_sample",
    )(*args)


QP = 8


def _wrap32(v):
    return ((v + 2 ** 31) % 2 ** 32) - 2 ** 31


def _k2s_kernel(pt_ref, qi_ref, wi_ref, qc_ref, qd_ref, kin_ref, kn_ref, vn_ref, dkn_ref, dvn_ref,
                lam_ref, subln_ref, *rest, NP, PS, TD, n_sel, lam_init):
    G = QP // TD
    n = G * NP
    idx_refs, k_refs, v_refs, dk_refs, dv_refs = (rest[i * n:(i + 1) * n] for i in range(5))
    out_ref, key_ref, clog_ref, dlog_ref = rest[5 * n:]
    del pt_ref
    SC = (NP + 1) * PS

    def own_half(x):
        elem0 = (lax.broadcasted_iota(I32, (x.shape[0], PS), 0) % QP) < TD
        return jnp.where(elem0, x[:, :PS], x[:, PS:])

    def split_half(x):
        elem0 = (lax.broadcasted_iota(I32, x.shape, 0) % QP) < TD
        zero = jnp.zeros_like(x)
        return jnp.concatenate([jnp.where(elem0, x, zero), jnp.where(elem0, zero, x)], axis=1)

    def pages(refs, new_ref, p, rows):
        tiles = [(refs[g * NP + p][0, 0] if p < NP else new_ref[0, g]).reshape(rows, PS).astype(BF16)
                 for g in range(G)]
        return jnp.concatenate(tiles, axis=1)

    qi = qi_ref[0]
    wi = wi_ref[0]
    col = lax.broadcasted_iota(I32, (QP, PS), 1)
    qt = lax.broadcasted_iota(I32, (QP, PS), 0) % TD
    new_ok = jnp.logical_and(col < TD, col <= qt)
    for p in range(NP + 1):
        s = own_half(_dot(qi, pages(idx_refs, kin_ref, p, IDX_HD)))
        r = (jnp.maximum(s, 0.0) * wi).reshape(IDX_HEADS, QP, PS)
        score = jnp.sum(r, axis=0) * IDX_SCALE + 0.0
        key = _float_key(score)
        if p == NP:
            key = jnp.where(new_ok, key, INT_MIN)
        key_ref[:, p * PS:(p + 1) * PS] = key

    qc = qc_ref[0] * (DSA_HD ** -0.5)
    for p in range(NP + 1):
        clog_ref[:, p * PS:(p + 1) * PS] = own_half(_dot(qc, pages(k_refs, kn_ref, p, DSA_HD)))

    qd0 = qd_ref[0]
    rhead = lax.broadcasted_iota(I32, qd0.shape, 0) // QP
    lane = lax.broadcasted_iota(I32, qd0.shape, 1)
    zero = jnp.zeros_like(qd0)
    qd = jnp.concatenate(
        [jnp.where(jnp.logical_and(lane >= rhead * 64 + mm * DIFF_QD, lane < rhead * 64 + (mm + 1) * DIFF_QD),
                   qd0, zero) for mm in range(2)], axis=0)
    HD = DIFF_HEADS * DIFF_VD
    nrow = 2 * DIFF_HEADS * QP
    for p in range(NP + 1):
        s = own_half(_dot(qd, pages(dk_refs, dkn_ref, p, HD))) * (DIFF_QD ** -0.5)
        if p == NP:
            ncol = lax.broadcasted_iota(I32, (nrow, PS), 1)
            nqt = lax.broadcasted_iota(I32, (nrow, PS), 0) % TD
            s = jnp.where(jnp.logical_and(ncol < TD, ncol <= nqt), s, NEG)
        dlog_ref[:, p * PS:(p + 1) * PS] = s
    lgd = dlog_ref[...]
    md = jnp.max(lgd, axis=1, keepdims=True)
    ped = jnp.exp(lgd - md)
    ld = jnp.sum(ped, axis=1, keepdims=True)
    dlog_ref[...] = ped
    accd = jnp.zeros((nrow, HD), F32)
    for p in range(NP + 1):
        pp = split_half(dlog_ref[:, p * PS:(p + 1) * PS]).astype(BF16)
        accd = accd + _dot_nt(pp, pages(dv_refs, dvn_ref, p, HD))
    od = accd / ld
    half = DIFF_HEADS * QP
    lam = _lambda_full(lam_ref, lam_init)
    subln = subln_ref[...]
    for h in range(DIFF_HEADS):
        o = (od[h * QP:(h + 1) * QP, h * 64:(h + 1) * 64]
             - lam * od[half + h * QP:half + (h + 1) * QP, h * 64:(h + 1) * 64])
        ms = jnp.mean(o * o, axis=-1, keepdims=True)
        out_ref[0, :, 256 + h * 64:256 + (h + 1) * 64] = (o * lax.rsqrt(ms + EPS)) * subln * (1.0 - lam_init)

    kcol = lax.broadcasted_iota(I32, (QP, SC), 1)

    def count(pred):
        c = jnp.where(pred(key_ref[...]), 1.0, 0.0)
        return jnp.sum(c, axis=1, keepdims=True)

    thr = jnp.full((QP, 1), INT_MIN, I32)
    for sh, nbits in [(30, 2)] + [(27 - 3 * s, 3) for s in range(10)]:
        passed = jnp.zeros((QP, 1), I32)
        for j in range(1, 2 ** nbits):
            cnt = count(lambda k, c=thr + _wrap32(j << sh): k >= c)
            passed = passed + jnp.where(cnt >= n_sel, 1, 0)
        thr = thr + lax.shift_left(passed, sh)
    cnt_gt = count(lambda k: k > thr)
    cnt_eq = count(lambda k: k == thr)
    need = n_sel - cnt_gt
    excess = jnp.logical_and(cnt_eq > need, thr > INT_MIN)
    any_excess = jnp.max(jnp.where(excess, 1, 0))

    @pl.when(any_excess > 0)
    def _():
        def tie_body(it, jp):
            c = jp + lax.shift_left(jnp.int32(1), 12 - it)
            f = count(lambda k: jnp.logical_and(k == thr, kcol < c))
            return jnp.where(f < need, c, jp)
        jp = lax.fori_loop(0, 13, tie_body, jnp.zeros((QP, 1), I32))
        k = key_ref[...]
        drop = jnp.logical_and(jnp.logical_and(k == thr, kcol > jp), excess)
        key_ref[...] = jnp.where(drop, thr - 1, k)

    thr_sel = jnp.maximum(thr, INT_MIN + 1)
    sel = key_ref[...] >= thr_sel

    lg = jnp.where(sel[None], clog_ref[...].reshape(DSA_HEADS, QP, SC), NEG)
    lg = lg.reshape(DSA_HEADS * QP, SC)
    m = jnp.max(lg, axis=1, keepdims=True)
    pe = jnp.exp(lg - m)
    l = jnp.sum(pe, axis=1, keepdims=True)
    clog_ref[...] = pe
    acc = jnp.zeros((DSA_HEADS * QP, DSA_HD), F32)
    for p in range(NP + 1):
        pp = split_half(clog_ref[:, p * PS:(p + 1) * PS]).astype(BF16)
        acc = acc + _dot_nt(pp, pages(v_refs, vn_ref, p, DSA_HD))
    oc = acc / l
    for h in range(DSA_HEADS):
        out_ref[0, :, h * 64:(h + 1) * 64] = oc[h * QP:(h + 1) * QP]


def _k2s(page_table, qi, wi, qc, qd, kin, kn, vn, dkn, dvn, lam, subln,
         c_idx, c_k, c_v, d_k, d_v, *, layer, TD, n_sel, lam_init):
    DB, NP = page_table.shape
    PS = c_k.shape[3]
    G = QP // TD
    NS = DB // G

    def b_spec(a):
        return pl.BlockSpec((1,) + a.shape[1:], lambda b, pt: (b,) + (0,) * (a.ndim - 1))

    def c_spec(a):
        nd = a.ndim
        return pl.BlockSpec(a.shape, lambda b, pt: (0,) * nd)

    def cpage_spec(g, p):
        return pl.BlockSpec((1, 1, 64, PS), lambda b, pt: (pt[G * b + g, p], layer, 0, 0))

    def dpage_spec(g, p):
        return pl.BlockSpec((1, 1, DIFF_HEADS, 64, PS), lambda b, pt: (pt[G * b + g, p], layer, 0, 0, 0))

    gp = [(g, p) for g in range(G) for p in range(NP)]
    in_specs = [b_spec(qi), b_spec(wi), b_spec(qc), b_spec(qd), b_spec(kin), b_spec(kn), b_spec(vn),
                b_spec(dkn), b_spec(dvn), c_spec(lam), c_spec(subln)]
    for _ in range(3):
        in_specs += [cpage_spec(g, p) for g, p in gp]
    for _ in range(2):
        in_specs += [dpage_spec(g, p) for g, p in gp]
    SC = (NP + 1) * PS
    grid_spec = pltpu.PrefetchScalarGridSpec(
        num_scalar_prefetch=1, grid=(NS,), in_specs=in_specs,
        out_specs=pl.BlockSpec((1, QP, 512), lambda b, pt: (b, 0, 0)),
        scratch_shapes=[pltpu.VMEM((QP, SC), I32), pltpu.VMEM((DSA_HEADS * QP, SC), F32),
                        pltpu.VMEM((2 * DIFF_HEADS * QP, SC), F32)],
    )
    args = [page_table, qi, wi, qc, qd, kin, kn, vn, dkn, dvn, lam, subln]
    args += [c_idx] * (G * NP) + [c_k] * (G * NP) + [c_v] * (G * NP) + [d_k] * (G * NP) + [d_v] * (G * NP)
    return pl.pallas_call(
        functools.partial(_k2s_kernel, NP=NP, PS=PS, TD=TD, n_sel=n_sel, lam_init=lam_init),
        grid_spec=grid_spec, out_shape=jax.ShapeDtypeStruct((NS, QP, 512), F32),
        compiler_params=pltpu.CompilerParams(dimension_semantics=("arbitrary",), vmem_limit_bytes=VMEM_LIMIT),
        name="attention_sample",
    )(*args)


FF_TILE = 256


def _k3s_kernel(h_ref, ab_ref, cd_ref, p_ref, wout_ref, gffn_ref, wg_ref, wu_ref, cf_ref, wd_ref,
                gple_ref, wpg_ref, wple_ref, gfin_ref, sffn_ref, hout_ref, gout_ref,
                h1_ref, xn2_ref, acc_ref, *, DB, TD, final_norm):
    f = pl.program_id(0)

    @pl.when(f == 0)
    def _():
        mix = jnp.concatenate([ab_ref[...], cd_ref[...]], axis=1).astype(BF16)
        h1 = h_ref[...] + _dot(mix, wout_ref[...])
        h1_ref[...] = h1
        xn2_ref[...] = _rms(h1, gffn_ref[...]).astype(BF16)
        acc_ref[...] = jnp.zeros(acc_ref.shape, F32)

    xn2 = xn2_ref[...]
    g = _dot(xn2, wg_ref[...])
    cf = cf_ref[...]
    gs = [sffn_ref[0], sffn_ref[1]] + [g[t * DB:(t + 1) * DB] for t in range(TD)]
    gc = jnp.concatenate([gs[t] * cf[0:1] + gs[t + 1] * cf[1:2] + gs[t + 2] * cf[2:3] for t in range(TD)], axis=0)
    act = (gc * _sigmoid(gc)) * _dot(xn2, wu_ref[...])
    acc_ref[...] += _dot(act.astype(BF16), wd_ref[...])
    gout_ref[0] = gs[TD]
    gout_ref[1] = gs[TD + 1]

    @pl.when(f == pl.num_programs(0) - 1)
    def _():
        h2 = h1_ref[...] + acc_ref[...]
        hout_ref[...] = _ple_tail(h2, p_ref[...].astype(BF16), gple_ref[...], wpg_ref[...], wple_ref[...],
                                  gfin_ref[...], final_norm)


def _k3s(h, ab, cd, p, wout, gffn, wg, wu, cf, wd, gple, wpg, wple, gfin, sffn, *, DB, TD, final_norm):
    R, D = h.shape
    dff = wg.shape[1]
    nf = dff // FF_TILE
    cs = _const_spec
    in_specs = [cs(h.shape), cs(ab.shape), cs(cd.shape), cs(p.shape), cs(wout.shape), cs(gffn.shape),
                pl.BlockSpec((D, FF_TILE), lambda f: (0, f)), pl.BlockSpec((D, FF_TILE), lambda f: (0, f)),
                pl.BlockSpec((3, FF_TILE), lambda f: (0, f)), pl.BlockSpec((FF_TILE, D), lambda f: (f, 0)),
                cs(gple.shape), cs(wpg.shape), cs(wple.shape), cs(gfin.shape),
                pl.BlockSpec((2, DB, FF_TILE), lambda f: (0, 0, f))]
    out_specs = (cs((R, D)), pl.BlockSpec((2, DB, FF_TILE), lambda f: (0, 0, f)))
    out_shape = (jax.ShapeDtypeStruct((R, D), F32), jax.ShapeDtypeStruct((2, DB, dff), F32))
    return pl.pallas_call(
        functools.partial(_k3s_kernel, DB=DB, TD=TD, final_norm=final_norm),
        grid=(nf,), in_specs=in_specs, out_specs=out_specs, out_shape=out_shape,
        scratch_shapes=[pltpu.VMEM((R, D), F32), pltpu.VMEM((R, D), BF16), pltpu.VMEM((R, D), F32)],
        compiler_params=pltpu.CompilerParams(dimension_semantics=("arbitrary",), vmem_limit_bytes=VMEM_LIMIT),
        name="channel_mix_sample",
    )(h, ab, cd, p, wout, gffn, wg, wu, cf, wd, gple, wpg, wple, gfin, sffn)


def _prep_w_in(w_in):
    cuts = np.cumsum(SPLIT_SIZES)[:-1].tolist()
    gb, gc, gh, pv, cq, ck, cv, cqi, cwi, cki, dq, dk, dv = jnp.split(w_in, cuts, axis=-1)
    pad = jnp.zeros(w_in.shape[:-1] + (LANES - 64 - IDX_HEADS,), w_in.dtype)
    cols = [gb, gc, gh, pv, cq, cqi, dq, dk, dv, ck, ck, cki, cki, cv, cwi, pad]
    return jnp.concatenate(cols, axis=-1).astype(BF16)


def _block_diag_maps(pool_maps):
    depth, G, c, _ = pool_maps.shape
    out = jnp.zeros((depth, G * c, G * c), pool_maps.dtype)
    for g in range(G):
        out = out.at[:, g * c:(g + 1) * c, g * c:(g + 1) * c].set(pool_maps[:, g])
    return out.astype(BF16)


def _heads_rows(a, NS, G, TD, nh):
    a = a.reshape(TD, NS, G, nh, 64).transpose(1, 3, 2, 0, 4)
    return a.reshape(NS, nh * QP, 64)


def _new_page(a, NS, G, TD, PS):
    c = a.shape[1]
    a = a.reshape(TD, NS, G, c).transpose(1, 2, 3, 0)
    return jnp.pad(a, ((0, 0), (0, 0), (0, 0), (0, PS - TD)))


def kernel(x_prompt, x_sample, state_conv_a, state_pool, state_ffn, cache_c_k, cache_c_v, cache_c_idx,
           cache_d_k, cache_d_v, page_table, p_prompt, p_sample, norm_mix, w_in, conv_a, pool_maps,
           pool_scale, diff_lambda, diff_subln, w_out, norm_ffn, w_gate, w_up, conv_ffn, w_down,
           norm_ple, w_ple_gate, w_ple, norm_final):
    depth = w_in.shape[0]
    B, T, D = x_prompt.shape
    DB, TD, _ = x_sample.shape
    NP = page_table.shape[1]
    PS = cache_c_k.shape[2]
    past = NP * PS
    dff = w_gate.shape[2]
    TS1 = min(512, T)
    TS3 = min(256, T)

    w_in_p = _prep_w_in(w_in)
    pm_bd = _block_diag_maps(pool_maps)
    w_out_b, w_gate_b, w_up_b, w_down_b = (w.astype(BF16) for w in (w_out, w_gate, w_up, w_down))
    w_pg_b, w_ple_b = w_ple_gate.astype(BF16), w_ple.astype(BF16)
    gfin = norm_final.reshape(1, D)
    c_idx_t, c_k_t, c_v_t = (c.transpose(0, 1, 3, 2) for c in (cache_c_idx, cache_c_k, cache_c_v))
    d_k_t, d_v_t = (c.transpose(0, 1, 3, 4, 2) for c in (cache_d_k, cache_d_v))
    assert QP % TD == 0 and DB % (QP // TD) == 0 and TD >= 2
    G = QP // TD
    NS = DB // G

    def row(a, i):
        return a[i].reshape(1, -1)

    h = x_prompt
    st_p = []
    n_sel_p = min(TOPK_MAX, T // 4)
    kv_p = tuple(jnp.zeros((B, depth, c, T), F32) for c in (64, 64, 64, 256, 256))
    for i in range(depth):
        lam_init = 0.8 - 0.6 * math.exp(-0.3 * i)
        res = _k1p(h, row(norm_mix, i), w_in_p[i], conv_a[i], pm_bd[i], row(pool_scale, i), kv_p,
                   TS=TS1, layer=i)
        (mixab, cq, cqi, dq, kkb, iib, dkb, cvT, cwiT, dvT, convst, poolst) = res[:12]
        kv_p = res[12:]
        mixcd = _k2p(cq, cqi, cwiT, dq, kkb, iib, cvT, dkb, dvT, diff_lambda[i], diff_subln[i].reshape(-1, 1),
                     n_sel=n_sel_p, lam_init=lam_init)
        h, ffnst = _k3p(h, mixab, mixcd, p_prompt[i], w_out_b[i], row(norm_ffn, i), w_gate_b[i], w_up_b[i],
                        conv_ffn[i], w_down_b[i], row(norm_ple, i), w_pg_b[i], w_ple_b[i], gfin,
                        TS=TS3, final_norm=(i == depth - 1))
        st_p.append((convst[:, 6:8], poolst[:, 1:16], ffnst[:, 6:8]))
    y_prompt = h
    c_k_p, c_v_p, c_idx_p = (a.transpose(0, 1, 3, 2) for a in kv_p[:3])
    d_k_p, d_v_p = (a.reshape(B, depth, DIFF_HEADS, 64, T).transpose(0, 1, 4, 2, 3) for a in kv_p[3:])

    hs = x_sample.transpose(1, 0, 2).reshape(TD * DB, D)
    st_s = []
    n_sel_s = min(TOPK_MAX, (past + TD) // 4)
    for i in range(depth):
        lam_init = 0.8 - 0.6 * math.exp(-0.3 * i)
        sconv = state_conv_a[i].transpose(1, 0, 2)
        spool = state_pool[i].transpose(1, 0, 2)
        sffn = state_ffn[i].transpose(1, 0, 2)
        (mixab, cq, cqi, dq, dk, dv, skk, sii, svw, u_new, pv_new) = _k1s(
            hs, row(norm_mix, i), w_in_p[i], conv_a[i], pm_bd[i], row(pool_scale, i), sconv, spool,
            DB=DB, TD=TD, pos0=past)
        qi = _heads_rows(cqi, NS, G, TD, IDX_HEADS)
        wi = svw[:, 64:64 + IDX_HEADS].reshape(TD, NS, G, IDX_HEADS).transpose(1, 3, 2, 0)
        wi = jnp.broadcast_to(wi.reshape(NS, IDX_HEADS * QP, 1), (NS, IDX_HEADS * QP, LANES))
        qc = _heads_rows(cq, NS, G, TD, DSA_HEADS)
        qd = dq.reshape(TD, NS, G, 256).transpose(1, 2, 0, 3).reshape(NS, 1, QP, 256)
        qd = jnp.broadcast_to(qd, (NS, DIFF_HEADS, QP, 256)).reshape(NS, DIFF_HEADS * QP, 256)
        kin = _new_page(sii[:, :64], NS, G, TD, PS)
        kn = _new_page(skk[:, :64], NS, G, TD, PS)
        vn = _new_page(svw[:, :64], NS, G, TD, PS)
        dkn = _new_page(dk, NS, G, TD, PS)
        dvn = _new_page(dv, NS, G, TD, PS)
        ycd = _k2s(page_table, qi, wi, qc, qd, kin, kn, vn, dkn, dvn, diff_lambda[i], row(diff_subln, i),
                   c_idx_t, c_k_t, c_v_t, d_k_t, d_v_t, layer=i, TD=TD, n_sel=n_sel_s, lam_init=lam_init)
        mixcd = ycd.reshape(NS, G, TD, 512).transpose(2, 0, 1, 3).reshape(TD * DB, 512)
        hs, g_new = _k3s(hs, mixab, mixcd, p_sample[i].transpose(1, 0, 2).reshape(TD * DB, -1), w_out_b[i],
                         row(norm_ffn, i), w_gate_b[i], w_up_b[i], conv_ffn[i], w_down_b[i], row(norm_ple, i),
                         w_pg_b[i], w_ple_b[i], gfin, sffn, DB=DB, TD=TD, final_norm=(i == depth - 1))

        def bm(a, width):
            return a[:, :width].reshape(TD, DB, width).transpose(1, 0, 2)

        new_conv = u_new[TD - 2:].transpose(1, 0, 2)
        new_pool = jnp.concatenate([state_pool[i], pv_new.transpose(1, 0, 2)], axis=1)[:, -POOL_BUF:]
        st_s.append((new_conv, new_pool, g_new.transpose(1, 0, 2), bm(skk, 64), bm(svw, 64), bm(sii, 64),
                     bm(dk, 256).reshape(DB, TD, DIFF_HEADS, 64), bm(dv, 256).reshape(DB, TD, DIFF_HEADS, 64)))
    y_sample = hs.reshape(TD, DB, D).transpose(1, 0, 2)

    def col(outs, j, axis):
        return jnp.stack([o[j] for o in outs], axis=axis)

    return (y_prompt, y_sample, col(st_p, 0, 0), col(st_s, 0, 0), col(st_p, 1, 0), col(st_s, 1, 0),
            col(st_p, 2, 0), col(st_s, 2, 0), c_k_p, col(st_s, 3, 1), c_v_p, col(st_s, 4, 1),
            c_idx_p, col(st_s, 5, 1), d_k_p, col(st_s, 6, 1), d_v_p, col(st_s, 7, 1))
```

```python
import functools
import math

import numpy as np
import jax
import jax.numpy as jnp
from jax import lax
from jax.experimental import pallas as pl
from jax.experimental.pallas import tpu as pltpu

F32 = jnp.float32
BF16 = jnp.bfloat16
I32 = jnp.int32

EPS = 1e-6
GROUP_W = 256
POOL_WINDOWS = (2, 4, 8, 16)
POOL_BUF = 15
DSA_HEADS = 4
DSA_HD = 64
IDX_HEADS = 8
IDX_HD = 64
IDX_SCALE = IDX_HEADS ** -0.5 * IDX_HD ** -0.5
TOPK_MAX = 256
DIFF_HEADS = 4
DIFF_VD = 64
DIFF_QD = 32
SPLIT_SIZES = (256, 256, 256, 256, 256, 64, 64, 512, 8, 64, 256, 256, 256)

LANES = 128
SUBLANES = 8
VMEM_LIMIT = 56 * 1024 * 1024

C_GB, C_GC, C_GH, C_PV = 0, 256, 512, 768
C_CQ, C_CQI, C_DQ, C_DK, C_DV = 1024, 1280, 1792, 2048, 2304
C_KK, C_II, C_VW = 2560, 2688, 2816
D_IN_P = 2944

KEY_BLK = 256
Q_BLK = 128
V_AUG = 80
LOG2E = math.log2(math.e)
NEG = -1e30
INT_MIN = -2 ** 31

NT_DIMS = (((1,), (1,)), ((), ()))


def _dot(a, b):
    return jnp.dot(a, b, preferred_element_type=F32)


def _dot_nt(a, b):
    return lax.dot_general(a, b, NT_DIMS, preferred_element_type=F32)


def _rms(x, g):
    ms = jnp.mean(x * x, axis=-1, keepdims=True)
    return (x * lax.rsqrt(ms + EPS)) * g


def _sigmoid(x):
    return 1.0 / (1.0 + jnp.exp(-x))


def _float_key(x):
    b = lax.bitcast_convert_type(x, I32)
    return b ^ (lax.shift_right_arithmetic(b, 31) & 0x7FFFFFFF)


def _lambda_full(lam_ref, lam_init):
    lp = lam_ref[...]
    s1 = jnp.sum(lp[0:1] * lp[1:2], axis=-1, keepdims=True)
    s2 = jnp.sum(lp[2:3] * lp[3:4], axis=-1, keepdims=True)
    return jnp.exp(s1) - jnp.exp(s2) + lam_init


def _const_spec(shape):
    nd = len(shape)
    return pl.BlockSpec(shape, lambda *_: (0,) * nd)


def _k1p_kernel(x_ref, g_ref, w_ref, ca_ref, pm_ref, ps_ref, _a0, _a1, _a2, _a3, _a4,
                mixab_ref, cq_ref, cqi_ref, dq_ref,
                kkb_ref, iib_ref, dkb_ref, cvT_ref, cwiT_ref, dvT_ref, convst_ref, poolst_ref,
                ckT_ref, cvTf_ref, ckiT_ref, dkT_ref, dvTf_ref,
                extu_ref, extx_ref, *, TS):
    t = pl.program_id(1)

    @pl.when(t == 0)
    def _():
        extu_ref[0:8, :] = jnp.zeros((8, GROUP_W), F32)
        extx_ref[0:16, :] = jnp.zeros((16, GROUP_W), F32)

    xn = _rms(x_ref[0], g_ref[...])
    z = _dot(xn.astype(BF16), w_ref[...])

    u = z[:, C_GC:C_GC + 256] * z[:, C_GH:C_GH + 256]
    extu_ref[8:8 + TS, :] = u
    p1 = extu_ref[7:7 + TS, :]
    p2 = extu_ref[6:6 + TS, :]
    ca = ca_ref[...]
    conv = p2 * ca[0:1] + p1 * ca[1:2] + u * ca[2:3]
    mixab_ref[0, :, 0:256] = z[:, C_GB:C_GB + 256] * conv
    extu_ref[0:8, :] = u[TS - 8:TS]
    convst_ref[0] = u[TS - 8:TS]

    pv = z[:, C_PV:C_PV + 256]
    extx_ref[16:16 + TS, :] = pv

    def sh(j, c0):
        return extx_ref[16 - j:16 - j + TS, c0:c0 + LANES]

    s2 = sh(0, 0) + sh(1, 0)
    s4 = s2 + sh(2, 0) + sh(3, 0)
    s8 = sh(0, LANES)
    for j in range(1, 8):
        s8 = s8 + sh(j, LANES)
    s16 = s8
    for j in range(8, 16):
        s16 = s16 + sh(j, LANES)
    lo = lax.broadcasted_iota(I32, (TS, LANES), 1) < 64
    posp1 = (t * TS + 1 + lax.broadcasted_iota(I32, (TS, LANES), 0)).astype(F32)
    d0 = jnp.where(lo, s2, s4) / jnp.where(lo, jnp.minimum(posp1, 2.0), jnp.minimum(posp1, 4.0)) - pv[:, 0:LANES]
    d1 = jnp.where(lo, s8, s16) / jnp.where(lo, jnp.minimum(posp1, 8.0), jnp.minimum(posp1, 16.0)) - pv[:, LANES:]
    d = jnp.concatenate([d0, d1], axis=1)
    mixab_ref[0, :, 256:512] = _dot(d.astype(BF16), pm_ref[...]) * ps_ref[...]
    extx_ref[0:16, :] = pv[TS - 16:TS]
    poolst_ref[0] = pv[TS - 16:TS]

    cq_ref[0] = z[:, C_CQ:C_CQ + 256].astype(BF16)
    cqi_ref[0] = z[:, C_CQI:C_CQI + 512].astype(BF16)
    dq_ref[0] = z[:, C_DQ:C_DQ + 256].astype(BF16)
    dk = z[:, C_DK:C_DK + 256]
    dv = z[:, C_DV:C_DV + 256]
    skk = z[:, C_KK:C_KK + LANES]
    sii = z[:, C_II:C_II + LANES]
    svw = z[:, C_VW:C_VW + LANES]
    nb = TS // KEY_BLK
    kkb_ref[0] = skk.astype(BF16).reshape(nb, KEY_BLK, LANES)
    iib_ref[0] = sii.astype(BF16).reshape(nb, KEY_BLK, LANES)
    dkb_ref[0] = dk.astype(BF16).reshape(nb, KEY_BLK, 256)
    svw_t = svw.T
    dv_t = dv.T
    ckT_ref[0, 0] = skk.T[0:64]
    ckiT_ref[0, 0] = sii.T[0:64]
    cvTf_ref[0, 0] = svw_t[0:64]
    dkT_ref[0, 0] = dk.T
    dvTf_ref[0, 0] = dv_t
    cwiT_ref[0] = svw_t[64:72]
    ones_rows = jnp.where(lax.broadcasted_iota(I32, (V_AUG - 64, KEY_BLK), 0) == 0, 1.0, 0.0).astype(BF16)
    for j in range(nb):
        cvT_ref[0, j] = jnp.concatenate(
            [svw_t[0:64, j * KEY_BLK:(j + 1) * KEY_BLK].astype(BF16), ones_rows], axis=0)
        for h in range(DIFF_HEADS):
            dvT_ref[0, j, h] = jnp.concatenate(
                [dv_t[h * 64:(h + 1) * 64, j * KEY_BLK:(j + 1) * KEY_BLK].astype(BF16), ones_rows], axis=0)


def _k1p(x, g, w, ca, pm, ps, kv_out, *, TS, layer):
    B, T, D = x.shape
    nt = T // TS
    nb = TS // KEY_BLK
    nkb = T // KEY_BLK

    def row_spec(c):
        return pl.BlockSpec((1, TS, c), lambda b, t: (b, t, 0))

    def blk_spec(r, c):
        return pl.BlockSpec((1, nb, r, c), lambda b, t: (b, t, 0, 0))

    out_shape = (
        jax.ShapeDtypeStruct((B, T, 512), F32),
        jax.ShapeDtypeStruct((B, T, 256), BF16),
        jax.ShapeDtypeStruct((B, T, 512), BF16),
        jax.ShapeDtypeStruct((B, T, 256), BF16),
        jax.ShapeDtypeStruct((B, nkb, KEY_BLK, LANES), BF16),
        jax.ShapeDtypeStruct((B, nkb, KEY_BLK, LANES), BF16),
        jax.ShapeDtypeStruct((B, nkb, KEY_BLK, 256), BF16),
        jax.ShapeDtypeStruct((B, nkb, V_AUG, KEY_BLK), BF16),
        jax.ShapeDtypeStruct((B, 8, T), F32),
        jax.ShapeDtypeStruct((B, nkb, DIFF_HEADS, V_AUG, KEY_BLK), BF16),
        jax.ShapeDtypeStruct((B, 8, 256), F32),
        jax.ShapeDtypeStruct((B, 16, 256), F32),
    ) + tuple(jax.ShapeDtypeStruct(a.shape, a.dtype) for a in kv_out)

    def kv_spec(a):
        return pl.BlockSpec((1, 1, a.shape[2], TS), lambda b, t: (b, layer, 0, t))

    out_specs = (
        row_spec(512), row_spec(256), row_spec(512), row_spec(256),
        blk_spec(KEY_BLK, LANES), blk_spec(KEY_BLK, LANES), blk_spec(KEY_BLK, 256),
        blk_spec(V_AUG, KEY_BLK),
        pl.BlockSpec((1, 8, TS), lambda b, t: (b, 0, t)),
        pl.BlockSpec((1, nb, DIFF_HEADS, V_AUG, KEY_BLK), lambda b, t: (b, t, 0, 0, 0)),
        pl.BlockSpec((1, 8, 256), lambda b, t: (b, 0, 0)),
        pl.BlockSpec((1, 16, 256), lambda b, t: (b, 0, 0)),
    ) + tuple(kv_spec(a) for a in kv_out)
    in_specs = [
        row_spec(D), _const_spec(g.shape), _const_spec(w.shape), _const_spec(ca.shape),
        _const_spec(pm.shape), _const_spec(ps.shape),
    ] + [pl.BlockSpec(memory_space=pl.ANY)] * len(kv_out)
    n_in, n_out = 6, len(out_shape) - len(kv_out)
    return pl.pallas_call(
        functools.partial(_k1p_kernel, TS=TS),
        grid=(B, nt), in_specs=in_specs, out_specs=out_specs, out_shape=out_shape,
        input_output_aliases={n_in + k: n_out + k for k in range(len(kv_out))},
        scratch_shapes=[pltpu.VMEM((TS + 8, GROUP_W), F32), pltpu.VMEM((TS + 16, GROUP_W), F32)],
        compiler_params=pltpu.CompilerParams(dimension_semantics=("arbitrary", "arbitrary"),
                                             vmem_limit_bytes=VMEM_LIMIT),
        name="mixer_in_prompt",
    )(x, g, w, ca, pm, ps, *kv_out)


def _k2p_kernel(cq_ref, cqi_ref, cwiT_ref, dq_ref, kkb_ref, iib_ref, cvT_ref, dkb_ref, dvT_ref,
                lam_ref, subln_ref, out_ref, key_ref, lgc_ref, lgd_ref, accc_ref, accd_ref, mxc_ref, thr_ref,
                *, n_sel, lam_init):
    qb = pl.program_id(1)
    q0 = qb * Q_BLK
    nkb = (q0 + Q_BLK + KEY_BLK - 1) // KEY_BLK
    qpos = q0 + lax.broadcasted_iota(I32, (KEY_BLK, Q_BLK), 1)
    krow = lax.broadcasted_iota(I32, (KEY_BLK, Q_BLK), 0)
    lane_q = lax.broadcasted_iota(I32, (Q_BLK, LANES), 1)
    lo_half = lane_q < 64
    NG = KEY_BLK // SUBLANES

    def head_pair(tile):
        zero = jnp.zeros_like(tile)
        return jnp.concatenate([jnp.where(lo_half, tile, zero), jnp.where(lo_half, zero, tile)], axis=0)

    cqi = cqi_ref[0]
    wT = cwiT_ref[0]
    qi = [head_pair(cqi[:, j * LANES:(j + 1) * LANES]) for j in range(IDX_HEADS // 2)]
    cq = cq_ref[0] * (DSA_HD ** -0.5)
    qc = [head_pair(cq[:, j * LANES:(j + 1) * LANES]) for j in range(DSA_HEADS // 2)]
    dq = dq_ref[0]
    lane256 = lax.broadcasted_iota(I32, (Q_BLK, 256), 1)

    def map_rows(h, mm):
        c0 = h * 64 + mm * DIFF_QD
        return jnp.where(jnp.logical_and(lane256 >= c0, lane256 < c0 + DIFF_QD), dq, jnp.zeros_like(dq))

    qd = [jnp.concatenate([map_rows(h, 0), map_rows(h, 1)], axis=0) for h in range(DIFF_HEADS)]
    c2 = (DIFF_QD ** -0.5) * LOG2E

    def pass_a(kb, mx, masked):
        ki = iib_ref[0, kb]
        acc = jnp.zeros((KEY_BLK, Q_BLK), F32)
        for j in range(IDX_HEADS // 2):
            s = _dot_nt(ki, qi[j])
            acc = acc + jnp.maximum(s[:, :Q_BLK], 0.0) * wT[2 * j:2 * j + 1]
            acc = acc + jnp.maximum(s[:, Q_BLK:], 0.0) * wT[2 * j + 1:2 * j + 2]
        key = _float_key(acc * IDX_SCALE + 0.0)
        if masked:
            causal = (kb * KEY_BLK + krow) <= qpos
            key = jnp.where(causal, key, INT_MIN)
        key_ref[kb] = key
        kk = kkb_ref[0, kb]
        for j in range(DSA_HEADS // 2):
            lgc_ref[j, kb] = _dot_nt(kk, qc[j])
        kd = dkb_ref[0, kb]
        new = list(mx)
        for h in range(DIFF_HEADS):
            s = _dot_nt(kd, qd[h]) * c2
            if masked:
                s = jnp.where(jnp.concatenate([causal, causal], axis=1), s, NEG)
            lgd_ref[h, kb] = s
            new[h] = jnp.maximum(new[h], jnp.max(s.reshape(NG, SUBLANES, 2 * Q_BLK), axis=0))
        return tuple(new)

    n_open = nkb - 1
    mx = lax.fori_loop(0, n_open // 2, lambda i, m: pass_a(2 * i + 1, pass_a(2 * i, m, False), False),
                       tuple(jnp.full((SUBLANES, 2 * Q_BLK), NEG, F32) for _ in range(DIFF_HEADS)))
    mx = lax.cond(n_open % 2 == 1, lambda m: pass_a(n_open - 1, m, False), lambda m: m, mx)
    mx = pass_a(nkb - 1, mx, True)

    @pl.when(nkb % 2 == 1)
    def _():
        key_ref[nkb] = jnp.full((KEY_BLK, Q_BLK), INT_MIN, I32)

    def count(pred):
        def body(i, acc):
            for u in range(2):
                kb = 2 * i + u
                c = jnp.where(pred(kb, key_ref[kb]), 1, 0)
                acc = acc + jnp.sum(c.reshape(NG, SUBLANES, Q_BLK), axis=0)
            return acc
        acc = lax.fori_loop(0, (nkb + 1) // 2, body, jnp.zeros((SUBLANES, Q_BLK), I32))
        return jnp.sum(acc, axis=0, keepdims=True)

    def search(npairs):
        def step(it, base):
            cand = base + lax.shift_left(jnp.int32(1), 31 - it)
            acc = jnp.zeros((SUBLANES, Q_BLK), I32)
            for kb in range(2 * npairs):
                acc = acc + jnp.sum(jnp.where(key_ref[kb] >= cand, 1, 0).reshape(NG, SUBLANES, Q_BLK), axis=0)
            cnt = jnp.sum(acc, axis=0, keepdims=True)
            return jnp.where(cnt >= n_sel, cand, base)
        return lax.fori_loop(0, 32, step, jnp.full((1, Q_BLK), INT_MIN, I32))

    thr_ref[...] = jnp.full((1, Q_BLK), INT_MIN, I32)
    for npairs in range(1, key_ref.shape[0] // 2 + 1):
        @pl.when(jnp.logical_and(q0 + Q_BLK > n_sel, (nkb + 1) // 2 == npairs))
        def _(npairs=npairs):
            thr_ref[...] = search(npairs)

    thr = thr_ref[...]

    cnt_gt = count(lambda kb, k: k > thr)
    cnt_eq = count(lambda kb, k: k == thr)
    need = n_sel - cnt_gt
    excess = jnp.logical_and(cnt_eq > need, thr > INT_MIN)
    any_excess = jnp.max(jnp.where(excess, 1, 0))

    @pl.when(any_excess > 0)
    def _():
        def tie_body(it, jp):
            c = jp + lax.shift_left(jnp.int32(1), 10 - it)
            f = count(lambda kb, k: jnp.logical_and(k == thr, (kb * KEY_BLK + krow) < c))
            return jnp.where(f < need, c, jp)
        jp = lax.fori_loop(0, 11, tie_body, jnp.zeros((1, Q_BLK), I32))

        def demote(kb, carry):
            k = key_ref[kb]
            drop = jnp.logical_and(jnp.logical_and(k == thr, (kb * KEY_BLK + krow) > jp), excess)
            key_ref[kb] = jnp.where(drop, thr - 1, k)
            return carry
        lax.fori_loop(0, nkb, demote, 0)

    thr_sel = jnp.maximum(thr, INT_MIN + 1)

    def masked_c(j, kb):
        sel = key_ref[kb] >= thr_sel
        lg = lgc_ref[j, kb]
        return jnp.concatenate([jnp.where(sel, lg[:, :Q_BLK], NEG), jnp.where(sel, lg[:, Q_BLK:], NEG)], axis=1)

    md = [jnp.max(m, axis=0, keepdims=True) for m in mx]
    accd_ref[...] = jnp.zeros(accd_ref.shape, F32)

    mxc_ref[...] = jnp.full(mxc_ref.shape, NEG, F32)

    def over_blocks(step):
        def pair(i, carry):
            step((2 * i, 2 * i + 1))
            return carry
        lax.fori_loop(0, nkb // 2, pair, 0)

        @pl.when(nkb % 2 == 1)
        def _():
            step((nkb - 1,))

    def pass_b(kbs):
        for j in range(DSA_HEADS // 2):
            m = mxc_ref[j]
            for kb in kbs:
                m = jnp.maximum(m, jnp.max(masked_c(j, kb).reshape(NG, SUBLANES, 2 * Q_BLK), axis=0))
            mxc_ref[j] = m
        for h in range(DIFF_HEADS):
            p = jnp.concatenate([jnp.exp2(lgd_ref[h, kb] - md[h]).astype(BF16) for kb in kbs], axis=0)
            vT = jnp.concatenate([dvT_ref[0, kb, h] for kb in kbs], axis=1)
            accd_ref[h] += _dot(vT, p)

    over_blocks(pass_b)
    mc = [jnp.max(mxc_ref[j], axis=0, keepdims=True) for j in range(DSA_HEADS // 2)]

    accc_ref[...] = jnp.zeros(accc_ref.shape, F32)

    def pass_c(kbs):
        vT = jnp.concatenate([cvT_ref[0, kb] for kb in kbs], axis=1)
        for j in range(DSA_HEADS // 2):
            p = jnp.concatenate([jnp.exp(masked_c(j, kb) - mc[j]).astype(BF16) for kb in kbs], axis=0)
            accc_ref[j] += _dot(vT, p)

    over_blocks(pass_c)
    outs = []
    for h in range(DSA_HEADS):
        a = accc_ref[h // 2, :, (h % 2) * Q_BLK:(h % 2 + 1) * Q_BLK]
        outs.append(a[0:DSA_HD] / a[DSA_HD:DSA_HD + 1])
    out_ref[0, :, 0:256] = jnp.concatenate(outs, axis=0).T

    lam = _lambda_full(lam_ref, lam_init)
    subln = subln_ref[...]
    outs = []
    for h in range(DIFF_HEADS):
        a1 = accd_ref[h, :, 0:Q_BLK]
        a2 = accd_ref[h, :, Q_BLK:2 * Q_BLK]
        o = a1[0:DIFF_VD] / a1[DIFF_VD:DIFF_VD + 1] - lam * (a2[0:DIFF_VD] / a2[DIFF_VD:DIFF_VD + 1])
        ms = jnp.mean(o * o, axis=0, keepdims=True)
        outs.append((o * lax.rsqrt(ms + EPS)) * subln * (1.0 - lam_init))
    out_ref[0, :, 256:512] = jnp.concatenate(outs, axis=0).T


def _k2p(cq, cqi, cwiT, dq, kkb, iib, cvT, dkb, dvT, lam, subln, *, n_sel, lam_init):
    B, T, _ = cq.shape
    nq = T // Q_BLK
    nkb = T // KEY_BLK

    def q_spec(c):
        return pl.BlockSpec((1, Q_BLK, c), lambda b, q: (b, q, 0))

    def seq_spec(r, c):
        return pl.BlockSpec((1, nkb, r, c), lambda b, q: (b, 0, 0, 0))

    in_specs = [
        q_spec(256), q_spec(512), pl.BlockSpec((1, 8, Q_BLK), lambda b, q: (b, 0, q)), q_spec(256),
        seq_spec(KEY_BLK, LANES), seq_spec(KEY_BLK, LANES), seq_spec(V_AUG, KEY_BLK),
        seq_spec(KEY_BLK, 256),
        pl.BlockSpec((1, nkb, DIFF_HEADS, V_AUG, KEY_BLK), lambda b, q: (b, 0, 0, 0, 0)),
        _const_spec(lam.shape), _const_spec(subln.shape),
    ]
    return pl.pallas_call(
        functools.partial(_k2p_kernel, n_sel=n_sel, lam_init=lam_init),
        grid=(B, nq), in_specs=in_specs, out_specs=q_spec(512),
        out_shape=jax.ShapeDtypeStruct((B, T, 512), F32),
        scratch_shapes=[pltpu.VMEM((nkb + 1, KEY_BLK, Q_BLK), I32),
                        pltpu.VMEM((DSA_HEADS // 2, nkb, KEY_BLK, 2 * Q_BLK), F32),
                        pltpu.VMEM((DIFF_HEADS, nkb, KEY_BLK, 2 * Q_BLK), F32),
                        pltpu.VMEM((DSA_HEADS // 2, V_AUG, 2 * Q_BLK), F32),
                        pltpu.VMEM((DIFF_HEADS, V_AUG, 2 * Q_BLK), F32),
                        pltpu.VMEM((DSA_HEADS // 2, SUBLANES, 2 * Q_BLK), F32),
                        pltpu.VMEM((1, Q_BLK), I32)],
        compiler_params=pltpu.CompilerParams(dimension_semantics=("arbitrary", "arbitrary"),
                                             vmem_limit_bytes=VMEM_LIMIT),
        name="attention_prompt",
    )(cq, cqi, cwiT, dq, kkb, iib, cvT, dkb, dvT, lam, subln)


def _ple_tail(h2, p_bf, gple, wpg, wple, gfin, final_norm):
    gate = _sigmoid(_dot(_rms(h2, gple).astype(BF16), wpg))
    h3 = h2 + _dot(p_bf, wple) * gate
    if final_norm:
        h3 = _rms(h3, gfin)
    return h3


def _k3p_kernel(h_ref, ab_ref, cd_ref, p_ref, wout_ref, gffn_ref, wg_ref, wu_ref, cf_ref, wd_ref,
                gple_ref, wpg_ref, wple_ref, gfin_ref, hout_ref, ffnst_ref, extg_ref, *, TS, final_norm):
    t = pl.program_id(1)

    @pl.when(t == 0)
    def _():
        extg_ref[0:8, :] = jnp.zeros((8, extg_ref.shape[1]), F32)

    mix = jnp.concatenate([ab_ref[0], cd_ref[0]], axis=1).astype(BF16)
    h1 = h_ref[0] + _dot(mix, wout_ref[...])
    xn2 = _rms(h1, gffn_ref[...]).astype(BF16)
    g = _dot(xn2, wg_ref[...])
    extg_ref[8:8 + TS, :] = g
    p1 = extg_ref[7:7 + TS, :]
    p2 = extg_ref[6:6 + TS, :]
    cf = cf_ref[...]
    gc = p2 * cf[0:1] + p1 * cf[1:2] + g * cf[2:3]
    extg_ref[0:8, :] = g[TS - 8:TS]
    ffnst_ref[0] = g[TS - 8:TS]
    act = (gc * _sigmoid(gc)) * _dot(xn2, wu_ref[...])
    h2 = h1 + _dot(act.astype(BF16), wd_ref[...])
    hout_ref[0] = _ple_tail(h2, p_ref[0].astype(BF16), gple_ref[...], wpg_ref[...], wple_ref[...],
                            gfin_ref[...], final_norm)


def _k3p(h, ab, cd, p, wout, gffn, wg, wu, cf, wd, gple, wpg, wple, gfin, *, TS, final_norm):
    B, T, D = h.shape
    dff = wg.shape[1]
    nt = T // TS

    def row_spec(c):
        return pl.BlockSpec((1, TS, c), lambda b, t: (b, t, 0))

    def w_spec(a):
        nd = a.ndim
        return pl.BlockSpec(a.shape, lambda *_: (0,) * nd, pipeline_mode=pl.Buffered(1))

    in_specs = [row_spec(D), row_spec(512), row_spec(512), row_spec(p.shape[2]),
                w_spec(wout), w_spec(gffn), w_spec(wg), w_spec(wu), w_spec(cf), w_spec(wd),
                w_spec(gple), w_spec(wpg), w_spec(wple), w_spec(gfin)]
    out_specs = (row_spec(D), pl.BlockSpec((1, 8, dff), lambda b, t: (b, 0, 0)))
    out_shape = (jax.ShapeDtypeStruct((B, T, D), F32), jax.ShapeDtypeStruct((B, 8, dff), F32))
    return pl.pallas_call(
        functools.partial(_k3p_kernel, TS=TS, final_norm=final_norm),
        grid=(B, nt), in_specs=in_specs, out_specs=out_specs, out_shape=out_shape,
        scratch_shapes=[pltpu.VMEM((TS + 8, dff), F32)],
        compiler_params=pltpu.CompilerParams(dimension_semantics=("arbitrary", "arbitrary"),
                                             vmem_limit_bytes=VMEM_LIMIT),
        name="channel_mix_prompt",
    )(h, ab, cd, p, wout, gffn, wg, wu, cf, wd, gple, wpg, wple, gfin)


def _k1s_kernel(x_ref, g_ref, w_ref, ca_ref, pm_ref, ps_ref, sconv_ref, spool_ref,
                mixab_ref, cq_ref, cqi_ref, dq_ref, dk_ref, dv_ref, skk_ref, sii_ref, svw_ref,
                u_ref, pv_ref, *, DB, TD, pos0):
    xn = _rms(x_ref[...], g_ref[...])
    z = _dot(xn.astype(BF16), w_ref[...])

    def slab(t, c0, c1):
        return z[t * DB:(t + 1) * DB, c0:c1]

    ca = ca_ref[...]
    us = [slab(t, C_GC, C_GC + 256) * slab(t, C_GH, C_GH + 256) for t in range(TD)]
    extu = [sconv_ref[0], sconv_ref[1]] + us
    pvs = [slab(t, C_PV, C_PV + 256) for t in range(TD)]
    extx = [spool_ref[j] for j in range(POOL_BUF)] + pvs
    lo = lax.broadcasted_iota(I32, (DB, LANES), 1) < 64
    pm = pm_ref[...]
    ps = ps_ref[...]
    for t in range(TD):
        conv = extu[t] * ca[0:1] + extu[t + 1] * ca[1:2] + extu[t + 2] * ca[2:3]
        mixab_ref[t * DB:(t + 1) * DB, 0:256] = slab(t, C_GB, C_GB + 256) * conv
        u_ref[t] = us[t]
        pv_ref[t] = pvs[t]
        e = POOL_BUF + t

        def win(n, c0):
            s = extx[e][:, c0:c0 + LANES]
            for j in range(1, n):
                s = s + extx[e - j][:, c0:c0 + LANES]
            return s

        cnt = [float(min(w, pos0 + t + 1)) for w in POOL_WINDOWS]
        d0 = jnp.where(lo, win(2, 0) / cnt[0], win(4, 0) / cnt[1]) - pvs[t][:, 0:LANES]
        d1 = jnp.where(lo, win(8, LANES) / cnt[2], win(16, LANES) / cnt[3]) - pvs[t][:, LANES:]
        d = jnp.concatenate([d0, d1], axis=1)
        mixab_ref[t * DB:(t + 1) * DB, 256:512] = _dot(d.astype(BF16), pm) * ps

    cq_ref[...] = z[:, C_CQ:C_CQ + 256].astype(BF16)
    cqi_ref[...] = z[:, C_CQI:C_CQI + 512].astype(BF16)
    dq_ref[...] = z[:, C_DQ:C_DQ + 256].astype(BF16)
    dk_ref[...] = z[:, C_DK:C_DK + 256]
    dv_ref[...] = z[:, C_DV:C_DV + 256]
    skk_ref[...] = z[:, C_KK:C_KK + LANES]
    sii_ref[...] = z[:, C_II:C_II + LANES]
    svw_ref[...] = z[:, C_VW:C_VW + LANES]


def _k1s(x, g, w, ca, pm, ps, sconv, spool, *, DB, TD, pos0):
    R = x.shape[0]
    out_shape = (
        jax.ShapeDtypeStruct((R, 512), F32), jax.ShapeDtypeStruct((R, 256), BF16),
        jax.ShapeDtypeStruct((R, 512), BF16), jax.ShapeDtypeStruct((R, 256), BF16),
        jax.ShapeDtypeStruct((R, 256), F32), jax.ShapeDtypeStruct((R, 256), F32),
        jax.ShapeDtypeStruct((R, LANES), F32), jax.ShapeDtypeStruct((R, LANES), F32),
        jax.ShapeDtypeStruct((R, LANES), F32),
        jax.ShapeDtypeStruct((TD, DB, 256), F32), jax.ShapeDtypeStruct((TD, DB, 256), F32),
    )
    args = (x, g, w, ca, pm, ps, sconv, spool)
    return pl.pallas_call(
        functools.partial(_k1s_kernel, DB=DB, TD=TD, pos0=pos0),
        grid=(1,), in_specs=[_const_spec(a.shape) for a in args],
        out_specs=tuple(_const_spec(s.shape) for s in out_shape), out_shape=out_shape,
        compiler_params=pltpu.CompilerParams(dimension_semantics=("arbitrary",), vmem_limit_bytes=VMEM_LIMIT),
        name="mixer_in_sample",
    )(*args)


QP = 8


def _wrap32(v):
    return ((v + 2 ** 31) % 2 ** 32) - 2 ** 31


def _k2s_kernel(pt_ref, qi_ref, wi_ref, qc_ref, qd_ref, kin_ref, kn_ref, vn_ref, dkn_ref, dvn_ref,
                lam_ref, subln_ref, cidx_hbm, ck_hbm, cv_hbm, dk_hbm, dv_hbm,
                out_ref, key_ref, clog_ref, dlog_ref, idx_refs, k_refs, v_refs, dk_refs, dv_refs, sem,
                *, NP, PS, TD, n_sel, lam_init, layer):
    G = QP // TD
    n = G * NP
    SC = (NP + 1) * PS

    i = pl.program_id(0)
    slot = lax.rem(i, 2)
    hbms = (cidx_hbm, ck_hbm, cv_hbm, dk_hbm, dv_hbm)
    bufs = (idx_refs, k_refs, v_refs, dk_refs, dv_refs)

    def page_copy(hbm, buf, page, sl, gp):
        return pltpu.make_async_copy(hbm.at[page, layer], buf.at[sl, gp], sem.at[sl])

    def issue(step, sl):
        def body(gp, carry):
            page = pt_ref[G * step + gp // NP, gp % NP]
            for hbm, buf in zip(hbms, bufs):
                page_copy(hbm, buf, page, sl, gp).start()
            return carry
        lax.fori_loop(0, n, body, 0)

    @pl.when(i == 0)
    def _():
        issue(0, 0)

    @pl.when(i + 1 < pl.num_programs(0))
    def _():
        issue(i + 1, 1 - slot)

    def wait_body(gp, carry):
        for hbm, buf in zip(hbms, bufs):
            page_copy(hbm, buf, 0, slot, gp).wait()
        return carry

    lax.fori_loop(0, n, wait_body, 0)

    def own_half(x):
        elem0 = (lax.broadcasted_iota(I32, (x.shape[0], PS), 0) % QP) < TD
        return jnp.where(elem0, x[:, :PS], x[:, PS:])

    def split_half(x):
        elem0 = (lax.broadcasted_iota(I32, x.shape, 0) % QP) < TD
        zero = jnp.zeros_like(x)
        return jnp.concatenate([jnp.where(elem0, x, zero), jnp.where(elem0, zero, x)], axis=1)

    def pages(refs, new_ref, p, rows):
        tiles = [(refs[slot, g * NP + p] if p < NP else new_ref[0, g]).reshape(rows, PS).astype(BF16)
                 for g in range(G)]
        return jnp.concatenate(tiles, axis=1)

    qi = qi_ref[0]
    wi = wi_ref[0]
    col = lax.broadcasted_iota(I32, (QP, PS), 1)
    qt = lax.broadcasted_iota(I32, (QP, PS), 0) % TD
    new_ok = jnp.logical_and(col < TD, col <= qt)
    for p in range(NP + 1):
        s = own_half(_dot(qi, pages(idx_refs, kin_ref, p, IDX_HD)))
        r = (jnp.maximum(s, 0.0) * wi).reshape(IDX_HEADS, QP, PS)
        score = jnp.sum(r, axis=0) * IDX_SCALE + 0.0
        key = _float_key(score)
        if p == NP:
            key = jnp.where(new_ok, key, INT_MIN)
        key_ref[:, p * PS:(p + 1) * PS] = key

    qc = qc_ref[0] * (DSA_HD ** -0.5)
    for p in range(NP + 1):
        clog_ref[:, p * PS:(p + 1) * PS] = own_half(_dot(qc, pages(k_refs, kn_ref, p, DSA_HD)))

    qd0 = qd_ref[0]
    rhead = lax.broadcasted_iota(I32, qd0.shape, 0) // QP
    lane = lax.broadcasted_iota(I32, qd0.shape, 1)
    zero = jnp.zeros_like(qd0)
    qd = jnp.concatenate(
        [jnp.where(jnp.logical_and(lane >= rhead * 64 + mm * DIFF_QD, lane < rhead * 64 + (mm + 1) * DIFF_QD),
                   qd0, zero) for mm in range(2)], axis=0)
    HD = DIFF_HEADS * DIFF_VD
    nrow = 2 * DIFF_HEADS * QP
    for p in range(NP + 1):
        s = own_half(_dot(qd, pages(dk_refs, dkn_ref, p, HD))) * (DIFF_QD ** -0.5)
        if p == NP:
            ncol = lax.broadcasted_iota(I32, (nrow, PS), 1)
            nqt = lax.broadcasted_iota(I32, (nrow, PS), 0) % TD
            s = jnp.where(jnp.logical_and(ncol < TD, ncol <= nqt), s, NEG)
        dlog_ref[:, p * PS:(p + 1) * PS] = s
    lgd = dlog_ref[...]
    md = jnp.max(lgd, axis=1, keepdims=True)
    ped = jnp.exp(lgd - md)
    ld = jnp.sum(ped, axis=1, keepdims=True)
    dlog_ref[...] = ped
    accd = jnp.zeros((nrow, HD), F32)
    for p in range(NP + 1):
        pp = split_half(dlog_ref[:, p * PS:(p + 1) * PS]).astype(BF16)
        accd = accd + _dot_nt(pp, pages(dv_refs, dvn_ref, p, HD))
    od = accd / ld
    half = DIFF_HEADS * QP
    lam = _lambda_full(lam_ref, lam_init)
    subln = subln_ref[...]
    for h in range(DIFF_HEADS):
        o = (od[h * QP:(h + 1) * QP, h * 64:(h + 1) * 64]
             - lam * od[half + h * QP:half + (h + 1) * QP, h * 64:(h + 1) * 64])
        ms = jnp.mean(o * o, axis=-1, keepdims=True)
        out_ref[0, :, 256 + h * 64:256 + (h + 1) * 64] = (o * lax.rsqrt(ms + EPS)) * subln * (1.0 - lam_init)

    kcol = lax.broadcasted_iota(I32, (QP, SC), 1)

    def count(pred):
        c = jnp.where(pred(key_ref[...]), 1.0, 0.0)
        return jnp.sum(c, axis=1, keepdims=True)

    thr = jnp.full((QP, 1), INT_MIN, I32)
    for sh, nbits in [(30, 2)] + [(27 - 3 * s, 3) for s in range(10)]:
        passed = jnp.zeros((QP, 1), I32)
        for j in range(1, 2 ** nbits):
            cnt = count(lambda k, c=thr + _wrap32(j << sh): k >= c)
            passed = passed + jnp.where(cnt >= n_sel, 1, 0)
        thr = thr + lax.shift_left(passed, sh)
    cnt_gt = count(lambda k: k > thr)
    cnt_eq = count(lambda k: k == thr)
    need = n_sel - cnt_gt
    excess = jnp.logical_and(cnt_eq > need, thr > INT_MIN)
    any_excess = jnp.max(jnp.where(excess, 1, 0))

    @pl.when(any_excess > 0)
    def _():
        def tie_body(it, jp):
            c = jp + lax.shift_left(jnp.int32(1), 12 - it)
            f = count(lambda k: jnp.logical_and(k == thr, kcol < c))
            return jnp.where(f < need, c, jp)
        jp = lax.fori_loop(0, 13, tie_body, jnp.zeros((QP, 1), I32))
        k = key_ref[...]
        drop = jnp.logical_and(jnp.logical_and(k == thr, kcol > jp), excess)
        key_ref[...] = jnp.where(drop, thr - 1, k)

    thr_sel = jnp.maximum(thr, INT_MIN + 1)
    sel = key_ref[...] >= thr_sel

    lg = jnp.where(sel[None], clog_ref[...].reshape(DSA_HEADS, QP, SC), NEG)
    lg = lg.reshape(DSA_HEADS * QP, SC)
    m = jnp.max(lg, axis=1, keepdims=True)
    pe = jnp.exp(lg - m)
    l = jnp.sum(pe, axis=1, keepdims=True)
    clog_ref[...] = pe
    acc = jnp.zeros((DSA_HEADS * QP, DSA_HD), F32)
    for p in range(NP + 1):
        pp = split_half(clog_ref[:, p * PS:(p + 1) * PS]).astype(BF16)
        acc = acc + _dot_nt(pp, pages(v_refs, vn_ref, p, DSA_HD))
    oc = acc / l
    for h in range(DSA_HEADS):
        out_ref[0, :, h * 64:(h + 1) * 64] = oc[h * QP:(h + 1) * QP]


def _k2s(page_table, qi, wi, qc, qd, kin, kn, vn, dkn, dvn, lam, subln,
         c_idx, c_k, c_v, d_k, d_v, *, layer, TD, n_sel, lam_init):
    DB, NP = page_table.shape
    PS = c_k.shape[3]
    G = QP // TD
    NS = DB // G

    def b_spec(a):
        return pl.BlockSpec((1,) + a.shape[1:], lambda b, pt: (b,) + (0,) * (a.ndim - 1))

    def c_spec(a):
        nd = a.ndim
        return pl.BlockSpec(a.shape, lambda b, pt: (0,) * nd)

    in_specs = [b_spec(qi), b_spec(wi), b_spec(qc), b_spec(qd), b_spec(kin), b_spec(kn), b_spec(vn),
                b_spec(dkn), b_spec(dvn), c_spec(lam), c_spec(subln)]
    in_specs += [pl.BlockSpec(memory_space=pl.ANY)] * 5
    SC = (NP + 1) * PS
    n = G * NP
    grid_spec = pltpu.PrefetchScalarGridSpec(
        num_scalar_prefetch=1, grid=(NS,), in_specs=in_specs,
        out_specs=pl.BlockSpec((1, QP, 512), lambda b, pt: (b, 0, 0)),
        scratch_shapes=[pltpu.VMEM((QP, SC), I32), pltpu.VMEM((DSA_HEADS * QP, SC), F32),
                        pltpu.VMEM((2 * DIFF_HEADS * QP, SC), F32)]
        + [pltpu.VMEM((2, n, 64, PS), F32)] * 3 + [pltpu.VMEM((2, n, DIFF_HEADS, 64, PS), F32)] * 2
        + [pltpu.SemaphoreType.DMA((2,))],
    )
    args = [page_table, qi, wi, qc, qd, kin, kn, vn, dkn, dvn, lam, subln, c_idx, c_k, c_v, d_k, d_v]
    return pl.pallas_call(
        functools.partial(_k2s_kernel, NP=NP, PS=PS, TD=TD, n_sel=n_sel, lam_init=lam_init, layer=layer),
        grid_spec=grid_spec, out_shape=jax.ShapeDtypeStruct((NS, QP, 512), F32),
        compiler_params=pltpu.CompilerParams(dimension_semantics=("arbitrary",), vmem_limit_bytes=VMEM_LIMIT),
        name="attention_sample",
    )(*args)


FF_TILE = 256


def _k3s_kernel(h_ref, ab_ref, cd_ref, p_ref, wout_ref, gffn_ref, wg_ref, wu_ref, cf_ref, wd_ref,
                gple_ref, wpg_ref, wple_ref, gfin_ref, sffn_ref, hout_ref, gout_ref,
                h1_ref, xn2_ref, acc_ref, *, DB, TD, final_norm):
    f = pl.program_id(0)

    @pl.when(f == 0)
    def _():
        mix = jnp.concatenate([ab_ref[...], cd_ref[...]], axis=1).astype(BF16)
        h1 = h_ref[...] + _dot(mix, wout_ref[...])
        h1_ref[...] = h1
        xn2_ref[...] = _rms(h1, gffn_ref[...]).astype(BF16)
        acc_ref[...] = jnp.zeros(acc_ref.shape, F32)

    xn2 = xn2_ref[...]
    g = _dot(xn2, wg_ref[...])
    cf = cf_ref[...]
    gs = [sffn_ref[0], sffn_ref[1]] + [g[t * DB:(t + 1) * DB] for t in range(TD)]
    gc = jnp.concatenate([gs[t] * cf[0:1] + gs[t + 1] * cf[1:2] + gs[t + 2] * cf[2:3] for t in range(TD)], axis=0)
    act = (gc * _sigmoid(gc)) * _dot(xn2, wu_ref[...])
    acc_ref[...] += _dot(act.astype(BF16), wd_ref[...])
    gout_ref[0] = gs[TD]
    gout_ref[1] = gs[TD + 1]

    @pl.when(f == pl.num_programs(0) - 1)
    def _():
        h2 = h1_ref[...] + acc_ref[...]
        hout_ref[...] = _ple_tail(h2, p_ref[...].astype(BF16), gple_ref[...], wpg_ref[...], wple_ref[...],
                                  gfin_ref[...], final_norm)


def _k3s(h, ab, cd, p, wout, gffn, wg, wu, cf, wd, gple, wpg, wple, gfin, sffn, *, DB, TD, final_norm):
    R, D = h.shape
    dff = wg.shape[1]
    nf = dff // FF_TILE
    cs = _const_spec
    in_specs = [cs(h.shape), cs(ab.shape), cs(cd.shape), cs(p.shape), cs(wout.shape), cs(gffn.shape),
                pl.BlockSpec((D, FF_TILE), lambda f: (0, f)), pl.BlockSpec((D, FF_TILE), lambda f: (0, f)),
                pl.BlockSpec((3, FF_TILE), lambda f: (0, f)), pl.BlockSpec((FF_TILE, D), lambda f: (f, 0)),
                cs(gple.shape), cs(wpg.shape), cs(wple.shape), cs(gfin.shape),
                pl.BlockSpec((2, DB, FF_TILE), lambda f: (0, 0, f))]
    out_specs = (cs((R, D)), pl.BlockSpec((2, DB, FF_TILE), lambda f: (0, 0, f)))
    out_shape = (jax.ShapeDtypeStruct((R, D), F32), jax.ShapeDtypeStruct((2, DB, dff), F32))
    return pl.pallas_call(
        functools.partial(_k3s_kernel, DB=DB, TD=TD, final_norm=final_norm),
        grid=(nf,), in_specs=in_specs, out_specs=out_specs, out_shape=out_shape,
        scratch_shapes=[pltpu.VMEM((R, D), F32), pltpu.VMEM((R, D), BF16), pltpu.VMEM((R, D), F32)],
        compiler_params=pltpu.CompilerParams(dimension_semantics=("arbitrary",), vmem_limit_bytes=VMEM_LIMIT),
        name="channel_mix_sample",
    )(h, ab, cd, p, wout, gffn, wg, wu, cf, wd, gple, wpg, wple, gfin, sffn)


def _prep_w_in(w_in):
    cuts = np.cumsum(SPLIT_SIZES)[:-1].tolist()
    gb, gc, gh, pv, cq, ck, cv, cqi, cwi, cki, dq, dk, dv = jnp.split(w_in, cuts, axis=-1)
    pad = jnp.zeros(w_in.shape[:-1] + (LANES - 64 - IDX_HEADS,), w_in.dtype)
    cols = [gb, gc, gh, pv, cq, cqi, dq, dk, dv, ck, ck, cki, cki, cv, cwi, pad]
    return jnp.concatenate(cols, axis=-1).astype(BF16)


def _block_diag_maps(pool_maps):
    depth, G, c, _ = pool_maps.shape
    out = jnp.zeros((depth, G * c, G * c), pool_maps.dtype)
    for g in range(G):
        out = out.at[:, g * c:(g + 1) * c, g * c:(g + 1) * c].set(pool_maps[:, g])
    return out.astype(BF16)


def _heads_rows(a, NS, G, TD, nh):
    a = a.reshape(TD, NS, G, nh, 64).transpose(1, 3, 2, 0, 4)
    return a.reshape(NS, nh * QP, 64)


def _new_page(a, NS, G, TD, PS):
    c = a.shape[1]
    a = a.reshape(TD, NS, G, c).transpose(1, 2, 3, 0)
    return jnp.pad(a, ((0, 0), (0, 0), (0, 0), (0, PS - TD)))


def kernel(x_prompt, x_sample, state_conv_a, state_pool, state_ffn, cache_c_k, cache_c_v, cache_c_idx,
           cache_d_k, cache_d_v, page_table, p_prompt, p_sample, norm_mix, w_in, conv_a, pool_maps,
           pool_scale, diff_lambda, diff_subln, w_out, norm_ffn, w_gate, w_up, conv_ffn, w_down,
           norm_ple, w_ple_gate, w_ple, norm_final):
    depth = w_in.shape[0]
    B, T, D = x_prompt.shape
    DB, TD, _ = x_sample.shape
    NP = page_table.shape[1]
    PS = cache_c_k.shape[2]
    past = NP * PS
    dff = w_gate.shape[2]
    TS1 = min(512, T)
    TS3 = min(256, T)

    w_in_p = _prep_w_in(w_in)
    pm_bd = _block_diag_maps(pool_maps)
    w_out_b, w_gate_b, w_up_b, w_down_b = (w.astype(BF16) for w in (w_out, w_gate, w_up, w_down))
    w_pg_b, w_ple_b = w_ple_gate.astype(BF16), w_ple.astype(BF16)
    gfin = norm_final.reshape(1, D)
    c_idx_t, c_k_t, c_v_t = (c.transpose(0, 1, 3, 2) for c in (cache_c_idx, cache_c_k, cache_c_v))
    d_k_t, d_v_t = (c.transpose(0, 1, 3, 4, 2) for c in (cache_d_k, cache_d_v))
    assert QP % TD == 0 and DB % (QP // TD) == 0 and TD >= 2
    G = QP // TD
    NS = DB // G

    def row(a, i):
        return a[i].reshape(1, -1)

    h = x_prompt
    st_p = []
    n_sel_p = min(TOPK_MAX, T // 4)
    kv_p = tuple(jnp.zeros((B, depth, c, T), F32) for c in (64, 64, 64, 256, 256))
    for i in range(depth):
        lam_init = 0.8 - 0.6 * math.exp(-0.3 * i)
        res = _k1p(h, row(norm_mix, i), w_in_p[i], conv_a[i], pm_bd[i], row(pool_scale, i), kv_p,
                   TS=TS1, layer=i)
        (mixab, cq, cqi, dq, kkb, iib, dkb, cvT, cwiT, dvT, convst, poolst) = res[:12]
        kv_p = res[12:]
        mixcd = _k2p(cq, cqi, cwiT, dq, kkb, iib, cvT, dkb, dvT, diff_lambda[i], diff_subln[i].reshape(-1, 1),
                     n_sel=n_sel_p, lam_init=lam_init)
        h, ffnst = _k3p(h, mixab, mixcd, p_prompt[i], w_out_b[i], row(norm_ffn, i), w_gate_b[i], w_up_b[i],
                        conv_ffn[i], w_down_b[i], row(norm_ple, i), w_pg_b[i], w_ple_b[i], gfin,
                        TS=TS3, final_norm=(i == depth - 1))
        st_p.append((convst[:, 6:8], poolst[:, 1:16], ffnst[:, 6:8]))
    y_prompt = h
    c_k_p, c_v_p, c_idx_p = (a.transpose(0, 1, 3, 2) for a in kv_p[:3])
    d_k_p, d_v_p = (a.reshape(B, depth, DIFF_HEADS, 64, T).transpose(0, 1, 4, 2, 3) for a in kv_p[3:])

    hs = x_sample.transpose(1, 0, 2).reshape(TD * DB, D)
    st_s = []
    n_sel_s = min(TOPK_MAX, (past + TD) // 4)
    for i in range(depth):
        lam_init = 0.8 - 0.6 * math.exp(-0.3 * i)
        sconv = state_conv_a[i].transpose(1, 0, 2)
        spool = state_pool[i].transpose(1, 0, 2)
        sffn = state_ffn[i].transpose(1, 0, 2)
        (mixab, cq, cqi, dq, dk, dv, skk, sii, svw, u_new, pv_new) = _k1s(
            hs, row(norm_mix, i), w_in_p[i], conv_a[i], pm_bd[i], row(pool_scale, i), sconv, spool,
            DB=DB, TD=TD, pos0=past)
        qi = _heads_rows(cqi, NS, G, TD, IDX_HEADS)
        wi = svw[:, 64:64 + IDX_HEADS].reshape(TD, NS, G, IDX_HEADS).transpose(1, 3, 2, 0)
        wi = jnp.broadcast_to(wi.reshape(NS, IDX_HEADS * QP, 1), (NS, IDX_HEADS * QP, LANES))
        qc = _heads_rows(cq, NS, G, TD, DSA_HEADS)
        qd = dq.reshape(TD, NS, G, 256).transpose(1, 2, 0, 3).reshape(NS, 1, QP, 256)
        qd = jnp.broadcast_to(qd, (NS, DIFF_HEADS, QP, 256)).reshape(NS, DIFF_HEADS * QP, 256)
        kin = _new_page(sii[:, :64], NS, G, TD, PS)
        kn = _new_page(skk[:, :64], NS, G, TD, PS)
        vn = _new_page(svw[:, :64], NS, G, TD, PS)
        dkn = _new_page(dk, NS, G, TD, PS)
        dvn = _new_page(dv, NS, G, TD, PS)
        ycd = _k2s(page_table, qi, wi, qc, qd, kin, kn, vn, dkn, dvn, diff_lambda[i], row(diff_subln, i),
                   c_idx_t, c_k_t, c_v_t, d_k_t, d_v_t, layer=i, TD=TD, n_sel=n_sel_s, lam_init=lam_init)
        mixcd = ycd.reshape(NS, G, TD, 512).transpose(2, 0, 1, 3).reshape(TD * DB, 512)
        hs, g_new = _k3s(hs, mixab, mixcd, p_sample[i].transpose(1, 0, 2).reshape(TD * DB, -1), w_out_b[i],
                         row(norm_ffn, i), w_gate_b[i], w_up_b[i], conv_ffn[i], w_down_b[i], row(norm_ple, i),
                         w_pg_b[i], w_ple_b[i], gfin, sffn, DB=DB, TD=TD, final_norm=(i == depth - 1))

        def bm(a, width):
            return a[:, :width].reshape(TD, DB, width).transpose(1, 0, 2)

        new_conv = u_new[TD - 2:].transpose(1, 0, 2)
        new_pool = jnp.concatenate([state_pool[i], pv_new.transpose(1, 0, 2)], axis=1)[:, -POOL_BUF:]
        st_s.append((new_conv, new_pool, g_new.transpose(1, 0, 2), bm(skk, 64), bm(svw, 64), bm(sii, 64),
                     bm(dk, 256).reshape(DB, TD, DIFF_HEADS, 64), bm(dv, 256).reshape(DB, TD, DIFF_HEADS, 64)))
    y_sample = hs.reshape(TD, DB, D).transpose(1, 0, 2)

    def col(outs, j, axis):
        return jnp.stack([o[j] for o in outs], axis=axis)

    return (y_prompt, y_sample, col(st_p, 0, 0), col(st_s, 0, 0), col(st_p, 1, 0), col(st_s, 1, 0),
            col(st_p, 2, 0), col(st_s, 2, 0), c_k_p, col(st_s, 3, 1), c_v_p, col(st_s, 4, 1),
            c_idx_p, col(st_s, 5, 1), d_k_p, col(st_s, 6, 1), d_v_p, col(st_s, 7, 1))
```

```python
import functools
import math

import numpy as np
import jax
import jax.numpy as jnp
from jax import lax
from jax.experimental import pallas as pl
from jax.experimental.pallas import tpu as pltpu

F32 = jnp.float32
BF16 = jnp.bfloat16
I32 = jnp.int32

EPS = 1e-6
GROUP_W = 256
POOL_WINDOWS = (2, 4, 8, 16)
POOL_BUF = 15
DSA_HEADS = 4
DSA_HD = 64
IDX_HEADS = 8
IDX_HD = 64
IDX_SCALE = IDX_HEADS ** -0.5 * IDX_HD ** -0.5
TOPK_MAX = 256
DIFF_HEADS = 4
DIFF_VD = 64
DIFF_QD = 32
SPLIT_SIZES = (256, 256, 256, 256, 256, 64, 64, 512, 8, 64, 256, 256, 256)

LANES = 128
SUBLANES = 8
VMEM_LIMIT = 56 * 1024 * 1024

C_GB, C_GC, C_GH, C_PV = 0, 256, 512, 768
C_CQ, C_CQI, C_DQ, C_DK, C_DV = 1024, 1280, 1792, 2048, 2304
C_KK, C_II, C_VW = 2560, 2688, 2816
D_IN_P = 2944

KEY_BLK = 256
Q_BLK = 128
V_AUG = 80
LOG2E = math.log2(math.e)
NEG = -1e30
INT_MIN = -2 ** 31

NT_DIMS = (((1,), (1,)), ((), ()))


def _dot(a, b):
    return jnp.dot(a, b, preferred_element_type=F32)


def _dot_nt(a, b):
    return lax.dot_general(a, b, NT_DIMS, preferred_element_type=F32)


def _rms(x, g):
    ms = jnp.mean(x * x, axis=-1, keepdims=True)
    return (x * lax.rsqrt(ms + EPS)) * g


def _sigmoid(x):
    return 1.0 / (1.0 + jnp.exp(-x))


def _float_key(x):
    b = lax.bitcast_convert_type(x, I32)
    return b ^ (lax.shift_right_arithmetic(b, 31) & 0x7FFFFFFF)


def _lambda_full(lam_ref, lam_init):
    lp = lam_ref[...]
    s1 = jnp.sum(lp[0:1] * lp[1:2], axis=-1, keepdims=True)
    s2 = jnp.sum(lp[2:3] * lp[3:4], axis=-1, keepdims=True)
    return jnp.exp(s1) - jnp.exp(s2) + lam_init


def _const_spec(shape):
    nd = len(shape)
    return pl.BlockSpec(shape, lambda *_: (0,) * nd)


def _k1p_kernel(x_ref, g_ref, w_ref, ca_ref, pm_ref, ps_ref, _a0, _a1, _a2, _a3, _a4,
                mixab_ref, cq_ref, cqi_ref, dq_ref,
                kkb_ref, iib_ref, dkb_ref, cvT_ref, cwiT_ref, dvT_ref, convst_ref, poolst_ref,
                ckT_ref, cvTf_ref, ckiT_ref, dkT_ref, dvTf_ref,
                extu_ref, extx_ref, *, TS):
    t = pl.program_id(1)

    @pl.when(t == 0)
    def _():
        extu_ref[0:8, :] = jnp.zeros((8, GROUP_W), F32)
        extx_ref[0:16, :] = jnp.zeros((16, GROUP_W), F32)

    xn = _rms(x_ref[0], g_ref[...])
    z = _dot(xn.astype(BF16), w_ref[...])

    u = z[:, C_GC:C_GC + 256] * z[:, C_GH:C_GH + 256]
    extu_ref[8:8 + TS, :] = u
    p1 = extu_ref[7:7 + TS, :]
    p2 = extu_ref[6:6 + TS, :]
    ca = ca_ref[...]
    conv = p2 * ca[0:1] + p1 * ca[1:2] + u * ca[2:3]
    mixab_ref[0, :, 0:256] = z[:, C_GB:C_GB + 256] * conv
    extu_ref[0:8, :] = u[TS - 8:TS]
    convst_ref[0] = u[TS - 8:TS]

    pv = z[:, C_PV:C_PV + 256]
    extx_ref[16:16 + TS, :] = pv

    def sh(j, c0):
        return extx_ref[16 - j:16 - j + TS, c0:c0 + LANES]

    s2 = sh(0, 0) + sh(1, 0)
    s4 = s2 + sh(2, 0) + sh(3, 0)
    s8 = sh(0, LANES)
    for j in range(1, 8):
        s8 = s8 + sh(j, LANES)
    s16 = s8
    for j in range(8, 16):
        s16 = s16 + sh(j, LANES)
    lo = lax.broadcasted_iota(I32, (TS, LANES), 1) < 64
    posp1 = (t * TS + 1 + lax.broadcasted_iota(I32, (TS, LANES), 0)).astype(F32)
    d0 = jnp.where(lo, s2, s4) / jnp.where(lo, jnp.minimum(posp1, 2.0), jnp.minimum(posp1, 4.0)) - pv[:, 0:LANES]
    d1 = jnp.where(lo, s8, s16) / jnp.where(lo, jnp.minimum(posp1, 8.0), jnp.minimum(posp1, 16.0)) - pv[:, LANES:]
    d = jnp.concatenate([d0, d1], axis=1)
    mixab_ref[0, :, 256:512] = _dot(d.astype(BF16), pm_ref[...]) * ps_ref[...]
    extx_ref[0:16, :] = pv[TS - 16:TS]
    poolst_ref[0] = pv[TS - 16:TS]

    cq_ref[0] = z[:, C_CQ:C_CQ + 256].astype(BF16)
    cqi_ref[0] = z[:, C_CQI:C_CQI + 512].astype(BF16)
    dq_ref[0] = z[:, C_DQ:C_DQ + 256].astype(BF16)
    dk = z[:, C_DK:C_DK + 256]
    dv = z[:, C_DV:C_DV + 256]
    skk = z[:, C_KK:C_KK + LANES]
    sii = z[:, C_II:C_II + LANES]
    svw = z[:, C_VW:C_VW + LANES]
    nb = TS // KEY_BLK
    kkb_ref[0] = skk.astype(BF16).reshape(nb, KEY_BLK, LANES)
    iib_ref[0] = sii.astype(BF16).reshape(nb, KEY_BLK, LANES)
    dkb_ref[0] = dk.astype(BF16).reshape(nb, KEY_BLK, 256)
    svw_t = svw.T
    dv_t = dv.T
    ckT_ref[0, 0] = skk.T[0:64]
    ckiT_ref[0, 0] = sii.T[0:64]
    cvTf_ref[0, 0] = svw_t[0:64]
    dkT_ref[0, 0] = dk.T
    dvTf_ref[0, 0] = dv_t
    cwiT_ref[0] = svw_t[64:72]
    ones_rows = jnp.where(lax.broadcasted_iota(I32, (V_AUG - 64, KEY_BLK), 0) == 0, 1.0, 0.0).astype(BF16)
    for j in range(nb):
        cvT_ref[0, j] = jnp.concatenate(
            [svw_t[0:64, j * KEY_BLK:(j + 1) * KEY_BLK].astype(BF16), ones_rows], axis=0)
        for h in range(DIFF_HEADS):
            dvT_ref[0, j, h] = jnp.concatenate(
                [dv_t[h * 64:(h + 1) * 64, j * KEY_BLK:(j + 1) * KEY_BLK].astype(BF16), ones_rows], axis=0)


def _k1p(x, g, w, ca, pm, ps, kv_out, *, TS, layer):
    B, T, D = x.shape
    nt = T // TS
    nb = TS // KEY_BLK
    nkb = T // KEY_BLK

    def row_spec(c):
        return pl.BlockSpec((1, TS, c), lambda b, t: (b, t, 0))

    def blk_spec(r, c):
        return pl.BlockSpec((1, nb, r, c), lambda b, t: (b, t, 0, 0))

    out_shape = (
        jax.ShapeDtypeStruct((B, T, 512), F32),
        jax.ShapeDtypeStruct((B, T, 256), BF16),
        jax.ShapeDtypeStruct((B, T, 512), BF16),
        jax.ShapeDtypeStruct((B, T, 256), BF16),
        jax.ShapeDtypeStruct((B, nkb, KEY_BLK, LANES), BF16),
        jax.ShapeDtypeStruct((B, nkb, KEY_BLK, LANES), BF16),
        jax.ShapeDtypeStruct((B, nkb, KEY_BLK, 256), BF16),
        jax.ShapeDtypeStruct((B, nkb, V_AUG, KEY_BLK), BF16),
        jax.ShapeDtypeStruct((B, 8, T), F32),
        jax.ShapeDtypeStruct((B, nkb, DIFF_HEADS, V_AUG, KEY_BLK), BF16),
        jax.ShapeDtypeStruct((B, 8, 256), F32),
        jax.ShapeDtypeStruct((B, 16, 256), F32),
    ) + tuple(jax.ShapeDtypeStruct(a.shape, a.dtype) for a in kv_out)

    def kv_spec(a):
        return pl.BlockSpec((1, 1, a.shape[2], TS), lambda b, t: (b, layer, 0, t))

    out_specs = (
        row_spec(512), row_spec(256), row_spec(512), row_spec(256),
        blk_spec(KEY_BLK, LANES), blk_spec(KEY_BLK, LANES), blk_spec(KEY_BLK, 256),
        blk_spec(V_AUG, KEY_BLK),
        pl.BlockSpec((1, 8, TS), lambda b, t: (b, 0, t)),
        pl.BlockSpec((1, nb, DIFF_HEADS, V_AUG, KEY_BLK), lambda b, t: (b, t, 0, 0, 0)),
        pl.BlockSpec((1, 8, 256), lambda b, t: (b, 0, 0)),
        pl.BlockSpec((1, 16, 256), lambda b, t: (b, 0, 0)),
    ) + tuple(kv_spec(a) for a in kv_out)
    in_specs = [
        row_spec(D), _const_spec(g.shape), _const_spec(w.shape), _const_spec(ca.shape),
        _const_spec(pm.shape), _const_spec(ps.shape),
    ] + [pl.BlockSpec(memory_space=pl.ANY)] * len(kv_out)
    n_in, n_out = 6, len(out_shape) - len(kv_out)
    return pl.pallas_call(
        functools.partial(_k1p_kernel, TS=TS),
        grid=(B, nt), in_specs=in_specs, out_specs=out_specs, out_shape=out_shape,
        input_output_aliases={n_in + k: n_out + k for k in range(len(kv_out))},
        scratch_shapes=[pltpu.VMEM((TS + 8, GROUP_W), F32), pltpu.VMEM((TS + 16, GROUP_W), F32)],
        compiler_params=pltpu.CompilerParams(dimension_semantics=("arbitrary", "arbitrary"),
                                             vmem_limit_bytes=VMEM_LIMIT),
        name="mixer_in_prompt",
    )(x, g, w, ca, pm, ps, *kv_out)


def _k2p_kernel(cq_ref, cqi_ref, cwiT_ref, dq_ref, kkb_ref, iib_ref, cvT_ref, dkb_ref, dvT_ref,
                lam_ref, subln_ref, out_ref, key_ref, lgc_ref, lgd_ref, accc_ref, accd_ref, mxc_ref, thr_ref,
                *, n_sel, lam_init):
    qb = pl.program_id(1)
    q0 = qb * Q_BLK
    nkb = (q0 + Q_BLK + KEY_BLK - 1) // KEY_BLK
    qpos = q0 + lax.broadcasted_iota(I32, (KEY_BLK, Q_BLK), 1)
    krow = lax.broadcasted_iota(I32, (KEY_BLK, Q_BLK), 0)
    lane_q = lax.broadcasted_iota(I32, (Q_BLK, LANES), 1)
    lo_half = lane_q < 64
    NG = KEY_BLK // SUBLANES

    def head_pair(tile):
        zero = jnp.zeros_like(tile)
        return jnp.concatenate([jnp.where(lo_half, tile, zero), jnp.where(lo_half, zero, tile)], axis=0)

    cqi = cqi_ref[0]
    wT = cwiT_ref[0]
    qi = [head_pair(cqi[:, j * LANES:(j + 1) * LANES]) for j in range(IDX_HEADS // 2)]
    cq = cq_ref[0] * (DSA_HD ** -0.5)
    qc = [head_pair(cq[:, j * LANES:(j + 1) * LANES]) for j in range(DSA_HEADS // 2)]
    dq = dq_ref[0]
    lane256 = lax.broadcasted_iota(I32, (Q_BLK, 256), 1)

    def map_rows(h, mm):
        c0 = h * 64 + mm * DIFF_QD
        return jnp.where(jnp.logical_and(lane256 >= c0, lane256 < c0 + DIFF_QD), dq, jnp.zeros_like(dq))

    qd = [jnp.concatenate([map_rows(h, 0), map_rows(h, 1)], axis=0) for h in range(DIFF_HEADS)]
    c2 = (DIFF_QD ** -0.5) * LOG2E

    def pass_a(kb, mx, masked):
        ki = iib_ref[0, kb]
        acc = jnp.zeros((KEY_BLK, Q_BLK), F32)
        for j in range(IDX_HEADS // 2):
            s = _dot_nt(ki, qi[j])
            acc = acc + jnp.maximum(s[:, :Q_BLK], 0.0) * wT[2 * j:2 * j + 1]
            acc = acc + jnp.maximum(s[:, Q_BLK:], 0.0) * wT[2 * j + 1:2 * j + 2]
        key = _float_key(acc * IDX_SCALE + 0.0)
        if masked:
            causal = (kb * KEY_BLK + krow) <= qpos
            key = jnp.where(causal, key, INT_MIN)
        key_ref[kb] = key
        kk = kkb_ref[0, kb]
        for j in range(DSA_HEADS // 2):
            lgc_ref[j, kb] = _dot_nt(kk, qc[j])
        kd = dkb_ref[0, kb]
        new = list(mx)
        for h in range(DIFF_HEADS):
            s = _dot_nt(kd, qd[h]) * c2
            if masked:
                s = jnp.where(jnp.concatenate([causal, causal], axis=1), s, NEG)
            lgd_ref[h, kb] = s
            new[h] = jnp.maximum(new[h], jnp.max(s.reshape(NG, SUBLANES, 2 * Q_BLK), axis=0))
        return tuple(new)

    n_open = nkb - 1
    mx = lax.fori_loop(0, n_open // 2, lambda i, m: pass_a(2 * i + 1, pass_a(2 * i, m, False), False),
                       tuple(jnp.full((SUBLANES, 2 * Q_BLK), NEG, F32) for _ in range(DIFF_HEADS)))
    mx = lax.cond(n_open % 2 == 1, lambda m: pass_a(n_open - 1, m, False), lambda m: m, mx)
    mx = pass_a(nkb - 1, mx, True)

    @pl.when(nkb % 2 == 1)
    def _():
        key_ref[nkb] = jnp.full((KEY_BLK, Q_BLK), INT_MIN, I32)

    def count(pred):
        def body(i, acc):
            for u in range(2):
                kb = 2 * i + u
                c = jnp.where(pred(kb, key_ref[kb]), 1, 0)
                acc = acc + jnp.sum(c.reshape(NG, SUBLANES, Q_BLK), axis=0)
            return acc
        acc = lax.fori_loop(0, (nkb + 1) // 2, body, jnp.zeros((SUBLANES, Q_BLK), I32))
        return jnp.sum(acc, axis=0, keepdims=True)

    def search(npairs):
        def step(it, base):
            cand = base + lax.shift_left(jnp.int32(1), 31 - it)
            acc = jnp.zeros((SUBLANES, Q_BLK), I32)
            for kb in range(2 * npairs):
                acc = acc + jnp.sum(jnp.where(key_ref[kb] >= cand, 1, 0).reshape(NG, SUBLANES, Q_BLK), axis=0)
            cnt = jnp.sum(acc, axis=0, keepdims=True)
            return jnp.where(cnt >= n_sel, cand, base)
        return lax.fori_loop(0, 32, step, jnp.full((1, Q_BLK), INT_MIN, I32))

    thr_ref[...] = jnp.full((1, Q_BLK), INT_MIN, I32)
    for npairs in range(1, key_ref.shape[0] // 2 + 1):
        @pl.when(jnp.logical_and(q0 + Q_BLK > n_sel, (nkb + 1) // 2 == npairs))
        def _(npairs=npairs):
            thr_ref[...] = search(npairs)

    thr = thr_ref[...]

    cnt_gt = count(lambda kb, k: k > thr)
    cnt_eq = count(lambda kb, k: k == thr)
    need = n_sel - cnt_gt
    excess = jnp.logical_and(cnt_eq > need, thr > INT_MIN)
    any_excess = jnp.max(jnp.where(excess, 1, 0))

    @pl.when(any_excess > 0)
    def _():
        def tie_body(it, jp):
            c = jp + lax.shift_left(jnp.int32(1), 10 - it)
            f = count(lambda kb, k: jnp.logical_and(k == thr, (kb * KEY_BLK + krow) < c))
            return jnp.where(f < need, c, jp)
        jp = lax.fori_loop(0, 11, tie_body, jnp.zeros((1, Q_BLK), I32))

        def demote(kb, carry):
            k = key_ref[kb]
            drop = jnp.logical_and(jnp.logical_and(k == thr, (kb * KEY_BLK + krow) > jp), excess)
            key_ref[kb] = jnp.where(drop, thr - 1, k)
            return carry
        lax.fori_loop(0, nkb, demote, 0)

    thr_sel = jnp.maximum(thr, INT_MIN + 1)

    def masked_c(j, kb):
        sel = key_ref[kb] >= thr_sel
        lg = lgc_ref[j, kb]
        return jnp.concatenate([jnp.where(sel, lg[:, :Q_BLK], NEG), jnp.where(sel, lg[:, Q_BLK:], NEG)], axis=1)

    md = [jnp.max(m, axis=0, keepdims=True) for m in mx]
    accd_ref[...] = jnp.zeros(accd_ref.shape, F32)

    mxc_ref[...] = jnp.full(mxc_ref.shape, NEG, F32)

    def over_blocks(step):
        def quad(i, carry):
            step(tuple(4 * i + u for u in range(4)))
            return carry
        lax.fori_loop(0, nkb // 4, quad, 0)
        rest = (nkb // 4) * 4

        @pl.when(nkb % 4 >= 2)
        def _():
            step((rest, rest + 1))

        @pl.when(nkb % 2 == 1)
        def _():
            step((nkb - 1,))

    def pass_b(kbs):
        for j in range(DSA_HEADS // 2):
            m = mxc_ref[j]
            for kb in kbs:
                m = jnp.maximum(m, jnp.max(masked_c(j, kb).reshape(NG, SUBLANES, 2 * Q_BLK), axis=0))
            mxc_ref[j] = m
        for h in range(DIFF_HEADS):
            p = jnp.concatenate([jnp.exp2(lgd_ref[h, kb] - md[h]).astype(BF16) for kb in kbs], axis=0)
            vT = jnp.concatenate([dvT_ref[0, kb, h] for kb in kbs], axis=1)
            accd_ref[h] += _dot(vT, p)

    over_blocks(pass_b)
    mc = [jnp.max(mxc_ref[j], axis=0, keepdims=True) for j in range(DSA_HEADS // 2)]

    accc_ref[...] = jnp.zeros(accc_ref.shape, F32)

    def pass_c(kbs):
        vT = jnp.concatenate([cvT_ref[0, kb] for kb in kbs], axis=1)
        for j in range(DSA_HEADS // 2):
            p = jnp.concatenate([jnp.exp(masked_c(j, kb) - mc[j]).astype(BF16) for kb in kbs], axis=0)
            accc_ref[j] += _dot(vT, p)

    over_blocks(pass_c)
    outs = []
    for h in range(DSA_HEADS):
        a = accc_ref[h // 2, :, (h % 2) * Q_BLK:(h % 2 + 1) * Q_BLK]
        outs.append(a[0:DSA_HD] / a[DSA_HD:DSA_HD + 1])
    out_ref[0, :, 0:256] = jnp.concatenate(outs, axis=0).T

    lam = _lambda_full(lam_ref, lam_init)
    subln = subln_ref[...]
    outs = []
    for h in range(DIFF_HEADS):
        a1 = accd_ref[h, :, 0:Q_BLK]
        a2 = accd_ref[h, :, Q_BLK:2 * Q_BLK]
        o = a1[0:DIFF_VD] / a1[DIFF_VD:DIFF_VD + 1] - lam * (a2[0:DIFF_VD] / a2[DIFF_VD:DIFF_VD + 1])
        ms = jnp.mean(o * o, axis=0, keepdims=True)
        outs.append((o * lax.rsqrt(ms + EPS)) * subln * (1.0 - lam_init))
    out_ref[0, :, 256:512] = jnp.concatenate(outs, axis=0).T


def _k2p(cq, cqi, cwiT, dq, kkb, iib, cvT, dkb, dvT, lam, subln, *, n_sel, lam_init):
    B, T, _ = cq.shape
    nq = T // Q_BLK
    nkb = T // KEY_BLK

    def q_spec(c):
        return pl.BlockSpec((1, Q_BLK, c), lambda b, q: (b, q, 0))

    def seq_spec(r, c):
        return pl.BlockSpec((1, nkb, r, c), lambda b, q: (b, 0, 0, 0))

    in_specs = [
        q_spec(256), q_spec(512), pl.BlockSpec((1, 8, Q_BLK), lambda b, q: (b, 0, q)), q_spec(256),
        seq_spec(KEY_BLK, LANES), seq_spec(KEY_BLK, LANES), seq_spec(V_AUG, KEY_BLK),
        seq_spec(KEY_BLK, 256),
        pl.BlockSpec((1, nkb, DIFF_HEADS, V_AUG, KEY_BLK), lambda b, q: (b, 0, 0, 0, 0)),
        _const_spec(lam.shape), _const_spec(subln.shape),
    ]
    return pl.pallas_call(
        functools.partial(_k2p_kernel, n_sel=n_sel, lam_init=lam_init),
        grid=(B, nq), in_specs=in_specs, out_specs=q_spec(512),
        out_shape=jax.ShapeDtypeStruct((B, T, 512), F32),
        scratch_shapes=[pltpu.VMEM((nkb + 1, KEY_BLK, Q_BLK), I32),
                        pltpu.VMEM((DSA_HEADS // 2, nkb, KEY_BLK, 2 * Q_BLK), F32),
                        pltpu.VMEM((DIFF_HEADS, nkb, KEY_BLK, 2 * Q_BLK), F32),
                        pltpu.VMEM((DSA_HEADS // 2, V_AUG, 2 * Q_BLK), F32),
                        pltpu.VMEM((DIFF_HEADS, V_AUG, 2 * Q_BLK), F32),
                        pltpu.VMEM((DSA_HEADS // 2, SUBLANES, 2 * Q_BLK), F32),
                        pltpu.VMEM((1, Q_BLK), I32)],
        compiler_params=pltpu.CompilerParams(dimension_semantics=("arbitrary", "arbitrary"),
                                             vmem_limit_bytes=VMEM_LIMIT),
        name="attention_prompt",
    )(cq, cqi, cwiT, dq, kkb, iib, cvT, dkb, dvT, lam, subln)


def _ple_tail(h2, p_bf, gple, wpg, wple, gfin, final_norm):
    gate = _sigmoid(_dot(_rms(h2, gple).astype(BF16), wpg))
    h3 = h2 + _dot(p_bf, wple) * gate
    if final_norm:
        h3 = _rms(h3, gfin)
    return h3


def _k3p_kernel(h_ref, ab_ref, cd_ref, p_ref, wout_ref, gffn_ref, wg_ref, wu_ref, cf_ref, wd_ref,
                gple_ref, wpg_ref, wple_ref, gfin_ref, hout_ref, ffnst_ref, extg_ref, *, TS, final_norm):
    t = pl.program_id(1)

    @pl.when(t == 0)
    def _():
        extg_ref[0:8, :] = jnp.zeros((8, extg_ref.shape[1]), F32)

    mix = jnp.concatenate([ab_ref[0], cd_ref[0]], axis=1).astype(BF16)
    h1 = h_ref[0] + _dot(mix, wout_ref[...])
    xn2 = _rms(h1, gffn_ref[...]).astype(BF16)
    g = _dot(xn2, wg_ref[...])
    extg_ref[8:8 + TS, :] = g
    p1 = extg_ref[7:7 + TS, :]
    p2 = extg_ref[6:6 + TS, :]
    cf = cf_ref[...]
    gc = p2 * cf[0:1] + p1 * cf[1:2] + g * cf[2:3]
    extg_ref[0:8, :] = g[TS - 8:TS]
    ffnst_ref[0] = g[TS - 8:TS]
    act = (gc * _sigmoid(gc)) * _dot(xn2, wu_ref[...])
    h2 = h1 + _dot(act.astype(BF16), wd_ref[...])
    hout_ref[0] = _ple_tail(h2, p_ref[0].astype(BF16), gple_ref[...], wpg_ref[...], wple_ref[...],
                            gfin_ref[...], final_norm)


def _k3p(h, ab, cd, p, wout, gffn, wg, wu, cf, wd, gple, wpg, wple, gfin, *, TS, final_norm):
    B, T, D = h.shape
    dff = wg.shape[1]
    nt = T // TS

    def row_spec(c):
        return pl.BlockSpec((1, TS, c), lambda b, t: (b, t, 0))

    def w_spec(a):
        nd = a.ndim
        return pl.BlockSpec(a.shape, lambda *_: (0,) * nd, pipeline_mode=pl.Buffered(1))

    in_specs = [row_spec(D), row_spec(512), row_spec(512), row_spec(p.shape[2]),
                w_spec(wout), w_spec(gffn), w_spec(wg), w_spec(wu), w_spec(cf), w_spec(wd),
                w_spec(gple), w_spec(wpg), w_spec(wple), w_spec(gfin)]
    out_specs = (row_spec(D), pl.BlockSpec((1, 8, dff), lambda b, t: (b, 0, 0)))
    out_shape = (jax.ShapeDtypeStruct((B, T, D), F32), jax.ShapeDtypeStruct((B, 8, dff), F32))
    return pl.pallas_call(
        functools.partial(_k3p_kernel, TS=TS, final_norm=final_norm),
        grid=(B, nt), in_specs=in_specs, out_specs=out_specs, out_shape=out_shape,
        scratch_shapes=[pltpu.VMEM((TS + 8, dff), F32)],
        compiler_params=pltpu.CompilerParams(dimension_semantics=("arbitrary", "arbitrary"),
                                             vmem_limit_bytes=VMEM_LIMIT),
        name="channel_mix_prompt",
    )(h, ab, cd, p, wout, gffn, wg, wu, cf, wd, gple, wpg, wple, gfin)


def _k1s_kernel(x_ref, g_ref, w_ref, ca_ref, pm_ref, ps_ref, sconv_ref, spool_ref,
                mixab_ref, cq_ref, cqi_ref, dq_ref, dk_ref, dv_ref, skk_ref, sii_ref, svw_ref,
                u_ref, pv_ref, *, DB, TD, pos0):
    xn = _rms(x_ref[...], g_ref[...])
    z = _dot(xn.astype(BF16), w_ref[...])

    def slab(t, c0, c1):
        return z[t * DB:(t + 1) * DB, c0:c1]

    ca = ca_ref[...]
    us = [slab(t, C_GC, C_GC + 256) * slab(t, C_GH, C_GH + 256) for t in range(TD)]
    extu = [sconv_ref[0], sconv_ref[1]] + us
    pvs = [slab(t, C_PV, C_PV + 256) for t in range(TD)]
    extx = [spool_ref[j] for j in range(POOL_BUF)] + pvs
    lo = lax.broadcasted_iota(I32, (DB, LANES), 1) < 64
    pm = pm_ref[...]
    ps = ps_ref[...]
    for t in range(TD):
        conv = extu[t] * ca[0:1] + extu[t + 1] * ca[1:2] + extu[t + 2] * ca[2:3]
        mixab_ref[t * DB:(t + 1) * DB, 0:256] = slab(t, C_GB, C_GB + 256) * conv
        u_ref[t] = us[t]
        pv_ref[t] = pvs[t]
        e = POOL_BUF + t

        def win(n, c0):
            s = extx[e][:, c0:c0 + LANES]
            for j in range(1, n):
                s = s + extx[e - j][:, c0:c0 + LANES]
            return s

        cnt = [float(min(w, pos0 + t + 1)) for w in POOL_WINDOWS]
        d0 = jnp.where(lo, win(2, 0) / cnt[0], win(4, 0) / cnt[1]) - pvs[t][:, 0:LANES]
        d1 = jnp.where(lo, win(8, LANES) / cnt[2], win(16, LANES) / cnt[3]) - pvs[t][:, LANES:]
        d = jnp.concatenate([d0, d1], axis=1)
        mixab_ref[t * DB:(t + 1) * DB, 256:512] = _dot(d.astype(BF16), pm) * ps

    cq_ref[...] = z[:, C_CQ:C_CQ + 256].astype(BF16)
    cqi_ref[...] = z[:, C_CQI:C_CQI + 512].astype(BF16)
    dq_ref[...] = z[:, C_DQ:C_DQ + 256].astype(BF16)
    dk_ref[...] = z[:, C_DK:C_DK + 256]
    dv_ref[...] = z[:, C_DV:C_DV + 256]
    skk_ref[...] = z[:, C_KK:C_KK + LANES]
    sii_ref[...] = z[:, C_II:C_II + LANES]
    svw_ref[...] = z[:, C_VW:C_VW + LANES]


def _k1s(x, g, w, ca, pm, ps, sconv, spool, *, DB, TD, pos0):
    R = x.shape[0]
    out_shape = (
        jax.ShapeDtypeStruct((R, 512), F32), jax.ShapeDtypeStruct((R, 256), BF16),
        jax.ShapeDtypeStruct((R, 512), BF16), jax.ShapeDtypeStruct((R, 256), BF16),
        jax.ShapeDtypeStruct((R, 256), F32), jax.ShapeDtypeStruct((R, 256), F32),
        jax.ShapeDtypeStruct((R, LANES), F32), jax.ShapeDtypeStruct((R, LANES), F32),
        jax.ShapeDtypeStruct((R, LANES), F32),
        jax.ShapeDtypeStruct((TD, DB, 256), F32), jax.ShapeDtypeStruct((TD, DB, 256), F32),
    )
    args = (x, g, w, ca, pm, ps, sconv, spool)
    return pl.pallas_call(
        functools.partial(_k1s_kernel, DB=DB, TD=TD, pos0=pos0),
        grid=(1,), in_specs=[_const_spec(a.shape) for a in args],
        out_specs=tuple(_const_spec(s.shape) for s in out_shape), out_shape=out_shape,
        compiler_params=pltpu.CompilerParams(dimension_semantics=("arbitrary",), vmem_limit_bytes=VMEM_LIMIT),
        name="mixer_in_sample",
    )(*args)


QP = 8


def _wrap32(v):
    return ((v + 2 ** 31) % 2 ** 32) - 2 ** 31


def _k2s_kernel(pt_ref, qi_ref, wi_ref, qc_ref, qd_ref, kin_ref, kn_ref, vn_ref, dkn_ref, dvn_ref,
                lam_ref, subln_ref, cidx_hbm, ck_hbm, cv_hbm, dk_hbm, dv_hbm,
                out_ref, key_ref, clog_ref, dlog_ref, idx_refs, k_refs, v_refs, dk_refs, dv_refs, sem,
                *, NP, PS, TD, n_sel, lam_init, layer):
    G = QP // TD
    n = G * NP
    SC = (NP + 1) * PS

    i = pl.program_id(0)
    slot = lax.rem(i, 2)
    hbms = (cidx_hbm, ck_hbm, cv_hbm, dk_hbm, dv_hbm)
    bufs = (idx_refs, k_refs, v_refs, dk_refs, dv_refs)

    def page_copy(hbm, buf, page, sl, gp):
        return pltpu.make_async_copy(hbm.at[page, layer], buf.at[sl, gp], sem.at[sl])

    def issue(step, sl):
        def body(gp, carry):
            page = pt_ref[G * step + gp // NP, gp % NP]
            for hbm, buf in zip(hbms, bufs):
                page_copy(hbm, buf, page, sl, gp).start()
            return carry
        lax.fori_loop(0, n, body, 0)

    @pl.when(i == 0)
    def _():
        issue(0, 0)

    @pl.when(i + 1 < pl.num_programs(0))
    def _():
        issue(i + 1, 1 - slot)

    def wait_body(gp, carry):
        for hbm, buf in zip(hbms, bufs):
            page_copy(hbm, buf, 0, slot, gp).wait()
        return carry

    lax.fori_loop(0, n, wait_body, 0)

    def own_half(x):
        elem0 = (lax.broadcasted_iota(I32, (x.shape[0], PS), 0) % QP) < TD
        return jnp.where(elem0, x[:, :PS], x[:, PS:])

    def split_half(x):
        elem0 = (lax.broadcasted_iota(I32, x.shape, 0) % QP) < TD
        zero = jnp.zeros_like(x)
        return jnp.concatenate([jnp.where(elem0, x, zero), jnp.where(elem0, zero, x)], axis=1)

    def pages(refs, new_ref, p, rows):
        tiles = [(refs[slot, g * NP + p] if p < NP else new_ref[0, g]).reshape(rows, PS).astype(BF16)
                 for g in range(G)]
        return jnp.concatenate(tiles, axis=1)

    qi = qi_ref[0]
    wi = wi_ref[0]
    col = lax.broadcasted_iota(I32, (QP, PS), 1)
    qt = lax.broadcasted_iota(I32, (QP, PS), 0) % TD
    new_ok = jnp.logical_and(col < TD, col <= qt)
    for p in range(NP + 1):
        s = own_half(_dot(qi, pages(idx_refs, kin_ref, p, IDX_HD)))
        r = (jnp.maximum(s, 0.0) * wi).reshape(IDX_HEADS, QP, PS)
        score = jnp.sum(r, axis=0) * IDX_SCALE + 0.0
        key = _float_key(score)
        if p == NP:
            key = jnp.where(new_ok, key, INT_MIN)
        key_ref[:, p * PS:(p + 1) * PS] = key

    qc = qc_ref[0] * (DSA_HD ** -0.5)
    for p in range(NP + 1):
        clog_ref[:, p * PS:(p + 1) * PS] = own_half(_dot(qc, pages(k_refs, kn_ref, p, DSA_HD)))

    qd0 = qd_ref[0]
    rhead = lax.broadcasted_iota(I32, qd0.shape, 0) // QP
    lane = lax.broadcasted_iota(I32, qd0.shape, 1)
    zero = jnp.zeros_like(qd0)
    qd = jnp.concatenate(
        [jnp.where(jnp.logical_and(lane >= rhead * 64 + mm * DIFF_QD, lane < rhead * 64 + (mm + 1) * DIFF_QD),
                   qd0, zero) for mm in range(2)], axis=0)
    HD = DIFF_HEADS * DIFF_VD
    nrow = 2 * DIFF_HEADS * QP
    for p in range(NP + 1):
        s = own_half(_dot(qd, pages(dk_refs, dkn_ref, p, HD))) * (DIFF_QD ** -0.5)
        if p == NP:
            ncol = lax.broadcasted_iota(I32, (nrow, PS), 1)
            nqt = lax.broadcasted_iota(I32, (nrow, PS), 0) % TD
            s = jnp.where(jnp.logical_and(ncol < TD, ncol <= nqt), s, NEG)
        dlog_ref[:, p * PS:(p + 1) * PS] = s
    lgd = dlog_ref[...]
    md = jnp.max(lgd, axis=1, keepdims=True)
    ped = jnp.exp(lgd - md)
    ld = jnp.sum(ped, axis=1, keepdims=True)
    dlog_ref[...] = ped
    accd = jnp.zeros((nrow, HD), F32)
    for p in range(NP + 1):
        pp = split_half(dlog_ref[:, p * PS:(p + 1) * PS]).astype(BF16)
        accd = accd + _dot_nt(pp, pages(dv_refs, dvn_ref, p, HD))
    od = accd / ld
    half = DIFF_HEADS * QP
    lam = _lambda_full(lam_ref, lam_init)
    subln = subln_ref[...]
    for h in range(DIFF_HEADS):
        o = (od[h * QP:(h + 1) * QP, h * 64:(h + 1) * 64]
             - lam * od[half + h * QP:half + (h + 1) * QP, h * 64:(h + 1) * 64])
        ms = jnp.mean(o * o, axis=-1, keepdims=True)
        out_ref[0, :, 256 + h * 64:256 + (h + 1) * 64] = (o * lax.rsqrt(ms + EPS)) * subln * (1.0 - lam_init)

    kcol = lax.broadcasted_iota(I32, (QP, SC), 1)

    def count(pred):
        c = jnp.where(pred(key_ref[...]), 1.0, 0.0)
        return jnp.sum(c, axis=1, keepdims=True)

    thr = jnp.full((QP, 1), INT_MIN, I32)
    for sh, nbits in [(30, 2)] + [(27 - 3 * s, 3) for s in range(10)]:
        passed = jnp.zeros((QP, 1), I32)
        for j in range(1, 2 ** nbits):
            cnt = count(lambda k, c=thr + _wrap32(j << sh): k >= c)
            passed = passed + jnp.where(cnt >= n_sel, 1, 0)
        thr = thr + lax.shift_left(passed, sh)
    cnt_gt = count(lambda k: k > thr)
    cnt_eq = count(lambda k: k == thr)
    need = n_sel - cnt_gt
    excess = jnp.logical_and(cnt_eq > need, thr > INT_MIN)
    any_excess = jnp.max(jnp.where(excess, 1, 0))

    @pl.when(any_excess > 0)
    def _():
        def tie_body(it, jp):
            c = jp + lax.shift_left(jnp.int32(1), 12 - it)
            f = count(lambda k: jnp.logical_and(k == thr, kcol < c))
            return jnp.where(f < need, c, jp)
        jp = lax.fori_loop(0, 13, tie_body, jnp.zeros((QP, 1), I32))
        k = key_ref[...]
        drop = jnp.logical_and(jnp.logical_and(k == thr, kcol > jp), excess)
        key_ref[...] = jnp.where(drop, thr - 1, k)

    thr_sel = jnp.maximum(thr, INT_MIN + 1)
    sel = key_ref[...] >= thr_sel

    lg = jnp.where(sel[None], clog_ref[...].reshape(DSA_HEADS, QP, SC), NEG)
    lg = lg.reshape(DSA_HEADS * QP, SC)
    m = jnp.max(lg, axis=1, keepdims=True)
    pe = jnp.exp(lg - m)
    l = jnp.sum(pe, axis=1, keepdims=True)
    clog_ref[...] = pe
    acc = jnp.zeros((DSA_HEADS * QP, DSA_HD), F32)
    for p in range(NP + 1):
        pp = split_half(clog_ref[:, p * PS:(p + 1) * PS]).astype(BF16)
        acc = acc + _dot_nt(pp, pages(v_refs, vn_ref, p, DSA_HD))
    oc = acc / l
    for h in range(DSA_HEADS):
        out_ref[0, :, h * 64:(h + 1) * 64] = oc[h * QP:(h + 1) * QP]


def _k2s(page_table, qi, wi, qc, qd, kin, kn, vn, dkn, dvn, lam, subln,
         c_idx, c_k, c_v, d_k, d_v, *, layer, TD, n_sel, lam_init):
    DB, NP = page_table.shape
    PS = c_k.shape[3]
    G = QP // TD
    NS = DB // G

    def b_spec(a):
        return pl.BlockSpec((1,) + a.shape[1:], lambda b, pt: (b,) + (0,) * (a.ndim - 1))

    def c_spec(a):
        nd = a.ndim
        return pl.BlockSpec(a.shape, lambda b, pt: (0,) * nd)

    in_specs = [b_spec(qi), b_spec(wi), b_spec(qc), b_spec(qd), b_spec(kin), b_spec(kn), b_spec(vn),
                b_spec(dkn), b_spec(dvn), c_spec(lam), c_spec(subln)]
    in_specs += [pl.BlockSpec(memory_space=pl.ANY)] * 5
    SC = (NP + 1) * PS
    n = G * NP
    grid_spec = pltpu.PrefetchScalarGridSpec(
        num_scalar_prefetch=1, grid=(NS,), in_specs=in_specs,
        out_specs=pl.BlockSpec((1, QP, 512), lambda b, pt: (b, 0, 0)),
        scratch_shapes=[pltpu.VMEM((QP, SC), I32), pltpu.VMEM((DSA_HEADS * QP, SC), F32),
                        pltpu.VMEM((2 * DIFF_HEADS * QP, SC), F32)]
        + [pltpu.VMEM((2, n, 64, PS), F32)] * 3 + [pltpu.VMEM((2, n, DIFF_HEADS, 64, PS), F32)] * 2
        + [pltpu.SemaphoreType.DMA((2,))],
    )
    args = [page_table, qi, wi, qc, qd, kin, kn, vn, dkn, dvn, lam, subln, c_idx, c_k, c_v, d_k, d_v]
    return pl.pallas_call(
        functools.partial(_k2s_kernel, NP=NP, PS=PS, TD=TD, n_sel=n_sel, lam_init=lam_init, layer=layer),
        grid_spec=grid_spec, out_shape=jax.ShapeDtypeStruct((NS, QP, 512), F32),
        compiler_params=pltpu.CompilerParams(dimension_semantics=("arbitrary",), vmem_limit_bytes=VMEM_LIMIT),
        name="attention_sample",
    )(*args)


FF_TILE = 256


def _k3s_kernel(h_ref, ab_ref, cd_ref, p_ref, wout_ref, gffn_ref, wg_ref, wu_ref, cf_ref, wd_ref,
                gple_ref, wpg_ref, wple_ref, gfin_ref, sffn_ref, hout_ref, gout_ref,
                h1_ref, xn2_ref, acc_ref, *, DB, TD, final_norm):
    f = pl.program_id(0)

    @pl.when(f == 0)
    def _():
        mix = jnp.concatenate([ab_ref[...], cd_ref[...]], axis=1).astype(BF16)
        h1 = h_ref[...] + _dot(mix, wout_ref[...])
        h1_ref[...] = h1
        xn2_ref[...] = _rms(h1, gffn_ref[...]).astype(BF16)
        acc_ref[...] = jnp.zeros(acc_ref.shape, F32)

    xn2 = xn2_ref[...]
    g = _dot(xn2, wg_ref[...])
    cf = cf_ref[...]
    gs = [sffn_ref[0], sffn_ref[1]] + [g[t * DB:(t + 1) * DB] for t in range(TD)]
    gc = jnp.concatenate([gs[t] * cf[0:1] + gs[t + 1] * cf[1:2] + gs[t + 2] * cf[2:3] for t in range(TD)], axis=0)
    act = (gc * _sigmoid(gc)) * _dot(xn2, wu_ref[...])
    acc_ref[...] += _dot(act.astype(BF16), wd_ref[...])
    gout_ref[0] = gs[TD]
    gout_ref[1] = gs[TD + 1]

    @pl.when(f == pl.num_programs(0) - 1)
    def _():
        h2 = h1_ref[...] + acc_ref[...]
        hout_ref[...] = _ple_tail(h2, p_ref[...].astype(BF16), gple_ref[...], wpg_ref[...], wple_ref[...],
                                  gfin_ref[...], final_norm)


def _k3s(h, ab, cd, p, wout, gffn, wg, wu, cf, wd, gple, wpg, wple, gfin, sffn, *, DB, TD, final_norm):
    R, D = h.shape
    dff = wg.shape[1]
    nf = dff // FF_TILE
    cs = _const_spec
    in_specs = [cs(h.shape), cs(ab.shape), cs(cd.shape), cs(p.shape), cs(wout.shape), cs(gffn.shape),
                pl.BlockSpec((D, FF_TILE), lambda f: (0, f)), pl.BlockSpec((D, FF_TILE), lambda f: (0, f)),
                pl.BlockSpec((3, FF_TILE), lambda f: (0, f)), pl.BlockSpec((FF_TILE, D), lambda f: (f, 0)),
                cs(gple.shape), cs(wpg.shape), cs(wple.shape), cs(gfin.shape),
                pl.BlockSpec((2, DB, FF_TILE), lambda f: (0, 0, f))]
    out_specs = (cs((R, D)), pl.BlockSpec((2, DB, FF_TILE), lambda f: (0, 0, f)))
    out_shape = (jax.ShapeDtypeStruct((R, D), F32), jax.ShapeDtypeStruct((2, DB, dff), F32))
    return pl.pallas_call(
        functools.partial(_k3s_kernel, DB=DB, TD=TD, final_norm=final_norm),
        grid=(nf,), in_specs=in_specs, out_specs=out_specs, out_shape=out_shape,
        scratch_shapes=[pltpu.VMEM((R, D), F32), pltpu.VMEM((R, D), BF16), pltpu.VMEM((R, D), F32)],
        compiler_params=pltpu.CompilerParams(dimension_semantics=("arbitrary",), vmem_limit_bytes=VMEM_LIMIT),
        name="channel_mix_sample",
    )(h, ab, cd, p, wout, gffn, wg, wu, cf, wd, gple, wpg, wple, gfin, sffn)


def _prep_w_in(w_in):
    cuts = np.cumsum(SPLIT_SIZES)[:-1].tolist()
    gb, gc, gh, pv, cq, ck, cv, cqi, cwi, cki, dq, dk, dv = jnp.split(w_in, cuts, axis=-1)
    pad = jnp.zeros(w_in.shape[:-1] + (LANES - 64 - IDX_HEADS,), w_in.dtype)
    cols = [gb, gc, gh, pv, cq, cqi, dq, dk, dv, ck, ck, cki, cki, cv, cwi, pad]
    return jnp.concatenate(cols, axis=-1).astype(BF16)


def _block_diag_maps(pool_maps):
    depth, G, c, _ = pool_maps.shape
    out = jnp.zeros((depth, G * c, G * c), pool_maps.dtype)
    for g in range(G):
        out = out.at[:, g * c:(g + 1) * c, g * c:(g + 1) * c].set(pool_maps[:, g])
    return out.astype(BF16)


def _heads_rows(a, NS, G, TD, nh):
    a = a.reshape(TD, NS, G, nh, 64).transpose(1, 3, 2, 0, 4)
    return a.reshape(NS, nh * QP, 64)


def _new_page(a, NS, G, TD, PS):
    c = a.shape[1]
    a = a.astype(BF16).reshape(TD, NS, G, c).transpose(1, 2, 3, 0)
    return jnp.pad(a, ((0, 0), (0, 0), (0, 0), (0, PS - TD)))


def kernel(x_prompt, x_sample, state_conv_a, state_pool, state_ffn, cache_c_k, cache_c_v, cache_c_idx,
           cache_d_k, cache_d_v, page_table, p_prompt, p_sample, norm_mix, w_in, conv_a, pool_maps,
           pool_scale, diff_lambda, diff_subln, w_out, norm_ffn, w_gate, w_up, conv_ffn, w_down,
           norm_ple, w_ple_gate, w_ple, norm_final):
    depth = w_in.shape[0]
    B, T, D = x_prompt.shape
    DB, TD, _ = x_sample.shape
    NP = page_table.shape[1]
    PS = cache_c_k.shape[2]
    past = NP * PS
    dff = w_gate.shape[2]
    TS1 = min(512, T)
    TS3 = min(512, T)

    w_in_p = _prep_w_in(w_in)
    pm_bd = _block_diag_maps(pool_maps)
    w_out_b, w_gate_b, w_up_b, w_down_b = (w.astype(BF16) for w in (w_out, w_gate, w_up, w_down))
    w_pg_b, w_ple_b = w_ple_gate.astype(BF16), w_ple.astype(BF16)
    gfin = norm_final.reshape(1, D)
    c_idx_t, c_k_t, c_v_t = (c.transpose(0, 1, 3, 2) for c in (cache_c_idx, cache_c_k, cache_c_v))
    d_k_t, d_v_t = (c.transpose(0, 1, 3, 4, 2) for c in (cache_d_k, cache_d_v))
    assert QP % TD == 0 and DB % (QP // TD) == 0 and TD >= 2
    G = QP // TD
    NS = DB // G

    def row(a, i):
        return a[i].reshape(1, -1)

    h = x_prompt
    st_p = []
    n_sel_p = min(TOPK_MAX, T // 4)
    kv_p = tuple(jnp.zeros((B, depth, c, T), F32) for c in (64, 64, 64, 256, 256))
    for i in range(depth):
        lam_init = 0.8 - 0.6 * math.exp(-0.3 * i)
        res = _k1p(h, row(norm_mix, i), w_in_p[i], conv_a[i], pm_bd[i], row(pool_scale, i), kv_p,
                   TS=TS1, layer=i)
        (mixab, cq, cqi, dq, kkb, iib, dkb, cvT, cwiT, dvT, convst, poolst) = res[:12]
        kv_p = res[12:]
        mixcd = _k2p(cq, cqi, cwiT, dq, kkb, iib, cvT, dkb, dvT, diff_lambda[i], diff_subln[i].reshape(-1, 1),
                     n_sel=n_sel_p, lam_init=lam_init)
        h, ffnst = _k3p(h, mixab, mixcd, p_prompt[i], w_out_b[i], row(norm_ffn, i), w_gate_b[i], w_up_b[i],
                        conv_ffn[i], w_down_b[i], row(norm_ple, i), w_pg_b[i], w_ple_b[i], gfin,
                        TS=TS3, final_norm=(i == depth - 1))
        st_p.append((convst[:, 6:8], poolst[:, 1:16], ffnst[:, 6:8]))
    y_prompt = h
    c_k_p, c_v_p, c_idx_p = (a.transpose(0, 1, 3, 2) for a in kv_p[:3])
    d_k_p, d_v_p = (a.reshape(B, depth, DIFF_HEADS, 64, T).transpose(0, 1, 4, 2, 3) for a in kv_p[3:])

    hs = x_sample.transpose(1, 0, 2).reshape(TD * DB, D)
    st_s = []
    n_sel_s = min(TOPK_MAX, (past + TD) // 4)
    for i in range(depth):
        lam_init = 0.8 - 0.6 * math.exp(-0.3 * i)
        sconv = state_conv_a[i].transpose(1, 0, 2)
        spool = state_pool[i].transpose(1, 0, 2)
        sffn = state_ffn[i].transpose(1, 0, 2)
        (mixab, cq, cqi, dq, dk, dv, skk, sii, svw, u_new, pv_new) = _k1s(
            hs, row(norm_mix, i), w_in_p[i], conv_a[i], pm_bd[i], row(pool_scale, i), sconv, spool,
            DB=DB, TD=TD, pos0=past)
        qi = _heads_rows(cqi, NS, G, TD, IDX_HEADS)
        wi = svw[:, 64:64 + IDX_HEADS].reshape(TD, NS, G, IDX_HEADS).transpose(1, 3, 2, 0)
        wi = jnp.broadcast_to(wi.reshape(NS, IDX_HEADS * QP, 1), (NS, IDX_HEADS * QP, LANES))
        qc = _heads_rows(cq, NS, G, TD, DSA_HEADS)
        qd = dq.reshape(TD, NS, G, 256).transpose(1, 2, 0, 3).reshape(NS, 1, QP, 256)
        qd = jnp.broadcast_to(qd, (NS, DIFF_HEADS, QP, 256)).reshape(NS, DIFF_HEADS * QP, 256)
        kin = _new_page(sii[:, :64], NS, G, TD, PS)
        kn = _new_page(skk[:, :64], NS, G, TD, PS)
        vn = _new_page(svw[:, :64], NS, G, TD, PS)
        dkn = _new_page(dk, NS, G, TD, PS)
        dvn = _new_page(dv, NS, G, TD, PS)
        ycd = _k2s(page_table, qi, wi, qc, qd, kin, kn, vn, dkn, dvn, diff_lambda[i], row(diff_subln, i),
                   c_idx_t, c_k_t, c_v_t, d_k_t, d_v_t, layer=i, TD=TD, n_sel=n_sel_s, lam_init=lam_init)
        mixcd = ycd.reshape(NS, G, TD, 512).transpose(2, 0, 1, 3).reshape(TD * DB, 512)
        hs, g_new = _k3s(hs, mixab, mixcd, p_sample[i].transpose(1, 0, 2).reshape(TD * DB, -1), w_out_b[i],
                         row(norm_ffn, i), w_gate_b[i], w_up_b[i], conv_ffn[i], w_down_b[i], row(norm_ple, i),
                         w_pg_b[i], w_ple_b[i], gfin, sffn, DB=DB, TD=TD, final_norm=(i == depth - 1))

        def bm(a, width):
            return a[:, :width].reshape(TD, DB, width).transpose(1, 0, 2)

        new_conv = u_new[TD - 2:].transpose(1, 0, 2)
        new_pool = jnp.concatenate([state_pool[i], pv_new.transpose(1, 0, 2)], axis=1)[:, -POOL_BUF:]
        st_s.append((new_conv, new_pool, g_new.transpose(1, 0, 2), bm(skk, 64), bm(svw, 64), bm(sii, 64),
                     bm(dk, 256).reshape(DB, TD, DIFF_HEADS, 64), bm(dv, 256).reshape(DB, TD, DIFF_HEADS, 64)))
    y_sample = hs.reshape(TD, DB, D).transpose(1, 0, 2)

    def col(outs, j, axis):
        return jnp.stack([o[j] for o in outs], axis=axis)

    return (y_prompt, y_sample, col(st_p, 0, 0), col(st_s, 0, 0), col(st_p, 1, 0), col(st_s, 1, 0),
            col(st_p, 2, 0), col(st_s, 2, 0), c_k_p, col(st_s, 3, 1), c_v_p, col(st_s, 4, 1),
            c_idx_p, col(st_s, 5, 1), d_k_p, col(st_s, 6, 1), d_v_p, col(st_s, 7, 1))
```

```python
import functools
import math

import numpy as np
import jax
import jax.numpy as jnp
from jax import lax
from jax.experimental import pallas as pl
from jax.experimental.pallas import tpu as pltpu

F32 = jnp.float32
BF16 = jnp.bfloat16
I32 = jnp.int32

EPS = 1e-6
GROUP_W = 256
POOL_WINDOWS = (2, 4, 8, 16)
POOL_BUF = 15
DSA_HEADS = 4
DSA_HD = 64
IDX_HEADS = 8
IDX_HD = 64
IDX_SCALE = IDX_HEADS ** -0.5 * IDX_HD ** -0.5
TOPK_MAX = 256
DIFF_HEADS = 4
DIFF_VD = 64
DIFF_QD = 32
SPLIT_SIZES = (256, 256, 256, 256, 256, 64, 64, 512, 8, 64, 256, 256, 256)

LANES = 128
SUBLANES = 8
VMEM_LIMIT = 56 * 1024 * 1024

C_GB, C_GC, C_GH, C_PV = 0, 256, 512, 768
C_CQ, C_CQI, C_DQ, C_DK, C_DV = 1024, 1280, 1792, 2048, 2304
C_KK, C_II, C_VW = 2560, 2688, 2816
D_IN_P = 2944

KEY_BLK = 256
Q_BLK = 128
V_AUG = 80
LOG2E = math.log2(math.e)
NEG = -1e30
INT_MIN = -2 ** 31

NT_DIMS = (((1,), (1,)), ((), ()))


def _dot(a, b):
    return jnp.dot(a, b, preferred_element_type=F32)


def _dot_nt(a, b):
    return lax.dot_general(a, b, NT_DIMS, preferred_element_type=F32)


def _rms(x, g):
    ms = jnp.mean(x * x, axis=-1, keepdims=True)
    return (x * lax.rsqrt(ms + EPS)) * g


def _sigmoid(x):
    return 1.0 / (1.0 + jnp.exp(-x))


def _float_key(x):
    b = lax.bitcast_convert_type(x, I32)
    return b ^ (lax.shift_right_arithmetic(b, 31) & 0x7FFFFFFF)


def _lambda_full(lam_ref, lam_init):
    lp = lam_ref[...]
    s1 = jnp.sum(lp[0:1] * lp[1:2], axis=-1, keepdims=True)
    s2 = jnp.sum(lp[2:3] * lp[3:4], axis=-1, keepdims=True)
    return jnp.exp(s1) - jnp.exp(s2) + lam_init


def _const_spec(shape):
    nd = len(shape)
    return pl.BlockSpec(shape, lambda *_: (0,) * nd)


def _k1p_kernel(x_ref, g_ref, w_ref, ca_ref, pm_ref, ps_ref, _a0, _a1, _a2, _a3, _a4,
                mixab_ref, cq_ref, cqi_ref, dq_ref,
                kkb_ref, iib_ref, dkb_ref, cvT_ref, cwiT_ref, dvT_ref, convst_ref, poolst_ref,
                ckT_ref, cvTf_ref, ckiT_ref, dkT_ref, dvTf_ref,
                extu_ref, extx_ref, *, TS):
    t = pl.program_id(1)

    @pl.when(t == 0)
    def _():
        extu_ref[0:8, :] = jnp.zeros((8, GROUP_W), F32)
        extx_ref[0:16, :] = jnp.zeros((16, GROUP_W), F32)

    xn = _rms(x_ref[0], g_ref[...])
    z = _dot(xn.astype(BF16), w_ref[...])

    u = z[:, C_GC:C_GC + 256] * z[:, C_GH:C_GH + 256]
    extu_ref[8:8 + TS, :] = u
    p1 = extu_ref[7:7 + TS, :]
    p2 = extu_ref[6:6 + TS, :]
    ca = ca_ref[...]
    conv = p2 * ca[0:1] + p1 * ca[1:2] + u * ca[2:3]
    mixab_ref[0, :, 0:256] = z[:, C_GB:C_GB + 256] * conv
    extu_ref[0:8, :] = u[TS - 8:TS]
    convst_ref[0] = u[TS - 8:TS]

    pv = z[:, C_PV:C_PV + 256]
    extx_ref[16:16 + TS, :] = pv

    def sh(j, c0):
        return extx_ref[16 - j:16 - j + TS, c0:c0 + LANES]

    s2 = sh(0, 0) + sh(1, 0)
    s4 = s2 + sh(2, 0) + sh(3, 0)
    s8 = sh(0, LANES)
    for j in range(1, 8):
        s8 = s8 + sh(j, LANES)
    s16 = s8
    for j in range(8, 16):
        s16 = s16 + sh(j, LANES)
    lo = lax.broadcasted_iota(I32, (TS, LANES), 1) < 64
    posp1 = (t * TS + 1 + lax.broadcasted_iota(I32, (TS, LANES), 0)).astype(F32)
    d0 = jnp.where(lo, s2, s4) / jnp.where(lo, jnp.minimum(posp1, 2.0), jnp.minimum(posp1, 4.0)) - pv[:, 0:LANES]
    d1 = jnp.where(lo, s8, s16) / jnp.where(lo, jnp.minimum(posp1, 8.0), jnp.minimum(posp1, 16.0)) - pv[:, LANES:]
    d = jnp.concatenate([d0, d1], axis=1)
    mixab_ref[0, :, 256:512] = _dot(d.astype(BF16), pm_ref[...]) * ps_ref[...]
    extx_ref[0:16, :] = pv[TS - 16:TS]
    poolst_ref[0] = pv[TS - 16:TS]

    cq_ref[0] = z[:, C_CQ:C_CQ + 256].astype(BF16)
    cqi_ref[0] = z[:, C_CQI:C_CQI + 512].astype(BF16)
    dq_ref[0] = z[:, C_DQ:C_DQ + 256].astype(BF16)
    dk = z[:, C_DK:C_DK + 256]
    dv = z[:, C_DV:C_DV + 256]
    skk = z[:, C_KK:C_KK + LANES]
    sii = z[:, C_II:C_II + LANES]
    svw = z[:, C_VW:C_VW + LANES]
    nb = TS // KEY_BLK
    kkb_ref[0] = skk.astype(BF16).reshape(nb, KEY_BLK, LANES)
    iib_ref[0] = sii.astype(BF16).reshape(nb, KEY_BLK, LANES)
    dkb_ref[0] = dk.astype(BF16).reshape(nb, KEY_BLK, 256)
    svw_t = svw.T
    dv_t = dv.T
    ckT_ref[0, 0] = skk.T[0:64]
    ckiT_ref[0, 0] = sii.T[0:64]
    cvTf_ref[0, 0] = svw_t[0:64]
    dkT_ref[0, 0] = dk.T
    dvTf_ref[0, 0] = dv_t
    cwiT_ref[0] = svw_t[64:72]
    ones_rows = jnp.where(lax.broadcasted_iota(I32, (V_AUG - 64, KEY_BLK), 0) == 0, 1.0, 0.0).astype(BF16)
    for j in range(nb):
        cvT_ref[0, j] = jnp.concatenate(
            [svw_t[0:64, j * KEY_BLK:(j + 1) * KEY_BLK].astype(BF16), ones_rows], axis=0)
        for h in range(DIFF_HEADS):
            dvT_ref[0, j, h] = jnp.concatenate(
                [dv_t[h * 64:(h + 1) * 64, j * KEY_BLK:(j + 1) * KEY_BLK].astype(BF16), ones_rows], axis=0)


def _k1p(x, g, w, ca, pm, ps, kv_out, *, TS, layer):
    B, T, D = x.shape
    nt = T // TS
    nb = TS // KEY_BLK
    nkb = T // KEY_BLK

    def row_spec(c):
        return pl.BlockSpec((1, TS, c), lambda b, t: (b, t, 0))

    def blk_spec(r, c):
        return pl.BlockSpec((1, nb, r, c), lambda b, t: (b, t, 0, 0))

    out_shape = (
        jax.ShapeDtypeStruct((B, T, 512), F32),
        jax.ShapeDtypeStruct((B, T, 256), BF16),
        jax.ShapeDtypeStruct((B, T, 512), BF16),
        jax.ShapeDtypeStruct((B, T, 256), BF16),
        jax.ShapeDtypeStruct((B, nkb, KEY_BLK, LANES), BF16),
        jax.ShapeDtypeStruct((B, nkb, KEY_BLK, LANES), BF16),
        jax.ShapeDtypeStruct((B, nkb, KEY_BLK, 256), BF16),
        jax.ShapeDtypeStruct((B, nkb, V_AUG, KEY_BLK), BF16),
        jax.ShapeDtypeStruct((B, 8, T), F32),
        jax.ShapeDtypeStruct((B, nkb, DIFF_HEADS, V_AUG, KEY_BLK), BF16),
        jax.ShapeDtypeStruct((B, 8, 256), F32),
        jax.ShapeDtypeStruct((B, 16, 256), F32),
    ) + tuple(jax.ShapeDtypeStruct(a.shape, a.dtype) for a in kv_out)

    def kv_spec(a):
        return pl.BlockSpec((1, 1, a.shape[2], TS), lambda b, t: (b, layer, 0, t))

    out_specs = (
        row_spec(512), row_spec(256), row_spec(512), row_spec(256),
        blk_spec(KEY_BLK, LANES), blk_spec(KEY_BLK, LANES), blk_spec(KEY_BLK, 256),
        blk_spec(V_AUG, KEY_BLK),
        pl.BlockSpec((1, 8, TS), lambda b, t: (b, 0, t)),
        pl.BlockSpec((1, nb, DIFF_HEADS, V_AUG, KEY_BLK), lambda b, t: (b, t, 0, 0, 0)),
        pl.BlockSpec((1, 8, 256), lambda b, t: (b, 0, 0)),
        pl.BlockSpec((1, 16, 256), lambda b, t: (b, 0, 0)),
    ) + tuple(kv_spec(a) for a in kv_out)
    in_specs = [
        row_spec(D), _const_spec(g.shape), _const_spec(w.shape), _const_spec(ca.shape),
        _const_spec(pm.shape), _const_spec(ps.shape),
    ] + [pl.BlockSpec(memory_space=pl.ANY)] * len(kv_out)
    n_in, n_out = 6, len(out_shape) - len(kv_out)
    return pl.pallas_call(
        functools.partial(_k1p_kernel, TS=TS),
        grid=(B, nt), in_specs=in_specs, out_specs=out_specs, out_shape=out_shape,
        input_output_aliases={n_in + k: n_out + k for k in range(len(kv_out))},
        scratch_shapes=[pltpu.VMEM((TS + 8, GROUP_W), F32), pltpu.VMEM((TS + 16, GROUP_W), F32)],
        compiler_params=pltpu.CompilerParams(dimension_semantics=("arbitrary", "arbitrary"),
                                             vmem_limit_bytes=VMEM_LIMIT),
        name="mixer_in_prompt",
    )(x, g, w, ca, pm, ps, *kv_out)


def _k2p_kernel(cq_ref, cqi_ref, cwiT_ref, dq_ref, kkb_ref, iib_ref, cvT_ref, dkb_ref, dvT_ref,
                lam_ref, subln_ref, out_ref, key_ref, lgc_ref, lgd_ref, accc_ref, accd_ref, mxc_ref, thr_ref,
                *, n_sel, lam_init):
    qb = pl.program_id(1)
    q0 = qb * Q_BLK
    nkb = (q0 + Q_BLK + KEY_BLK - 1) // KEY_BLK
    qpos = q0 + lax.broadcasted_iota(I32, (KEY_BLK, Q_BLK), 1)
    krow = lax.broadcasted_iota(I32, (KEY_BLK, Q_BLK), 0)
    lane_q = lax.broadcasted_iota(I32, (Q_BLK, LANES), 1)
    lo_half = lane_q < 64
    NG = KEY_BLK // SUBLANES

    def head_pair(tile):
        zero = jnp.zeros_like(tile)
        return jnp.concatenate([jnp.where(lo_half, tile, zero), jnp.where(lo_half, zero, tile)], axis=0)

    cqi = cqi_ref[0]
    wT = cwiT_ref[0]
    qi = [head_pair(cqi[:, j * LANES:(j + 1) * LANES]) for j in range(IDX_HEADS // 2)]
    cq = cq_ref[0] * (DSA_HD ** -0.5)
    qc = [head_pair(cq[:, j * LANES:(j + 1) * LANES]) for j in range(DSA_HEADS // 2)]
    dq = dq_ref[0]
    lane256 = lax.broadcasted_iota(I32, (Q_BLK, 256), 1)

    def map_rows(h, mm):
        c0 = h * 64 + mm * DIFF_QD
        return jnp.where(jnp.logical_and(lane256 >= c0, lane256 < c0 + DIFF_QD), dq, jnp.zeros_like(dq))

    qd = [jnp.concatenate([map_rows(h, 0), map_rows(h, 1)], axis=0) for h in range(DIFF_HEADS)]
    c2 = (DIFF_QD ** -0.5) * LOG2E

    def pass_a(kb, mx, masked):
        ki = iib_ref[0, kb]
        acc = jnp.zeros((KEY_BLK, Q_BLK), F32)
        for j in range(IDX_HEADS // 2):
            s = _dot_nt(ki, qi[j])
            acc = acc + jnp.maximum(s[:, :Q_BLK], 0.0) * wT[2 * j:2 * j + 1]
            acc = acc + jnp.maximum(s[:, Q_BLK:], 0.0) * wT[2 * j + 1:2 * j + 2]
        key = _float_key(acc * IDX_SCALE + 0.0)
        if masked:
            causal = (kb * KEY_BLK + krow) <= qpos
            key = jnp.where(causal, key, INT_MIN)
        key_ref[kb] = key
        kk = kkb_ref[0, kb]
        for j in range(DSA_HEADS // 2):
            lgc_ref[j, kb] = _dot_nt(kk, qc[j])
        kd = dkb_ref[0, kb]
        new = list(mx)
        for h in range(DIFF_HEADS):
            s = _dot_nt(kd, qd[h]) * c2
            if masked:
                s = jnp.where(jnp.concatenate([causal, causal], axis=1), s, NEG)
            lgd_ref[h, kb] = s
            new[h] = jnp.maximum(new[h], jnp.max(s.reshape(NG, SUBLANES, 2 * Q_BLK), axis=0))
        return tuple(new)

    n_open = nkb - 1
    mx = lax.fori_loop(0, n_open // 2, lambda i, m: pass_a(2 * i + 1, pass_a(2 * i, m, False), False),
                       tuple(jnp.full((SUBLANES, 2 * Q_BLK), NEG, F32) for _ in range(DIFF_HEADS)))
    mx = lax.cond(n_open % 2 == 1, lambda m: pass_a(n_open - 1, m, False), lambda m: m, mx)
    mx = pass_a(nkb - 1, mx, True)

    @pl.when(nkb % 2 == 1)
    def _():
        key_ref[nkb] = jnp.full((KEY_BLK, Q_BLK), INT_MIN, I32)

    def count(pred):
        def body(i, acc):
            for u in range(2):
                kb = 2 * i + u
                c = jnp.where(pred(kb, key_ref[kb]), 1, 0)
                acc = acc + jnp.sum(c.reshape(NG, SUBLANES, Q_BLK), axis=0)
            return acc
        acc = lax.fori_loop(0, (nkb + 1) // 2, body, jnp.zeros((SUBLANES, Q_BLK), I32))
        return jnp.sum(acc, axis=0, keepdims=True)

    def search(npairs):
        def step(it, base):
            cand = base + lax.shift_left(jnp.int32(1), 31 - it)
            acc = jnp.zeros((SUBLANES, Q_BLK), I32)
            for kb in range(2 * npairs):
                acc = acc + jnp.sum(jnp.where(key_ref[kb] >= cand, 1, 0).reshape(NG, SUBLANES, Q_BLK), axis=0)
            cnt = jnp.sum(acc, axis=0, keepdims=True)
            return jnp.where(cnt >= n_sel, cand, base)
        return lax.fori_loop(0, 32, step, jnp.full((1, Q_BLK), INT_MIN, I32))

    thr_ref[...] = jnp.full((1, Q_BLK), INT_MIN, I32)
    for npairs in range(1, key_ref.shape[0] // 2 + 1):
        @pl.when(jnp.logical_and(q0 + Q_BLK > n_sel, (nkb + 1) // 2 == npairs))
        def _(npairs=npairs):
            thr_ref[...] = search(npairs)

    thr = thr_ref[...]

    cnt_gt = count(lambda kb, k: k > thr)
    cnt_eq = count(lambda kb, k: k == thr)
    need = n_sel - cnt_gt
    excess = jnp.logical_and(cnt_eq > need, thr > INT_MIN)
    any_excess = jnp.max(jnp.where(excess, 1, 0))

    @pl.when(any_excess > 0)
    def _():
        def tie_body(it, jp):
            c = jp + lax.shift_left(jnp.int32(1), 10 - it)
            f = count(lambda kb, k: jnp.logical_and(k == thr, (kb * KEY_BLK + krow) < c))
            return jnp.where(f < need, c, jp)
        jp = lax.fori_loop(0, 11, tie_body, jnp.zeros((1, Q_BLK), I32))

        def demote(kb, carry):
            k = key_ref[kb]
            drop = jnp.logical_and(jnp.logical_and(k == thr, (kb * KEY_BLK + krow) > jp), excess)
            key_ref[kb] = jnp.where(drop, thr - 1, k)
            return carry
        lax.fori_loop(0, nkb, demote, 0)

    thr_sel = jnp.maximum(thr, INT_MIN + 1)

    def masked_c(j, kb):
        sel = key_ref[kb] >= thr_sel
        lg = lgc_ref[j, kb]
        return jnp.concatenate([jnp.where(sel, lg[:, :Q_BLK], NEG), jnp.where(sel, lg[:, Q_BLK:], NEG)], axis=1)

    md = [jnp.max(m, axis=0, keepdims=True) for m in mx]
    accd_ref[...] = jnp.zeros(accd_ref.shape, F32)
    accc_ref[...] = jnp.zeros(accc_ref.shape, F32)
    mxc_ref[...] = jnp.full(mxc_ref.shape, NEG, F32)

    def over_blocks(step):
        def quad(i, carry):
            step(tuple(4 * i + u for u in range(4)))
            return carry
        lax.fori_loop(0, nkb // 4, quad, 0)
        rest = (nkb // 4) * 4

        @pl.when(nkb % 4 >= 2)
        def _():
            step((rest, rest + 1))

        @pl.when(nkb % 2 == 1)
        def _():
            step((nkb - 1,))

    def pass_b(kbs):
        for h in range(DIFF_HEADS):
            p = jnp.concatenate([jnp.exp2(lgd_ref[h, kb] - md[h]).astype(BF16) for kb in kbs], axis=0)
            vT = jnp.concatenate([dvT_ref[0, kb, h] for kb in kbs], axis=1)
            accd_ref[h] += _dot(vT, p)
        vT = jnp.concatenate([cvT_ref[0, kb] for kb in kbs], axis=1)
        for j in range(DSA_HEADS // 2):
            lgs = [masked_c(j, kb) for kb in kbs]
            g = jnp.max(lgs[0].reshape(NG, SUBLANES, 2 * Q_BLK), axis=0)
            for lg in lgs[1:]:
                g = jnp.maximum(g, jnp.max(lg.reshape(NG, SUBLANES, 2 * Q_BLK), axis=0))
            m_old = mxc_ref[j, 0:1]
            m_new = jnp.maximum(m_old, jnp.max(g, axis=0, keepdims=True))
            p = jnp.concatenate([jnp.exp(lg - m_new).astype(BF16) for lg in lgs], axis=0)
            accc_ref[j] = accc_ref[j] * jnp.exp(m_old - m_new) + _dot(vT, p)
            mxc_ref[j] = jnp.broadcast_to(m_new, (SUBLANES, 2 * Q_BLK))

    over_blocks(pass_b)
    outs = []
    for h in range(DSA_HEADS):
        a = accc_ref[h // 2, :, (h % 2) * Q_BLK:(h % 2 + 1) * Q_BLK]
        outs.append(a[0:DSA_HD] / a[DSA_HD:DSA_HD + 1])
    out_ref[0, :, 0:256] = jnp.concatenate(outs, axis=0).T

    lam = _lambda_full(lam_ref, lam_init)
    subln = subln_ref[...]
    outs = []
    for h in range(DIFF_HEADS):
        a1 = accd_ref[h, :, 0:Q_BLK]
        a2 = accd_ref[h, :, Q_BLK:2 * Q_BLK]
        o = a1[0:DIFF_VD] / a1[DIFF_VD:DIFF_VD + 1] - lam * (a2[0:DIFF_VD] / a2[DIFF_VD:DIFF_VD + 1])
        ms = jnp.mean(o * o, axis=0, keepdims=True)
        outs.append((o * lax.rsqrt(ms + EPS)) * subln * (1.0 - lam_init))
    out_ref[0, :, 256:512] = jnp.concatenate(outs, axis=0).T


def _k2p(cq, cqi, cwiT, dq, kkb, iib, cvT, dkb, dvT, lam, subln, *, n_sel, lam_init):
    B, T, _ = cq.shape
    nq = T // Q_BLK
    nkb = T // KEY_BLK

    def q_spec(c):
        return pl.BlockSpec((1, Q_BLK, c), lambda b, q: (b, q, 0))

    def seq_spec(r, c):
        return pl.BlockSpec((1, nkb, r, c), lambda b, q: (b, 0, 0, 0))

    in_specs = [
        q_spec(256), q_spec(512), pl.BlockSpec((1, 8, Q_BLK), lambda b, q: (b, 0, q)), q_spec(256),
        seq_spec(KEY_BLK, LANES), seq_spec(KEY_BLK, LANES), seq_spec(V_AUG, KEY_BLK),
        seq_spec(KEY_BLK, 256),
        pl.BlockSpec((1, nkb, DIFF_HEADS, V_AUG, KEY_BLK), lambda b, q: (b, 0, 0, 0, 0)),
        _const_spec(lam.shape), _const_spec(subln.shape),
    ]
    return pl.pallas_call(
        functools.partial(_k2p_kernel, n_sel=n_sel, lam_init=lam_init),
        grid=(B, nq), in_specs=in_specs, out_specs=q_spec(512),
        out_shape=jax.ShapeDtypeStruct((B, T, 512), F32),
        scratch_shapes=[pltpu.VMEM((nkb + 1, KEY_BLK, Q_BLK), I32),
                        pltpu.VMEM((DSA_HEADS // 2, nkb, KEY_BLK, 2 * Q_BLK), F32),
                        pltpu.VMEM((DIFF_HEADS, nkb, KEY_BLK, 2 * Q_BLK), F32),
                        pltpu.VMEM((DSA_HEADS // 2, V_AUG, 2 * Q_BLK), F32),
                        pltpu.VMEM((DIFF_HEADS, V_AUG, 2 * Q_BLK), F32),
                        pltpu.VMEM((DSA_HEADS // 2, SUBLANES, 2 * Q_BLK), F32),
                        pltpu.VMEM((1, Q_BLK), I32)],
        compiler_params=pltpu.CompilerParams(dimension_semantics=("arbitrary", "arbitrary"),
                                             vmem_limit_bytes=VMEM_LIMIT),
        name="attention_prompt",
    )(cq, cqi, cwiT, dq, kkb, iib, cvT, dkb, dvT, lam, subln)


def _ple_tail(h2, p_bf, gple, wpg, wple, gfin, final_norm):
    gate = _sigmoid(_dot(_rms(h2, gple).astype(BF16), wpg))
    h3 = h2 + _dot(p_bf, wple) * gate
    if final_norm:
        h3 = _rms(h3, gfin)
    return h3


def _k3p_kernel(h_ref, ab_ref, cd_ref, p_ref, wout_ref, gffn_ref, wg_ref, wu_ref, cf_ref, wd_ref,
                gple_ref, wpg_ref, wple_ref, gfin_ref, hout_ref, ffnst_ref, extg_ref, *, TS, final_norm):
    t = pl.program_id(1)

    @pl.when(t == 0)
    def _():
        extg_ref[0:8, :] = jnp.zeros((8, extg_ref.shape[1]), F32)

    mix = jnp.concatenate([ab_ref[0], cd_ref[0]], axis=1).astype(BF16)
    h1 = h_ref[0] + _dot(mix, wout_ref[...])
    xn2 = _rms(h1, gffn_ref[...]).astype(BF16)
    g = _dot(xn2, wg_ref[...])
    extg_ref[8:8 + TS, :] = g
    p1 = extg_ref[7:7 + TS, :]
    p2 = extg_ref[6:6 + TS, :]
    cf = cf_ref[...]
    gc = p2 * cf[0:1] + p1 * cf[1:2] + g * cf[2:3]
    extg_ref[0:8, :] = g[TS - 8:TS]
    ffnst_ref[0] = g[TS - 8:TS]
    act = (gc * _sigmoid(gc)) * _dot(xn2, wu_ref[...])
    h2 = h1 + _dot(act.astype(BF16), wd_ref[...])
    hout_ref[0] = _ple_tail(h2, p_ref[0].astype(BF16), gple_ref[...], wpg_ref[...], wple_ref[...],
                            gfin_ref[...], final_norm)


def _k3p(h, ab, cd, p, wout, gffn, wg, wu, cf, wd, gple, wpg, wple, gfin, *, TS, final_norm):
    B, T, D = h.shape
    dff = wg.shape[1]
    nt = T // TS

    def row_spec(c):
        return pl.BlockSpec((1, TS, c), lambda b, t: (b, t, 0))

    def w_spec(a):
        nd = a.ndim
        return pl.BlockSpec(a.shape, lambda *_: (0,) * nd, pipeline_mode=pl.Buffered(1))

    in_specs = [row_spec(D), row_spec(512), row_spec(512), row_spec(p.shape[2]),
                w_spec(wout), w_spec(gffn), w_spec(wg), w_spec(wu), w_spec(cf), w_spec(wd),
                w_spec(gple), w_spec(wpg), w_spec(wple), w_spec(gfin)]
    out_specs = (row_spec(D), pl.BlockSpec((1, 8, dff), lambda b, t: (b, 0, 0)))
    out_shape = (jax.ShapeDtypeStruct((B, T, D), F32), jax.ShapeDtypeStruct((B, 8, dff), F32))
    return pl.pallas_call(
        functools.partial(_k3p_kernel, TS=TS, final_norm=final_norm),
        grid=(B, nt), in_specs=in_specs, out_specs=out_specs, out_shape=out_shape,
        scratch_shapes=[pltpu.VMEM((TS + 8, dff), F32)],
        compiler_params=pltpu.CompilerParams(dimension_semantics=("arbitrary", "arbitrary"),
                                             vmem_limit_bytes=VMEM_LIMIT),
        name="channel_mix_prompt",
    )(h, ab, cd, p, wout, gffn, wg, wu, cf, wd, gple, wpg, wple, gfin)


def _k1s_kernel(x_ref, g_ref, w_ref, ca_ref, pm_ref, ps_ref, sconv_ref, spool_ref,
                mixab_ref, cq_ref, cqi_ref, dq_ref, dk_ref, dv_ref, skk_ref, sii_ref, svw_ref,
                u_ref, pv_ref, *, DB, TD, pos0):
    xn = _rms(x_ref[...], g_ref[...])
    z = _dot(xn.astype(BF16), w_ref[...])

    def slab(t, c0, c1):
        return z[t * DB:(t + 1) * DB, c0:c1]

    ca = ca_ref[...]
    us = [slab(t, C_GC, C_GC + 256) * slab(t, C_GH, C_GH + 256) for t in range(TD)]
    extu = [sconv_ref[0], sconv_ref[1]] + us
    pvs = [slab(t, C_PV, C_PV + 256) for t in range(TD)]
    extx = [spool_ref[j] for j in range(POOL_BUF)] + pvs
    lo = lax.broadcasted_iota(I32, (DB, LANES), 1) < 64
    pm = pm_ref[...]
    ps = ps_ref[...]
    for t in range(TD):
        conv = extu[t] * ca[0:1] + extu[t + 1] * ca[1:2] + extu[t + 2] * ca[2:3]
        mixab_ref[t * DB:(t + 1) * DB, 0:256] = slab(t, C_GB, C_GB + 256) * conv
        u_ref[t] = us[t]
        pv_ref[t] = pvs[t]
        e = POOL_BUF + t

        def win(n, c0):
            s = extx[e][:, c0:c0 + LANES]
            for j in range(1, n):
                s = s + extx[e - j][:, c0:c0 + LANES]
            return s

        cnt = [float(min(w, pos0 + t + 1)) for w in POOL_WINDOWS]
        d0 = jnp.where(lo, win(2, 0) / cnt[0], win(4, 0) / cnt[1]) - pvs[t][:, 0:LANES]
        d1 = jnp.where(lo, win(8, LANES) / cnt[2], win(16, LANES) / cnt[3]) - pvs[t][:, LANES:]
        d = jnp.concatenate([d0, d1], axis=1)
        mixab_ref[t * DB:(t + 1) * DB, 256:512] = _dot(d.astype(BF16), pm) * ps

    cq_ref[...] = z[:, C_CQ:C_CQ + 256].astype(BF16)
    cqi_ref[...] = z[:, C_CQI:C_CQI + 512].astype(BF16)
    dq_ref[...] = z[:, C_DQ:C_DQ + 256].astype(BF16)
    dk_ref[...] = z[:, C_DK:C_DK + 256]
    dv_ref[...] = z[:, C_DV:C_DV + 256]
    skk_ref[...] = z[:, C_KK:C_KK + LANES]
    sii_ref[...] = z[:, C_II:C_II + LANES]
    svw_ref[...] = z[:, C_VW:C_VW + LANES]


def _k1s(x, g, w, ca, pm, ps, sconv, spool, *, DB, TD, pos0):
    R = x.shape[0]
    out_shape = (
        jax.ShapeDtypeStruct((R, 512), F32), jax.ShapeDtypeStruct((R, 256), BF16),
        jax.ShapeDtypeStruct((R, 512), BF16), jax.ShapeDtypeStruct((R, 256), BF16),
        jax.ShapeDtypeStruct((R, 256), F32), jax.ShapeDtypeStruct((R, 256), F32),
        jax.ShapeDtypeStruct((R, LANES), F32), jax.ShapeDtypeStruct((R, LANES), F32),
        jax.ShapeDtypeStruct((R, LANES), F32),
        jax.ShapeDtypeStruct((TD, DB, 256), F32), jax.ShapeDtypeStruct((TD, DB, 256), F32),
    )
    args = (x, g, w, ca, pm, ps, sconv, spool)
    return pl.pallas_call(
        functools.partial(_k1s_kernel, DB=DB, TD=TD, pos0=pos0),
        grid=(1,), in_specs=[_const_spec(a.shape) for a in args],
        out_specs=tuple(_const_spec(s.shape) for s in out_shape), out_shape=out_shape,
        compiler_params=pltpu.CompilerParams(dimension_semantics=("arbitrary",), vmem_limit_bytes=VMEM_LIMIT),
        name="mixer_in_sample",
    )(*args)


QP = 8


def _wrap32(v):
    return ((v + 2 ** 31) % 2 ** 32) - 2 ** 31


def _k2s_kernel(pt_ref, qi_ref, wi_ref, qc_ref, qd_ref, kin_ref, kn_ref, vn_ref, dkn_ref, dvn_ref,
                lam_ref, subln_ref, cidx_hbm, ck_hbm, cv_hbm, dk_hbm, dv_hbm,
                out_ref, key_ref, clog_ref, dlog_ref, idx_refs, k_refs, v_refs, dk_refs, dv_refs, sem,
                *, NP, PS, TD, n_sel, lam_init, layer):
    G = QP // TD
    n = G * NP
    SC = (NP + 1) * PS

    i = pl.program_id(0)
    slot = lax.rem(i, 2)
    hbms = (cidx_hbm, ck_hbm, cv_hbm, dk_hbm, dv_hbm)
    bufs = (idx_refs, k_refs, v_refs, dk_refs, dv_refs)

    def page_copy(hbm, buf, page, sl, gp):
        return pltpu.make_async_copy(hbm.at[page, layer], buf.at[sl, gp], sem.at[sl])

    def issue(step, sl):
        def body(gp, carry):
            page = pt_ref[G * step + gp // NP, gp % NP]
            for hbm, buf in zip(hbms, bufs):
                page_copy(hbm, buf, page, sl, gp).start()
            return carry
        lax.fori_loop(0, n, body, 0)

    @pl.when(i == 0)
    def _():
        issue(0, 0)

    @pl.when(i + 1 < pl.num_programs(0))
    def _():
        issue(i + 1, 1 - slot)

    def wait_body(gp, carry):
        for hbm, buf in zip(hbms, bufs):
            page_copy(hbm, buf, 0, slot, gp).wait()
        return carry

    lax.fori_loop(0, n, wait_body, 0)

    def own_half(x):
        elem0 = (lax.broadcasted_iota(I32, (x.shape[0], PS), 0) % QP) < TD
        return jnp.where(elem0, x[:, :PS], x[:, PS:])

    def split_half(x):
        elem0 = (lax.broadcasted_iota(I32, x.shape, 0) % QP) < TD
        zero = jnp.zeros_like(x)
        return jnp.concatenate([jnp.where(elem0, x, zero), jnp.where(elem0, zero, x)], axis=1)

    def pages(refs, new_ref, p, rows):
        tiles = [(refs[slot, g * NP + p] if p < NP else new_ref[0, g]).reshape(rows, PS).astype(BF16)
                 for g in range(G)]
        return jnp.concatenate(tiles, axis=1)

    qi = qi_ref[0]
    wi = wi_ref[0]
    col = lax.broadcasted_iota(I32, (QP, PS), 1)
    qt = lax.broadcasted_iota(I32, (QP, PS), 0) % TD
    new_ok = jnp.logical_and(col < TD, col <= qt)
    for p in range(NP + 1):
        s = own_half(_dot(qi, pages(idx_refs, kin_ref, p, IDX_HD)))
        r = (jnp.maximum(s, 0.0) * wi).reshape(IDX_HEADS, QP, PS)
        score = jnp.sum(r, axis=0) * IDX_SCALE + 0.0
        key = _float_key(score)
        if p == NP:
            key = jnp.where(new_ok, key, INT_MIN)
        key_ref[:, p * PS:(p + 1) * PS] = key

    qc = qc_ref[0] * (DSA_HD ** -0.5)
    for p in range(NP + 1):
        clog_ref[:, p * PS:(p + 1) * PS] = own_half(_dot(qc, pages(k_refs, kn_ref, p, DSA_HD)))

    qd0 = qd_ref[0]
    rhead = lax.broadcasted_iota(I32, qd0.shape, 0) // QP
    lane = lax.broadcasted_iota(I32, qd0.shape, 1)
    zero = jnp.zeros_like(qd0)
    qd = jnp.concatenate(
        [jnp.where(jnp.logical_and(lane >= rhead * 64 + mm * DIFF_QD, lane < rhead * 64 + (mm + 1) * DIFF_QD),
                   qd0, zero) for mm in range(2)], axis=0)
    HD = DIFF_HEADS * DIFF_VD
    nrow = 2 * DIFF_HEADS * QP
    for p in range(NP + 1):
        s = own_half(_dot(qd, pages(dk_refs, dkn_ref, p, HD))) * (DIFF_QD ** -0.5)
        if p == NP:
            ncol = lax.broadcasted_iota(I32, (nrow, PS), 1)
            nqt = lax.broadcasted_iota(I32, (nrow, PS), 0) % TD
            s = jnp.where(jnp.logical_and(ncol < TD, ncol <= nqt), s, NEG)
        dlog_ref[:, p * PS:(p + 1) * PS] = s
    lgd = dlog_ref[...]
    md = jnp.max(lgd, axis=1, keepdims=True)
    ped = jnp.exp(lgd - md)
    ld = jnp.sum(ped, axis=1, keepdims=True)
    dlog_ref[...] = ped
    accd = jnp.zeros((nrow, HD), F32)
    for p in range(NP + 1):
        pp = split_half(dlog_ref[:, p * PS:(p + 1) * PS]).astype(BF16)
        accd = accd + _dot_nt(pp, pages(dv_refs, dvn_ref, p, HD))
    od = accd / ld
    half = DIFF_HEADS * QP
    lam = _lambda_full(lam_ref, lam_init)
    subln = subln_ref[...]
    for h in range(DIFF_HEADS):
        o = (od[h * QP:(h + 1) * QP, h * 64:(h + 1) * 64]
             - lam * od[half + h * QP:half + (h + 1) * QP, h * 64:(h + 1) * 64])
        ms = jnp.mean(o * o, axis=-1, keepdims=True)
        out_ref[0, :, 256 + h * 64:256 + (h + 1) * 64] = (o * lax.rsqrt(ms + EPS)) * subln * (1.0 - lam_init)

    kcol = lax.broadcasted_iota(I32, (QP, SC), 1)

    def count(pred):
        c = jnp.where(pred(key_ref[...]), 1.0, 0.0)
        return jnp.sum(c, axis=1, keepdims=True)

    thr = jnp.full((QP, 1), INT_MIN, I32)
    for sh, nbits in [(30, 2)] + [(27 - 3 * s, 3) for s in range(10)]:
        passed = jnp.zeros((QP, 1), I32)
        for j in range(1, 2 ** nbits):
            cnt = count(lambda k, c=thr + _wrap32(j << sh): k >= c)
            passed = passed + jnp.where(cnt >= n_sel, 1, 0)
        thr = thr + lax.shift_left(passed, sh)
    cnt_gt = count(lambda k: k > thr)
    cnt_eq = count(lambda k: k == thr)
    need = n_sel - cnt_gt
    excess = jnp.logical_and(cnt_eq > need, thr > INT_MIN)
    any_excess = jnp.max(jnp.where(excess, 1, 0))

    @pl.when(any_excess > 0)
    def _():
        def tie_body(it, jp):
            c = jp + lax.shift_left(jnp.int32(1), 12 - it)
            f = count(lambda k: jnp.logical_and(k == thr, kcol < c))
            return jnp.where(f < need, c, jp)
        jp = lax.fori_loop(0, 13, tie_body, jnp.zeros((QP, 1), I32))
        k = key_ref[...]
        drop = jnp.logical_and(jnp.logical_and(k == thr, kcol > jp), excess)
        key_ref[...] = jnp.where(drop, thr - 1, k)

    thr_sel = jnp.maximum(thr, INT_MIN + 1)
    sel = key_ref[...] >= thr_sel

    lg = jnp.where(sel[None], clog_ref[...].reshape(DSA_HEADS, QP, SC), NEG)
    lg = lg.reshape(DSA_HEADS * QP, SC)
    m = jnp.max(lg, axis=1, keepdims=True)
    pe = jnp.exp(lg - m)
    l = jnp.sum(pe, axis=1, keepdims=True)
    clog_ref[...] = pe
    acc = jnp.zeros((DSA_HEADS * QP, DSA_HD), F32)
    for p in range(NP + 1):
        pp = split_half(clog_ref[:, p * PS:(p + 1) * PS]).astype(BF16)
        acc = acc + _dot_nt(pp, pages(v_refs, vn_ref, p, DSA_HD))
    oc = acc / l
    for h in range(DSA_HEADS):
        out_ref[0, :, h * 64:(h + 1) * 64] = oc[h * QP:(h + 1) * QP]


def _k2s(page_table, qi, wi, qc, qd, kin, kn, vn, dkn, dvn, lam, subln,
         c_idx, c_k, c_v, d_k, d_v, *, layer, TD, n_sel, lam_init):
    DB, NP = page_table.shape
    PS = c_k.shape[3]
    G = QP // TD
    NS = DB // G

    def b_spec(a):
        return pl.BlockSpec((1,) + a.shape[1:], lambda b, pt: (b,) + (0,) * (a.ndim - 1))

    def c_spec(a):
        nd = a.ndim
        return pl.BlockSpec(a.shape, lambda b, pt: (0,) * nd)

    in_specs = [b_spec(qi), b_spec(wi), b_spec(qc), b_spec(qd), b_spec(kin), b_spec(kn), b_spec(vn),
                b_spec(dkn), b_spec(dvn), c_spec(lam), c_spec(subln)]
    in_specs += [pl.BlockSpec(memory_space=pl.ANY)] * 5
    SC = (NP + 1) * PS
    n = G * NP
    grid_spec = pltpu.PrefetchScalarGridSpec(
        num_scalar_prefetch=1, grid=(NS,), in_specs=in_specs,
        out_specs=pl.BlockSpec((1, QP, 512), lambda b, pt: (b, 0, 0)),
        scratch_shapes=[pltpu.VMEM((QP, SC), I32), pltpu.VMEM((DSA_HEADS * QP, SC), F32),
                        pltpu.VMEM((2 * DIFF_HEADS * QP, SC), F32)]
        + [pltpu.VMEM((2, n, 64, PS), F32)] * 3 + [pltpu.VMEM((2, n, DIFF_HEADS, 64, PS), F32)] * 2
        + [pltpu.SemaphoreType.DMA((2,))],
    )
    args = [page_table, qi, wi, qc, qd, kin, kn, vn, dkn, dvn, lam, subln, c_idx, c_k, c_v, d_k, d_v]
    return pl.pallas_call(
        functools.partial(_k2s_kernel, NP=NP, PS=PS, TD=TD, n_sel=n_sel, lam_init=lam_init, layer=layer),
        grid_spec=grid_spec, out_shape=jax.ShapeDtypeStruct((NS, QP, 512), F32),
        compiler_params=pltpu.CompilerParams(dimension_semantics=("arbitrary",), vmem_limit_bytes=VMEM_LIMIT),
        name="attention_sample",
    )(*args)


FF_TILE = 256


def _k3s_kernel(h_ref, ab_ref, cd_ref, p_ref, wout_ref, gffn_ref, wg_ref, wu_ref, cf_ref, wd_ref,
                gple_ref, wpg_ref, wple_ref, gfin_ref, sffn_ref, hout_ref, gout_ref,
                h1_ref, xn2_ref, acc_ref, *, DB, TD, final_norm):
    f = pl.program_id(0)

    @pl.when(f == 0)
    def _():
        mix = jnp.concatenate([ab_ref[...], cd_ref[...]], axis=1).astype(BF16)
        h1 = h_ref[...] + _dot(mix, wout_ref[...])
        h1_ref[...] = h1
        xn2_ref[...] = _rms(h1, gffn_ref[...]).astype(BF16)
        acc_ref[...] = jnp.zeros(acc_ref.shape, F32)

    xn2 = xn2_ref[...]
    g = _dot(xn2, wg_ref[...])
    cf = cf_ref[...]
    gs = [sffn_ref[0], sffn_ref[1]] + [g[t * DB:(t + 1) * DB] for t in range(TD)]
    gc = jnp.concatenate([gs[t] * cf[0:1] + gs[t + 1] * cf[1:2] + gs[t + 2] * cf[2:3] for t in range(TD)], axis=0)
    act = (gc * _sigmoid(gc)) * _dot(xn2, wu_ref[...])
    acc_ref[...] += _dot(act.astype(BF16), wd_ref[...])
    gout_ref[0] = gs[TD]
    gout_ref[1] = gs[TD + 1]

    @pl.when(f == pl.num_programs(0) - 1)
    def _():
        h2 = h1_ref[...] + acc_ref[...]
        hout_ref[...] = _ple_tail(h2, p_ref[...].astype(BF16), gple_ref[...], wpg_ref[...], wple_ref[...],
                                  gfin_ref[...], final_norm)


def _k3s(h, ab, cd, p, wout, gffn, wg, wu, cf, wd, gple, wpg, wple, gfin, sffn, *, DB, TD, final_norm):
    R, D = h.shape
    dff = wg.shape[1]
    nf = dff // FF_TILE
    cs = _const_spec
    in_specs = [cs(h.shape), cs(ab.shape), cs(cd.shape), cs(p.shape), cs(wout.shape), cs(gffn.shape),
                pl.BlockSpec((D, FF_TILE), lambda f: (0, f)), pl.BlockSpec((D, FF_TILE), lambda f: (0, f)),
                pl.BlockSpec((3, FF_TILE), lambda f: (0, f)), pl.BlockSpec((FF_TILE, D), lambda f: (f, 0)),
                cs(gple.shape), cs(wpg.shape), cs(wple.shape), cs(gfin.shape),
                pl.BlockSpec((2, DB, FF_TILE), lambda f: (0, 0, f))]
    out_specs = (cs((R, D)), pl.BlockSpec((2, DB, FF_TILE), lambda f: (0, 0, f)))
    out_shape = (jax.ShapeDtypeStruct((R, D), F32), jax.ShapeDtypeStruct((2, DB, dff), F32))
    return pl.pallas_call(
        functools.partial(_k3s_kernel, DB=DB, TD=TD, final_norm=final_norm),
        grid=(nf,), in_specs=in_specs, out_specs=out_specs, out_shape=out_shape,
        scratch_shapes=[pltpu.VMEM((R, D), F32), pltpu.VMEM((R, D), BF16), pltpu.VMEM((R, D), F32)],
        compiler_params=pltpu.CompilerParams(dimension_semantics=("arbitrary",), vmem_limit_bytes=VMEM_LIMIT),
        name="channel_mix_sample",
    )(h, ab, cd, p, wout, gffn, wg, wu, cf, wd, gple, wpg, wple, gfin, sffn)


def _prep_w_in(w_in):
    cuts = np.cumsum(SPLIT_SIZES)[:-1].tolist()
    gb, gc, gh, pv, cq, ck, cv, cqi, cwi, cki, dq, dk, dv = jnp.split(w_in, cuts, axis=-1)
    pad = jnp.zeros(w_in.shape[:-1] + (LANES - 64 - IDX_HEADS,), w_in.dtype)
    cols = [gb, gc, gh, pv, cq, cqi, dq, dk, dv, ck, ck, cki, cki, cv, cwi, pad]
    return jnp.concatenate(cols, axis=-1).astype(BF16)


def _block_diag_maps(pool_maps):
    depth, G, c, _ = pool_maps.shape
    out = jnp.zeros((depth, G * c, G * c), pool_maps.dtype)
    for g in range(G):
        out = out.at[:, g * c:(g + 1) * c, g * c:(g + 1) * c].set(pool_maps[:, g])
    return out.astype(BF16)


def _heads_rows(a, NS, G, TD, nh):
    a = a.reshape(TD, NS, G, nh, 64).transpose(1, 3, 2, 0, 4)
    return a.reshape(NS, nh * QP, 64)


def _new_page(a, NS, G, TD, PS):
    c = a.shape[1]
    a = a.astype(BF16).reshape(TD, NS, G, c).transpose(1, 2, 3, 0)
    return jnp.pad(a, ((0, 0), (0, 0), (0, 0), (0, PS - TD)))


def kernel(x_prompt, x_sample, state_conv_a, state_pool, state_ffn, cache_c_k, cache_c_v, cache_c_idx,
           cache_d_k, cache_d_v, page_table, p_prompt, p_sample, norm_mix, w_in, conv_a, pool_maps,
           pool_scale, diff_lambda, diff_subln, w_out, norm_ffn, w_gate, w_up, conv_ffn, w_down,
           norm_ple, w_ple_gate, w_ple, norm_final):
    depth = w_in.shape[0]
    B, T, D = x_prompt.shape
    DB, TD, _ = x_sample.shape
    NP = page_table.shape[1]
    PS = cache_c_k.shape[2]
    past = NP * PS
    dff = w_gate.shape[2]
    TS1 = min(512, T)
    TS3 = min(512, T)

    w_in_p = _prep_w_in(w_in)
    pm_bd = _block_diag_maps(pool_maps)
    w_out_b, w_gate_b, w_up_b, w_down_b = (w.astype(BF16) for w in (w_out, w_gate, w_up, w_down))
    w_pg_b, w_ple_b = w_ple_gate.astype(BF16), w_ple.astype(BF16)
    gfin = norm_final.reshape(1, D)
    c_idx_t, c_k_t, c_v_t = (c.transpose(0, 1, 3, 2) for c in (cache_c_idx, cache_c_k, cache_c_v))
    d_k_t, d_v_t = (c.transpose(0, 1, 3, 4, 2) for c in (cache_d_k, cache_d_v))
    assert QP % TD == 0 and DB % (QP // TD) == 0 and TD >= 2
    G = QP // TD
    NS = DB // G

    def row(a, i):
        return a[i].reshape(1, -1)

    h = x_prompt
    st_p = []
    n_sel_p = min(TOPK_MAX, T // 4)
    kv_p = tuple(jnp.zeros((B, depth, c, T), F32) for c in (64, 64, 64, 256, 256))
    for i in range(depth):
        lam_init = 0.8 - 0.6 * math.exp(-0.3 * i)
        res = _k1p(h, row(norm_mix, i), w_in_p[i], conv_a[i], pm_bd[i], row(pool_scale, i), kv_p,
                   TS=TS1, layer=i)
        (mixab, cq, cqi, dq, kkb, iib, dkb, cvT, cwiT, dvT, convst, poolst) = res[:12]
        kv_p = res[12:]
        mixcd = _k2p(cq, cqi, cwiT, dq, kkb, iib, cvT, dkb, dvT, diff_lambda[i], diff_subln[i].reshape(-1, 1),
                     n_sel=n_sel_p, lam_init=lam_init)
        h, ffnst = _k3p(h, mixab, mixcd, p_prompt[i], w_out_b[i], row(norm_ffn, i), w_gate_b[i], w_up_b[i],
                        conv_ffn[i], w_down_b[i], row(norm_ple, i), w_pg_b[i], w_ple_b[i], gfin,
                        TS=TS3, final_norm=(i == depth - 1))
        st_p.append((convst[:, 6:8], poolst[:, 1:16], ffnst[:, 6:8]))
    y_prompt = h
    c_k_p, c_v_p, c_idx_p = (a.transpose(0, 1, 3, 2) for a in kv_p[:3])
    d_k_p, d_v_p = (a.reshape(B, depth, DIFF_HEADS, 64, T).transpose(0, 1, 4, 2, 3) for a in kv_p[3:])

    hs = x_sample.transpose(1, 0, 2).reshape(TD * DB, D)
    st_s = []
    n_sel_s = min(TOPK_MAX, (past + TD) // 4)
    for i in range(depth):
        lam_init = 0.8 - 0.6 * math.exp(-0.3 * i)
        sconv = state_conv_a[i].transpose(1, 0, 2)
        spool = state_pool[i].transpose(1, 0, 2)
        sffn = state_ffn[i].transpose(1, 0, 2)
        (mixab, cq, cqi, dq, dk, dv, skk, sii, svw, u_new, pv_new) = _k1s(
            hs, row(norm_mix, i), w_in_p[i], conv_a[i], pm_bd[i], row(pool_scale, i), sconv, spool,
            DB=DB, TD=TD, pos0=past)
        qi = _heads_rows(cqi, NS, G, TD, IDX_HEADS)
        wi = svw[:, 64:64 + IDX_HEADS].reshape(TD, NS, G, IDX_HEADS).transpose(1, 3, 2, 0)
        wi = jnp.broadcast_to(wi.reshape(NS, IDX_HEADS * QP, 1), (NS, IDX_HEADS * QP, LANES))
        qc = _heads_rows(cq, NS, G, TD, DSA_HEADS)
        qd = dq.reshape(TD, NS, G, 256).transpose(1, 2, 0, 3).reshape(NS, 1, QP, 256)
        qd = jnp.broadcast_to(qd, (NS, DIFF_HEADS, QP, 256)).reshape(NS, DIFF_HEADS * QP, 256)
        kin = _new_page(sii[:, :64], NS, G, TD, PS)
        kn = _new_page(skk[:, :64], NS, G, TD, PS)
        vn = _new_page(svw[:, :64], NS, G, TD, PS)
        dkn = _new_page(dk, NS, G, TD, PS)
        dvn = _new_page(dv, NS, G, TD, PS)
        ycd = _k2s(page_table, qi, wi, qc, qd, kin, kn, vn, dkn, dvn, diff_lambda[i], row(diff_subln, i),
                   c_idx_t, c_k_t, c_v_t, d_k_t, d_v_t, layer=i, TD=TD, n_sel=n_sel_s, lam_init=lam_init)
        mixcd = ycd.reshape(NS, G, TD, 512).transpose(2, 0, 1, 3).reshape(TD * DB, 512)
        hs, g_new = _k3s(hs, mixab, mixcd, p_sample[i].transpose(1, 0, 2).reshape(TD * DB, -1), w_out_b[i],
                         row(norm_ffn, i), w_gate_b[i], w_up_b[i], conv_ffn[i], w_down_b[i], row(norm_ple, i),
                         w_pg_b[i], w_ple_b[i], gfin, sffn, DB=DB, TD=TD, final_norm=(i == depth - 1))

        def bm(a, width):
            return a[:, :width].reshape(TD, DB, width).transpose(1, 0, 2)

        new_conv = u_new[TD - 2:].transpose(1, 0, 2)
        new_pool = jnp.concatenate([state_pool[i], pv_new.transpose(1, 0, 2)], axis=1)[:, -POOL_BUF:]
        st_s.append((new_conv, new_pool, g_new.transpose(1, 0, 2), bm(skk, 64), bm(svw, 64), bm(sii, 64),
                     bm(dk, 256).reshape(DB, TD, DIFF_HEADS, 64), bm(dv, 256).reshape(DB, TD, DIFF_HEADS, 64)))
    y_sample = hs.reshape(TD, DB, D).transpose(1, 0, 2)

    def col(outs, j, axis):
        return jnp.stack([o[j] for o in outs], axis=axis)

    return (y_prompt, y_sample, col(st_p, 0, 0), col(st_s, 0, 0), col(st_p, 1, 0), col(st_s, 1, 0),
            col(st_p, 2, 0), col(st_s, 2, 0), c_k_p, col(st_s, 3, 1), c_v_p, col(st_s, 4, 1),
            c_idx_p, col(st_s, 5, 1), d_k_p, col(st_s, 6, 1), d_v_p, col(st_s, 7, 1))
```

```python
import functools
import math

import numpy as np
import jax
import jax.numpy as jnp
from jax import lax
from jax.experimental import pallas as pl
from jax.experimental.pallas import tpu as pltpu

F32 = jnp.float32
BF16 = jnp.bfloat16
I32 = jnp.int32

EPS = 1e-6
GROUP_W = 256
POOL_WINDOWS = (2, 4, 8, 16)
POOL_BUF = 15
DSA_HEADS = 4
DSA_HD = 64
IDX_HEADS = 8
IDX_HD = 64
IDX_SCALE = IDX_HEADS ** -0.5 * IDX_HD ** -0.5
TOPK_MAX = 256
DIFF_HEADS = 4
DIFF_VD = 64
DIFF_QD = 32
SPLIT_SIZES = (256, 256, 256, 256, 256, 64, 64, 512, 8, 64, 256, 256, 256)

LANES = 128
SUBLANES = 8
VMEM_LIMIT = 56 * 1024 * 1024

C_GB, C_GC, C_GH, C_PV = 0, 256, 512, 768
C_CQ, C_CQI, C_DQ, C_DK, C_DV = 1024, 1280, 1792, 2048, 2304
C_KK, C_II, C_VW = 2560, 2688, 2816
D_IN_P = 2944

KEY_BLK = 256
Q_BLK = 128
V_AUG = 80
LOG2E = math.log2(math.e)
NEG = -1e30
INT_MIN = -2 ** 31

NT_DIMS = (((1,), (1,)), ((), ()))


def _dot(a, b):
    return jnp.dot(a, b, preferred_element_type=F32)


def _dot_nt(a, b):
    return lax.dot_general(a, b, NT_DIMS, preferred_element_type=F32)


def _rms(x, g):
    ms = jnp.mean(x * x, axis=-1, keepdims=True)
    return (x * lax.rsqrt(ms + EPS)) * g


def _sigmoid(x):
    return 1.0 / (1.0 + jnp.exp(-x))


def _float_key(x):
    b = lax.bitcast_convert_type(x, I32)
    return b ^ (lax.shift_right_arithmetic(b, 31) & 0x7FFFFFFF)


def _lambda_full(lam_ref, lam_init):
    lp = lam_ref[...]
    s1 = jnp.sum(lp[0:1] * lp[1:2], axis=-1, keepdims=True)
    s2 = jnp.sum(lp[2:3] * lp[3:4], axis=-1, keepdims=True)
    return jnp.exp(s1) - jnp.exp(s2) + lam_init


def _const_spec(shape):
    nd = len(shape)
    return pl.BlockSpec(shape, lambda *_: (0,) * nd)


def _k1p_kernel(x_ref, g_ref, w_ref, ca_ref, pm_ref, ps_ref, _a0, _a1, _a2, _a3, _a4,
                mixab_ref, cq_ref, cqi_ref, dq_ref,
                kkb_ref, iib_ref, dkb_ref, cvT_ref, cwiT_ref, dvT_ref, convst_ref, poolst_ref,
                ckT_ref, cvTf_ref, ckiT_ref, dkT_ref, dvTf_ref,
                extu_ref, extx_ref, *, TS):
    t = pl.program_id(1)

    @pl.when(t == 0)
    def _():
        extu_ref[0:8, :] = jnp.zeros((8, GROUP_W), F32)
        extx_ref[0:16, :] = jnp.zeros((16, GROUP_W), F32)

    xn = _rms(x_ref[0], g_ref[...])
    z = _dot(xn.astype(BF16), w_ref[...])

    u = z[:, C_GC:C_GC + 256] * z[:, C_GH:C_GH + 256]
    extu_ref[8:8 + TS, :] = u
    p1 = extu_ref[7:7 + TS, :]
    p2 = extu_ref[6:6 + TS, :]
    ca = ca_ref[...]
    conv = p2 * ca[0:1] + p1 * ca[1:2] + u * ca[2:3]
    mixab_ref[0, :, 0:256] = z[:, C_GB:C_GB + 256] * conv
    extu_ref[0:8, :] = u[TS - 8:TS]
    convst_ref[0] = u[TS - 8:TS]

    pv = z[:, C_PV:C_PV + 256]
    extx_ref[16:16 + TS, :] = pv

    def sh(j, c0):
        return extx_ref[16 - j:16 - j + TS, c0:c0 + LANES]

    s2 = sh(0, 0) + sh(1, 0)
    s4 = s2 + sh(2, 0) + sh(3, 0)
    s8 = sh(0, LANES)
    for j in range(1, 8):
        s8 = s8 + sh(j, LANES)
    s16 = s8
    for j in range(8, 16):
        s16 = s16 + sh(j, LANES)
    lo = lax.broadcasted_iota(I32, (TS, LANES), 1) < 64
    posp1 = (t * TS + 1 + lax.broadcasted_iota(I32, (TS, LANES), 0)).astype(F32)
    d0 = jnp.where(lo, s2, s4) / jnp.where(lo, jnp.minimum(posp1, 2.0), jnp.minimum(posp1, 4.0)) - pv[:, 0:LANES]
    d1 = jnp.where(lo, s8, s16) / jnp.where(lo, jnp.minimum(posp1, 8.0), jnp.minimum(posp1, 16.0)) - pv[:, LANES:]
    d = jnp.concatenate([d0, d1], axis=1)
    mixab_ref[0, :, 256:512] = _dot(d.astype(BF16), pm_ref[...]) * ps_ref[...]
    extx_ref[0:16, :] = pv[TS - 16:TS]
    poolst_ref[0] = pv[TS - 16:TS]

    cq_ref[0] = z[:, C_CQ:C_CQ + 256].astype(BF16)
    cqi_ref[0] = z[:, C_CQI:C_CQI + 512].astype(BF16)
    dq_ref[0] = z[:, C_DQ:C_DQ + 256].astype(BF16)
    dk = z[:, C_DK:C_DK + 256]
    dv = z[:, C_DV:C_DV + 256]
    skk = z[:, C_KK:C_KK + LANES]
    sii = z[:, C_II:C_II + LANES]
    svw = z[:, C_VW:C_VW + LANES]
    nb = TS // KEY_BLK
    kkb_ref[0] = skk.astype(BF16).reshape(nb, KEY_BLK, LANES)
    iib_ref[0] = sii.astype(BF16).reshape(nb, KEY_BLK, LANES)
    dkb_ref[0] = dk.astype(BF16).reshape(nb, KEY_BLK, 256)
    svw_t = svw.T
    dv_t = dv.T
    ckT_ref[0, 0] = skk.T[0:64]
    ckiT_ref[0, 0] = sii.T[0:64]
    cvTf_ref[0, 0] = svw_t[0:64]
    dkT_ref[0, 0] = dk.T
    dvTf_ref[0, 0] = dv_t
    cwiT_ref[0] = svw_t[64:72]
    ones_rows = jnp.where(lax.broadcasted_iota(I32, (V_AUG - 64, KEY_BLK), 0) == 0, 1.0, 0.0).astype(BF16)
    for j in range(nb):
        cvT_ref[0, j] = jnp.concatenate(
            [svw_t[0:64, j * KEY_BLK:(j + 1) * KEY_BLK].astype(BF16), ones_rows], axis=0)
        for h in range(DIFF_HEADS):
            dvT_ref[0, j, h] = jnp.concatenate(
                [dv_t[h * 64:(h + 1) * 64, j * KEY_BLK:(j + 1) * KEY_BLK].astype(BF16), ones_rows], axis=0)


def _k1p(x, g, w, ca, pm, ps, kv_out, *, TS, layer):
    B, T, D = x.shape
    nt = T // TS
    nb = TS // KEY_BLK
    nkb = T // KEY_BLK

    def row_spec(c):
        return pl.BlockSpec((1, TS, c), lambda b, t: (b, t, 0))

    def blk_spec(r, c):
        return pl.BlockSpec((1, nb, r, c), lambda b, t: (b, t, 0, 0))

    out_shape = (
        jax.ShapeDtypeStruct((B, T, 512), F32),
        jax.ShapeDtypeStruct((B, T, 256), BF16),
        jax.ShapeDtypeStruct((B, T, 512), BF16),
        jax.ShapeDtypeStruct((B, T, 256), BF16),
        jax.ShapeDtypeStruct((B, nkb, KEY_BLK, LANES), BF16),
        jax.ShapeDtypeStruct((B, nkb, KEY_BLK, LANES), BF16),
        jax.ShapeDtypeStruct((B, nkb, KEY_BLK, 256), BF16),
        jax.ShapeDtypeStruct((B, nkb, V_AUG, KEY_BLK), BF16),
        jax.ShapeDtypeStruct((B, 8, T), F32),
        jax.ShapeDtypeStruct((B, nkb, DIFF_HEADS, V_AUG, KEY_BLK), BF16),
        jax.ShapeDtypeStruct((B, 8, 256), F32),
        jax.ShapeDtypeStruct((B, 16, 256), F32),
    ) + tuple(jax.ShapeDtypeStruct(a.shape, a.dtype) for a in kv_out)

    def kv_spec(a):
        return pl.BlockSpec((1, 1, a.shape[2], TS), lambda b, t: (b, layer, 0, t))

    out_specs = (
        row_spec(512), row_spec(256), row_spec(512), row_spec(256),
        blk_spec(KEY_BLK, LANES), blk_spec(KEY_BLK, LANES), blk_spec(KEY_BLK, 256),
        blk_spec(V_AUG, KEY_BLK),
        pl.BlockSpec((1, 8, TS), lambda b, t: (b, 0, t)),
        pl.BlockSpec((1, nb, DIFF_HEADS, V_AUG, KEY_BLK), lambda b, t: (b, t, 0, 0, 0)),
        pl.BlockSpec((1, 8, 256), lambda b, t: (b, 0, 0)),
        pl.BlockSpec((1, 16, 256), lambda b, t: (b, 0, 0)),
    ) + tuple(kv_spec(a) for a in kv_out)
    in_specs = [
        row_spec(D), _const_spec(g.shape), _const_spec(w.shape), _const_spec(ca.shape),
        _const_spec(pm.shape), _const_spec(ps.shape),
    ] + [pl.BlockSpec(memory_space=pl.ANY)] * len(kv_out)
    n_in, n_out = 6, len(out_shape) - len(kv_out)
    return pl.pallas_call(
        functools.partial(_k1p_kernel, TS=TS),
        grid=(B, nt), in_specs=in_specs, out_specs=out_specs, out_shape=out_shape,
        input_output_aliases={n_in + k: n_out + k for k in range(len(kv_out))},
        scratch_shapes=[pltpu.VMEM((TS + 8, GROUP_W), F32), pltpu.VMEM((TS + 16, GROUP_W), F32)],
        compiler_params=pltpu.CompilerParams(dimension_semantics=("arbitrary", "arbitrary"),
                                             vmem_limit_bytes=VMEM_LIMIT),
        name="mixer_in_prompt",
    )(x, g, w, ca, pm, ps, *kv_out)


def _k2p_kernel(cq_ref, cqi_ref, cwiT_ref, dq_ref, kkb_ref, iib_ref, cvT_ref, dkb_ref, dvT_ref,
                lam_ref, subln_ref, out_ref, key_ref, lgc_ref, lgd_ref, accc_ref, accd_ref, mxc_ref, thr_ref,
                *, n_sel, lam_init):
    qb = pl.program_id(1)
    q0 = qb * Q_BLK
    nkb = (q0 + Q_BLK + KEY_BLK - 1) // KEY_BLK
    qpos = q0 + lax.broadcasted_iota(I32, (KEY_BLK, Q_BLK), 1)
    krow = lax.broadcasted_iota(I32, (KEY_BLK, Q_BLK), 0)
    lane_q = lax.broadcasted_iota(I32, (Q_BLK, LANES), 1)
    lo_half = lane_q < 64
    NG = KEY_BLK // SUBLANES

    def head_pair(tile):
        zero = jnp.zeros_like(tile)
        return jnp.concatenate([jnp.where(lo_half, tile, zero), jnp.where(lo_half, zero, tile)], axis=0)

    cqi = cqi_ref[0]
    wT = cwiT_ref[0]
    qi = [head_pair(cqi[:, j * LANES:(j + 1) * LANES]) for j in range(IDX_HEADS // 2)]
    cq = cq_ref[0] * (DSA_HD ** -0.5)
    qc = [head_pair(cq[:, j * LANES:(j + 1) * LANES]) for j in range(DSA_HEADS // 2)]
    dq = dq_ref[0]
    lane256 = lax.broadcasted_iota(I32, (Q_BLK, 256), 1)

    def map_rows(h, mm):
        c0 = h * 64 + mm * DIFF_QD
        return jnp.where(jnp.logical_and(lane256 >= c0, lane256 < c0 + DIFF_QD), dq, jnp.zeros_like(dq))

    qd = [jnp.concatenate([map_rows(h, 0), map_rows(h, 1)], axis=0) for h in range(DIFF_HEADS)]
    c2 = (DIFF_QD ** -0.5) * LOG2E

    def pass_a(kb, mx, masked):
        ki = iib_ref[0, kb]
        acc = jnp.zeros((KEY_BLK, Q_BLK), F32)
        for j in range(IDX_HEADS // 2):
            s = _dot_nt(ki, qi[j])
            acc = acc + jnp.maximum(s[:, :Q_BLK], 0.0) * wT[2 * j:2 * j + 1]
            acc = acc + jnp.maximum(s[:, Q_BLK:], 0.0) * wT[2 * j + 1:2 * j + 2]
        key = _float_key(acc * IDX_SCALE + 0.0)
        if masked:
            causal = (kb * KEY_BLK + krow) <= qpos
            key = jnp.where(causal, key, INT_MIN)
        key_ref[kb] = key
        kk = kkb_ref[0, kb]
        for j in range(DSA_HEADS // 2):
            lgc_ref[j, kb] = _dot_nt(kk, qc[j])
        kd = dkb_ref[0, kb]
        new = list(mx)
        for h in range(DIFF_HEADS):
            s = _dot_nt(kd, qd[h]) * c2
            if masked:
                s = jnp.where(jnp.concatenate([causal, causal], axis=1), s, NEG)
            lgd_ref[h, kb] = s
            new[h] = jnp.maximum(new[h], jnp.max(s.reshape(NG, SUBLANES, 2 * Q_BLK), axis=0))
        return tuple(new)

    n_open = nkb - 1

    def a_blocks(first, count):
        def run(m):
            for u in range(count):
                m = pass_a(first + u, m, False)
            return m
        return run

    mx = lax.fori_loop(0, n_open // 4, lambda i, m: a_blocks(4 * i, 4)(m),
                       tuple(jnp.full((SUBLANES, 2 * Q_BLK), NEG, F32) for _ in range(DIFF_HEADS)))
    rest_a = (n_open // 4) * 4
    mx = lax.cond(n_open % 4 >= 2, a_blocks(rest_a, 2), lambda m: m, mx)
    mx = lax.cond(n_open % 2 == 1, a_blocks(n_open - 1, 1), lambda m: m, mx)
    mx = pass_a(nkb - 1, mx, True)

    @pl.when(nkb % 2 == 1)
    def _():
        key_ref[nkb] = jnp.full((KEY_BLK, Q_BLK), INT_MIN, I32)

    def count(pred):
        def body(i, acc):
            for u in range(2):
                kb = 2 * i + u
                c = jnp.where(pred(kb, key_ref[kb]), 1, 0)
                acc = acc + jnp.sum(c.reshape(NG, SUBLANES, Q_BLK), axis=0)
            return acc
        acc = lax.fori_loop(0, (nkb + 1) // 2, body, jnp.zeros((SUBLANES, Q_BLK), I32))
        return jnp.sum(acc, axis=0, keepdims=True)

    def search(npairs):
        def step(it, base):
            cand = base + lax.shift_left(jnp.int32(1), 31 - it)
            acc = jnp.zeros((SUBLANES, Q_BLK), I32)
            for kb in range(2 * npairs):
                acc = acc + jnp.sum(jnp.where(key_ref[kb] >= cand, 1, 0).reshape(NG, SUBLANES, Q_BLK), axis=0)
            cnt = jnp.sum(acc, axis=0, keepdims=True)
            return jnp.where(cnt >= n_sel, cand, base)
        return lax.fori_loop(0, 32, step, jnp.full((1, Q_BLK), INT_MIN, I32))

    thr_ref[...] = jnp.full((1, Q_BLK), INT_MIN, I32)
    for npairs in range(1, key_ref.shape[0] // 2 + 1):
        @pl.when(jnp.logical_and(q0 + Q_BLK > n_sel, (nkb + 1) // 2 == npairs))
        def _(npairs=npairs):
            thr_ref[...] = search(npairs)

    thr = thr_ref[...]

    cnt_gt = count(lambda kb, k: k > thr)
    cnt_eq = count(lambda kb, k: k == thr)
    need = n_sel - cnt_gt
    excess = jnp.logical_and(cnt_eq > need, thr > INT_MIN)
    any_excess = jnp.max(jnp.where(excess, 1, 0))

    @pl.when(any_excess > 0)
    def _():
        def tie_body(it, jp):
            c = jp + lax.shift_left(jnp.int32(1), 10 - it)
            f = count(lambda kb, k: jnp.logical_and(k == thr, (kb * KEY_BLK + krow) < c))
            return jnp.where(f < need, c, jp)
        jp = lax.fori_loop(0, 11, tie_body, jnp.zeros((1, Q_BLK), I32))

        def demote(kb, carry):
            k = key_ref[kb]
            drop = jnp.logical_and(jnp.logical_and(k == thr, (kb * KEY_BLK + krow) > jp), excess)
            key_ref[kb] = jnp.where(drop, thr - 1, k)
            return carry
        lax.fori_loop(0, nkb, demote, 0)

    thr_sel = jnp.maximum(thr, INT_MIN + 1)

    def masked_c(j, kb):
        sel = key_ref[kb] >= thr_sel
        lg = lgc_ref[j, kb]
        return jnp.concatenate([jnp.where(sel, lg[:, :Q_BLK], NEG), jnp.where(sel, lg[:, Q_BLK:], NEG)], axis=1)

    md = [jnp.max(m, axis=0, keepdims=True) for m in mx]
    accd_ref[...] = jnp.zeros(accd_ref.shape, F32)
    accc_ref[...] = jnp.zeros(accc_ref.shape, F32)
    mxc_ref[...] = jnp.full(mxc_ref.shape, NEG, F32)

    def over_blocks(step):
        def quad(i, carry):
            step(tuple(4 * i + u for u in range(4)))
            return carry
        lax.fori_loop(0, nkb // 4, quad, 0)
        rest = (nkb // 4) * 4

        @pl.when(nkb % 4 >= 2)
        def _():
            step((rest, rest + 1))

        @pl.when(nkb % 2 == 1)
        def _():
            step((nkb - 1,))

    def pass_b(kbs):
        for h in range(DIFF_HEADS):
            p = jnp.concatenate([jnp.exp2(lgd_ref[h, kb] - md[h]).astype(BF16) for kb in kbs], axis=0)
            vT = jnp.concatenate([dvT_ref[0, kb, h] for kb in kbs], axis=1)
            accd_ref[h] += _dot(vT, p)
        vT = jnp.concatenate([cvT_ref[0, kb] for kb in kbs], axis=1)
        for j in range(DSA_HEADS // 2):
            lgs = [masked_c(j, kb) for kb in kbs]
            g = jnp.max(lgs[0].reshape(NG, SUBLANES, 2 * Q_BLK), axis=0)
            for lg in lgs[1:]:
                g = jnp.maximum(g, jnp.max(lg.reshape(NG, SUBLANES, 2 * Q_BLK), axis=0))
            m_old = mxc_ref[j, 0:1]
            m_new = jnp.maximum(m_old, jnp.max(g, axis=0, keepdims=True))
            p = jnp.concatenate([jnp.exp(lg - m_new).astype(BF16) for lg in lgs], axis=0)
            accc_ref[j] = accc_ref[j] * jnp.exp(m_old - m_new) + _dot(vT, p)
            mxc_ref[j] = jnp.broadcast_to(m_new, (SUBLANES, 2 * Q_BLK))

    over_blocks(pass_b)
    outs = []
    for h in range(DSA_HEADS):
        a = accc_ref[h // 2, :, (h % 2) * Q_BLK:(h % 2 + 1) * Q_BLK]
        outs.append(a[0:DSA_HD] / a[DSA_HD:DSA_HD + 1])
    out_ref[0, :, 0:256] = jnp.concatenate(outs, axis=0).T

    lam = _lambda_full(lam_ref, lam_init)
    subln = subln_ref[...]
    outs = []
    for h in range(DIFF_HEADS):
        a1 = accd_ref[h, :, 0:Q_BLK]
        a2 = accd_ref[h, :, Q_BLK:2 * Q_BLK]
        o = a1[0:DIFF_VD] / a1[DIFF_VD:DIFF_VD + 1] - lam * (a2[0:DIFF_VD] / a2[DIFF_VD:DIFF_VD + 1])
        ms = jnp.mean(o * o, axis=0, keepdims=True)
        outs.append((o * lax.rsqrt(ms + EPS)) * subln * (1.0 - lam_init))
    out_ref[0, :, 256:512] = jnp.concatenate(outs, axis=0).T


def _k2p(cq, cqi, cwiT, dq, kkb, iib, cvT, dkb, dvT, lam, subln, *, n_sel, lam_init):
    B, T, _ = cq.shape
    nq = T // Q_BLK
    nkb = T // KEY_BLK

    def q_spec(c):
        return pl.BlockSpec((1, Q_BLK, c), lambda b, q: (b, q, 0))

    def seq_spec(r, c):
        return pl.BlockSpec((1, nkb, r, c), lambda b, q: (b, 0, 0, 0))

    in_specs = [
        q_spec(256), q_spec(512), pl.BlockSpec((1, 8, Q_BLK), lambda b, q: (b, 0, q)), q_spec(256),
        seq_spec(KEY_BLK, LANES), seq_spec(KEY_BLK, LANES), seq_spec(V_AUG, KEY_BLK),
        seq_spec(KEY_BLK, 256),
        pl.BlockSpec((1, nkb, DIFF_HEADS, V_AUG, KEY_BLK), lambda b, q: (b, 0, 0, 0, 0)),
        _const_spec(lam.shape), _const_spec(subln.shape),
    ]
    return pl.pallas_call(
        functools.partial(_k2p_kernel, n_sel=n_sel, lam_init=lam_init),
        grid=(B, nq), in_specs=in_specs, out_specs=q_spec(512),
        out_shape=jax.ShapeDtypeStruct((B, T, 512), F32),
        scratch_shapes=[pltpu.VMEM((nkb + 1, KEY_BLK, Q_BLK), I32),
                        pltpu.VMEM((DSA_HEADS // 2, nkb, KEY_BLK, 2 * Q_BLK), F32),
                        pltpu.VMEM((DIFF_HEADS, nkb, KEY_BLK, 2 * Q_BLK), F32),
                        pltpu.VMEM((DSA_HEADS // 2, V_AUG, 2 * Q_BLK), F32),
                        pltpu.VMEM((DIFF_HEADS, V_AUG, 2 * Q_BLK), F32),
                        pltpu.VMEM((DSA_HEADS // 2, SUBLANES, 2 * Q_BLK), F32),
                        pltpu.VMEM((1, Q_BLK), I32)],
        compiler_params=pltpu.CompilerParams(dimension_semantics=("arbitrary", "arbitrary"),
                                             vmem_limit_bytes=VMEM_LIMIT),
        name="attention_prompt",
    )(cq, cqi, cwiT, dq, kkb, iib, cvT, dkb, dvT, lam, subln)


def _ple_tail(h2, p_bf, gple, wpg, wple, gfin, final_norm):
    gate = _sigmoid(_dot(_rms(h2, gple).astype(BF16), wpg))
    h3 = h2 + _dot(p_bf, wple) * gate
    if final_norm:
        h3 = _rms(h3, gfin)
    return h3


def _k3p_kernel(h_ref, ab_ref, cd_ref, p_ref, wout_ref, gffn_ref, wg_ref, wu_ref, cf_ref, wd_ref,
                gple_ref, wpg_ref, wple_ref, gfin_ref, hout_ref, ffnst_ref, extg_ref, *, TS, final_norm):
    t = pl.program_id(1)

    @pl.when(t == 0)
    def _():
        extg_ref[0:8, :] = jnp.zeros((8, extg_ref.shape[1]), F32)

    mix = jnp.concatenate([ab_ref[0], cd_ref[0]], axis=1).astype(BF16)
    h1 = h_ref[0] + _dot(mix, wout_ref[...])
    xn2 = _rms(h1, gffn_ref[...]).astype(BF16)
    g = _dot(xn2, wg_ref[...])
    extg_ref[8:8 + TS, :] = g
    p1 = extg_ref[7:7 + TS, :]
    p2 = extg_ref[6:6 + TS, :]
    cf = cf_ref[...]
    gc = p2 * cf[0:1] + p1 * cf[1:2] + g * cf[2:3]
    extg_ref[0:8, :] = g[TS - 8:TS]
    ffnst_ref[0] = g[TS - 8:TS]
    act = (gc * _sigmoid(gc)) * _dot(xn2, wu_ref[...])
    h2 = h1 + _dot(act.astype(BF16), wd_ref[...])
    hout_ref[0] = _ple_tail(h2, p_ref[0].astype(BF16), gple_ref[...], wpg_ref[...], wple_ref[...],
                            gfin_ref[...], final_norm)


def _k3p(h, ab, cd, p, wout, gffn, wg, wu, cf, wd, gple, wpg, wple, gfin, *, TS, final_norm):
    B, T, D = h.shape
    dff = wg.shape[1]
    nt = T // TS

    def row_spec(c):
        return pl.BlockSpec((1, TS, c), lambda b, t: (b, t, 0))

    def w_spec(a):
        nd = a.ndim
        return pl.BlockSpec(a.shape, lambda *_: (0,) * nd, pipeline_mode=pl.Buffered(1))

    in_specs = [row_spec(D), row_spec(512), row_spec(512), row_spec(p.shape[2]),
                w_spec(wout), w_spec(gffn), w_spec(wg), w_spec(wu), w_spec(cf), w_spec(wd),
                w_spec(gple), w_spec(wpg), w_spec(wple), w_spec(gfin)]
    out_specs = (row_spec(D), pl.BlockSpec((1, 8, dff), lambda b, t: (b, 0, 0)))
    out_shape = (jax.ShapeDtypeStruct((B, T, D), F32), jax.ShapeDtypeStruct((B, 8, dff), F32))
    return pl.pallas_call(
        functools.partial(_k3p_kernel, TS=TS, final_norm=final_norm),
        grid=(B, nt), in_specs=in_specs, out_specs=out_specs, out_shape=out_shape,
        scratch_shapes=[pltpu.VMEM((TS + 8, dff), F32)],
        compiler_params=pltpu.CompilerParams(dimension_semantics=("arbitrary", "arbitrary"),
                                             vmem_limit_bytes=VMEM_LIMIT),
        name="channel_mix_prompt",
    )(h, ab, cd, p, wout, gffn, wg, wu, cf, wd, gple, wpg, wple, gfin)


def _k1s_kernel(x_ref, g_ref, w_ref, ca_ref, pm_ref, ps_ref, sconv_ref, spool_ref,
                mixab_ref, cq_ref, cqi_ref, dq_ref, dk_ref, dv_ref, skk_ref, sii_ref, svw_ref,
                u_ref, pv_ref, *, DB, TD, pos0):
    xn = _rms(x_ref[...], g_ref[...])
    z = _dot(xn.astype(BF16), w_ref[...])

    def slab(t, c0, c1):
        return z[t * DB:(t + 1) * DB, c0:c1]

    ca = ca_ref[...]
    us = [slab(t, C_GC, C_GC + 256) * slab(t, C_GH, C_GH + 256) for t in range(TD)]
    extu = [sconv_ref[0], sconv_ref[1]] + us
    pvs = [slab(t, C_PV, C_PV + 256) for t in range(TD)]
    extx = [spool_ref[j] for j in range(POOL_BUF)] + pvs
    lo = lax.broadcasted_iota(I32, (DB, LANES), 1) < 64
    pm = pm_ref[...]
    ps = ps_ref[...]
    for t in range(TD):
        conv = extu[t] * ca[0:1] + extu[t + 1] * ca[1:2] + extu[t + 2] * ca[2:3]
        mixab_ref[t * DB:(t + 1) * DB, 0:256] = slab(t, C_GB, C_GB + 256) * conv
        u_ref[t] = us[t]
        pv_ref[t] = pvs[t]
        e = POOL_BUF + t

        def win(n, c0):
            s = extx[e][:, c0:c0 + LANES]
            for j in range(1, n):
                s = s + extx[e - j][:, c0:c0 + LANES]
            return s

        cnt = [float(min(w, pos0 + t + 1)) for w in POOL_WINDOWS]
        d0 = jnp.where(lo, win(2, 0) / cnt[0], win(4, 0) / cnt[1]) - pvs[t][:, 0:LANES]
        d1 = jnp.where(lo, win(8, LANES) / cnt[2], win(16, LANES) / cnt[3]) - pvs[t][:, LANES:]
        d = jnp.concatenate([d0, d1], axis=1)
        mixab_ref[t * DB:(t + 1) * DB, 256:512] = _dot(d.astype(BF16), pm) * ps

    cq_ref[...] = z[:, C_CQ:C_CQ + 256].astype(BF16)
    cqi_ref[...] = z[:, C_CQI:C_CQI + 512].astype(BF16)
    dq_ref[...] = z[:, C_DQ:C_DQ + 256].astype(BF16)
    dk_ref[...] = z[:, C_DK:C_DK + 256]
    dv_ref[...] = z[:, C_DV:C_DV + 256]
    skk_ref[...] = z[:, C_KK:C_KK + LANES]
    sii_ref[...] = z[:, C_II:C_II + LANES]
    svw_ref[...] = z[:, C_VW:C_VW + LANES]


def _k1s(x, g, w, ca, pm, ps, sconv, spool, *, DB, TD, pos0):
    R = x.shape[0]
    out_shape = (
        jax.ShapeDtypeStruct((R, 512), F32), jax.ShapeDtypeStruct((R, 256), BF16),
        jax.ShapeDtypeStruct((R, 512), BF16), jax.ShapeDtypeStruct((R, 256), BF16),
        jax.ShapeDtypeStruct((R, 256), F32), jax.ShapeDtypeStruct((R, 256), F32),
        jax.ShapeDtypeStruct((R, LANES), F32), jax.ShapeDtypeStruct((R, LANES), F32),
        jax.ShapeDtypeStruct((R, LANES), F32),
        jax.ShapeDtypeStruct((TD, DB, 256), F32), jax.ShapeDtypeStruct((TD, DB, 256), F32),
    )
    args = (x, g, w, ca, pm, ps, sconv, spool)
    return pl.pallas_call(
        functools.partial(_k1s_kernel, DB=DB, TD=TD, pos0=pos0),
        grid=(1,), in_specs=[_const_spec(a.shape) for a in args],
        out_specs=tuple(_const_spec(s.shape) for s in out_shape), out_shape=out_shape,
        compiler_params=pltpu.CompilerParams(dimension_semantics=("arbitrary",), vmem_limit_bytes=VMEM_LIMIT),
        name="mixer_in_sample",
    )(*args)


QP = 8


def _wrap32(v):
    return ((v + 2 ** 31) % 2 ** 32) - 2 ** 31


def _k2s_kernel(pt_ref, qi_ref, wi_ref, qc_ref, qd_ref, kin_ref, kn_ref, vn_ref, dkn_ref, dvn_ref,
                lam_ref, subln_ref, cidx_hbm, ck_hbm, cv_hbm, dk_hbm, dv_hbm,
                out_ref, key_ref, clog_ref, dlog_ref, idx_refs, k_refs, v_refs, dk_refs, dv_refs, sem,
                *, NP, PS, TD, n_sel, lam_init, layer):
    G = QP // TD
    n = G * NP
    SC = (NP + 1) * PS

    i = pl.program_id(0)
    slot = lax.rem(i, 2)
    hbms = (cidx_hbm, ck_hbm, cv_hbm, dk_hbm, dv_hbm)
    bufs = (idx_refs, k_refs, v_refs, dk_refs, dv_refs)

    def page_copy(hbm, buf, page, sl, gp):
        return pltpu.make_async_copy(hbm.at[page, layer], buf.at[sl, gp], sem.at[sl])

    def issue(step, sl):
        def body(gp, carry):
            page = pt_ref[G * step + gp // NP, gp % NP]
            for hbm, buf in zip(hbms, bufs):
                page_copy(hbm, buf, page, sl, gp).start()
            return carry
        lax.fori_loop(0, n, body, 0)

    @pl.when(i == 0)
    def _():
        issue(0, 0)

    @pl.when(i + 1 < pl.num_programs(0))
    def _():
        issue(i + 1, 1 - slot)

    def wait_body(gp, carry):
        for hbm, buf in zip(hbms, bufs):
            page_copy(hbm, buf, 0, slot, gp).wait()
        return carry

    lax.fori_loop(0, n, wait_body, 0)

    def own_half(x):
        elem0 = (lax.broadcasted_iota(I32, (x.shape[0], PS), 0) % QP) < TD
        return jnp.where(elem0, x[:, :PS], x[:, PS:])

    def split_half(x):
        elem0 = (lax.broadcasted_iota(I32, x.shape, 0) % QP) < TD
        zero = jnp.zeros_like(x)
        return jnp.concatenate([jnp.where(elem0, x, zero), jnp.where(elem0, zero, x)], axis=1)

    def pages(refs, new_ref, p, rows):
        tiles = [(refs[slot, g * NP + p] if p < NP else new_ref[0, g]).reshape(rows, PS).astype(BF16)
                 for g in range(G)]
        return jnp.concatenate(tiles, axis=1)

    qi = qi_ref[0]
    wi = wi_ref[0]
    col = lax.broadcasted_iota(I32, (QP, PS), 1)
    qt = lax.broadcasted_iota(I32, (QP, PS), 0) % TD
    new_ok = jnp.logical_and(col < TD, col <= qt)
    for p in range(NP + 1):
        s = own_half(_dot(qi, pages(idx_refs, kin_ref, p, IDX_HD)))
        r = (jnp.maximum(s, 0.0) * wi).reshape(IDX_HEADS, QP, PS)
        score = jnp.sum(r, axis=0) * IDX_SCALE + 0.0
        key = _float_key(score)
        if p == NP:
            key = jnp.where(new_ok, key, INT_MIN)
        key_ref[:, p * PS:(p + 1) * PS] = key

    qc = qc_ref[0] * (DSA_HD ** -0.5)
    for p in range(NP + 1):
        clog_ref[:, p * PS:(p + 1) * PS] = own_half(_dot(qc, pages(k_refs, kn_ref, p, DSA_HD)))

    qd0 = qd_ref[0]
    rhead = lax.broadcasted_iota(I32, qd0.shape, 0) // QP
    lane = lax.broadcasted_iota(I32, qd0.shape, 1)
    zero = jnp.zeros_like(qd0)
    qd = jnp.concatenate(
        [jnp.where(jnp.logical_and(lane >= rhead * 64 + mm * DIFF_QD, lane < rhead * 64 + (mm + 1) * DIFF_QD),
                   qd0, zero) for mm in range(2)], axis=0)
    HD = DIFF_HEADS * DIFF_VD
    nrow = 2 * DIFF_HEADS * QP
    for p in range(NP + 1):
        s = own_half(_dot(qd, pages(dk_refs, dkn_ref, p, HD))) * (DIFF_QD ** -0.5)
        if p == NP:
            ncol = lax.broadcasted_iota(I32, (nrow, PS), 1)
            nqt = lax.broadcasted_iota(I32, (nrow, PS), 0) % TD
            s = jnp.where(jnp.logical_and(ncol < TD, ncol <= nqt), s, NEG)
        dlog_ref[:, p * PS:(p + 1) * PS] = s
    lgd = dlog_ref[...]
    md = jnp.max(lgd, axis=1, keepdims=True)
    ped = jnp.exp(lgd - md)
    ld = jnp.sum(ped, axis=1, keepdims=True)
    dlog_ref[...] = ped
    accd = jnp.zeros((nrow, HD), F32)
    for p in range(NP + 1):
        pp = split_half(dlog_ref[:, p * PS:(p + 1) * PS]).astype(BF16)
        accd = accd + _dot_nt(pp, pages(dv_refs, dvn_ref, p, HD))
    od = accd / ld
    half = DIFF_HEADS * QP
    lam = _lambda_full(lam_ref, lam_init)
    subln = subln_ref[...]
    for h in range(DIFF_HEADS):
        o = (od[h * QP:(h + 1) * QP, h * 64:(h + 1) * 64]
             - lam * od[half + h * QP:half + (h + 1) * QP, h * 64:(h + 1) * 64])
        ms = jnp.mean(o * o, axis=-1, keepdims=True)
        out_ref[0, :, 256 + h * 64:256 + (h + 1) * 64] = (o * lax.rsqrt(ms + EPS)) * subln * (1.0 - lam_init)

    kcol = lax.broadcasted_iota(I32, (QP, SC), 1)

    def count(pred):
        c = jnp.where(pred(key_ref[...]), 1.0, 0.0)
        return jnp.sum(c, axis=1, keepdims=True)

    thr = jnp.full((QP, 1), INT_MIN, I32)
    for sh, nbits in [(30, 2)] + [(27 - 3 * s, 3) for s in range(10)]:
        passed = jnp.zeros((QP, 1), I32)
        for j in range(1, 2 ** nbits):
            cnt = count(lambda k, c=thr + _wrap32(j << sh): k >= c)
            passed = passed + jnp.where(cnt >= n_sel, 1, 0)
        thr = thr + lax.shift_left(passed, sh)
    cnt_gt = count(lambda k: k > thr)
    cnt_eq = count(lambda k: k == thr)
    need = n_sel - cnt_gt
    excess = jnp.logical_and(cnt_eq > need, thr > INT_MIN)
    any_excess = jnp.max(jnp.where(excess, 1, 0))

    @pl.when(any_excess > 0)
    def _():
        def tie_body(it, jp):
            c = jp + lax.shift_left(jnp.int32(1), 12 - it)
            f = count(lambda k: jnp.logical_and(k == thr, kcol < c))
            return jnp.where(f < need, c, jp)
        jp = lax.fori_loop(0, 13, tie_body, jnp.zeros((QP, 1), I32))
        k = key_ref[...]
        drop = jnp.logical_and(jnp.logical_and(k == thr, kcol > jp), excess)
        key_ref[...] = jnp.where(drop, thr - 1, k)

    thr_sel = jnp.maximum(thr, INT_MIN + 1)
    sel = key_ref[...] >= thr_sel

    lg = jnp.where(sel[None], clog_ref[...].reshape(DSA_HEADS, QP, SC), NEG)
    lg = lg.reshape(DSA_HEADS * QP, SC)
    m = jnp.max(lg, axis=1, keepdims=True)
    pe = jnp.exp(lg - m)
    l = jnp.sum(pe, axis=1, keepdims=True)
    clog_ref[...] = pe
    acc = jnp.zeros((DSA_HEADS * QP, DSA_HD), F32)
    for p in range(NP + 1):
        pp = split_half(clog_ref[:, p * PS:(p + 1) * PS]).astype(BF16)
        acc = acc + _dot_nt(pp, pages(v_refs, vn_ref, p, DSA_HD))
    oc = acc / l
    for h in range(DSA_HEADS):
        out_ref[0, :, h * 64:(h + 1) * 64] = oc[h * QP:(h + 1) * QP]


def _k2s(page_table, qi, wi, qc, qd, kin, kn, vn, dkn, dvn, lam, subln,
         c_idx, c_k, c_v, d_k, d_v, *, layer, TD, n_sel, lam_init):
    DB, NP = page_table.shape
    PS = c_k.shape[3]
    G = QP // TD
    NS = DB // G

    def b_spec(a):
        return pl.BlockSpec((1,) + a.shape[1:], lambda b, pt: (b,) + (0,) * (a.ndim - 1))

    def c_spec(a):
        nd = a.ndim
        return pl.BlockSpec(a.shape, lambda b, pt: (0,) * nd)

    in_specs = [b_spec(qi), b_spec(wi), b_spec(qc), b_spec(qd), b_spec(kin), b_spec(kn), b_spec(vn),
                b_spec(dkn), b_spec(dvn), c_spec(lam), c_spec(subln)]
    in_specs += [pl.BlockSpec(memory_space=pl.ANY)] * 5
    SC = (NP + 1) * PS
    n = G * NP
    grid_spec = pltpu.PrefetchScalarGridSpec(
        num_scalar_prefetch=1, grid=(NS,), in_specs=in_specs,
        out_specs=pl.BlockSpec((1, QP, 512), lambda b, pt: (b, 0, 0)),
        scratch_shapes=[pltpu.VMEM((QP, SC), I32), pltpu.VMEM((DSA_HEADS * QP, SC), F32),
                        pltpu.VMEM((2 * DIFF_HEADS * QP, SC), F32)]
        + [pltpu.VMEM((2, n, 64, PS), F32)] * 3 + [pltpu.VMEM((2, n, DIFF_HEADS, 64, PS), F32)] * 2
        + [pltpu.SemaphoreType.DMA((2,))],
    )
    args = [page_table, qi, wi, qc, qd, kin, kn, vn, dkn, dvn, lam, subln, c_idx, c_k, c_v, d_k, d_v]
    return pl.pallas_call(
        functools.partial(_k2s_kernel, NP=NP, PS=PS, TD=TD, n_sel=n_sel, lam_init=lam_init, layer=layer),
        grid_spec=grid_spec, out_shape=jax.ShapeDtypeStruct((NS, QP, 512), F32),
        compiler_params=pltpu.CompilerParams(dimension_semantics=("arbitrary",), vmem_limit_bytes=VMEM_LIMIT),
        name="attention_sample",
    )(*args)


FF_TILE = 256


def _k3s_kernel(h_ref, ab_ref, cd_ref, p_ref, wout_ref, gffn_ref, wg_ref, wu_ref, cf_ref, wd_ref,
                gple_ref, wpg_ref, wple_ref, gfin_ref, sffn_ref, hout_ref, gout_ref,
                h1_ref, xn2_ref, acc_ref, *, DB, TD, final_norm):
    f = pl.program_id(0)

    @pl.when(f == 0)
    def _():
        mix = jnp.concatenate([ab_ref[...], cd_ref[...]], axis=1).astype(BF16)
        h1 = h_ref[...] + _dot(mix, wout_ref[...])
        h1_ref[...] = h1
        xn2_ref[...] = _rms(h1, gffn_ref[...]).astype(BF16)
        acc_ref[...] = jnp.zeros(acc_ref.shape, F32)

    xn2 = xn2_ref[...]
    g = _dot(xn2, wg_ref[...])
    cf = cf_ref[...]
    gs = [sffn_ref[0], sffn_ref[1]] + [g[t * DB:(t + 1) * DB] for t in range(TD)]
    gc = jnp.concatenate([gs[t] * cf[0:1] + gs[t + 1] * cf[1:2] + gs[t + 2] * cf[2:3] for t in range(TD)], axis=0)
    act = (gc * _sigmoid(gc)) * _dot(xn2, wu_ref[...])
    acc_ref[...] += _dot(act.astype(BF16), wd_ref[...])
    gout_ref[0] = gs[TD]
    gout_ref[1] = gs[TD + 1]

    @pl.when(f == pl.num_programs(0) - 1)
    def _():
        h2 = h1_ref[...] + acc_ref[...]
        hout_ref[...] = _ple_tail(h2, p_ref[...].astype(BF16), gple_ref[...], wpg_ref[...], wple_ref[...],
                                  gfin_ref[...], final_norm)


def _k3s(h, ab, cd, p, wout, gffn, wg, wu, cf, wd, gple, wpg, wple, gfin, sffn, *, DB, TD, final_norm):
    R, D = h.shape
    dff = wg.shape[1]
    nf = dff // FF_TILE
    cs = _const_spec
    in_specs = [cs(h.shape), cs(ab.shape), cs(cd.shape), cs(p.shape), cs(wout.shape), cs(gffn.shape),
                pl.BlockSpec((D, FF_TILE), lambda f: (0, f)), pl.BlockSpec((D, FF_TILE), lambda f: (0, f)),
                pl.BlockSpec((3, FF_TILE), lambda f: (0, f)), pl.BlockSpec((FF_TILE, D), lambda f: (f, 0)),
                cs(gple.shape), cs(wpg.shape), cs(wple.shape), cs(gfin.shape),
                pl.BlockSpec((2, DB, FF_TILE), lambda f: (0, 0, f))]
    out_specs = (cs((R, D)), pl.BlockSpec((2, DB, FF_TILE), lambda f: (0, 0, f)))
    out_shape = (jax.ShapeDtypeStruct((R, D), F32), jax.ShapeDtypeStruct((2, DB, dff), F32))
    return pl.pallas_call(
        functools.partial(_k3s_kernel, DB=DB, TD=TD, final_norm=final_norm),
        grid=(nf,), in_specs=in_specs, out_specs=out_specs, out_shape=out_shape,
        scratch_shapes=[pltpu.VMEM((R, D), F32), pltpu.VMEM((R, D), BF16), pltpu.VMEM((R, D), F32)],
        compiler_params=pltpu.CompilerParams(dimension_semantics=("arbitrary",), vmem_limit_bytes=VMEM_LIMIT),
        name="channel_mix_sample",
    )(h, ab, cd, p, wout, gffn, wg, wu, cf, wd, gple, wpg, wple, gfin, sffn)


def _prep_w_in(w_in):
    cuts = np.cumsum(SPLIT_SIZES)[:-1].tolist()
    gb, gc, gh, pv, cq, ck, cv, cqi, cwi, cki, dq, dk, dv = jnp.split(w_in, cuts, axis=-1)
    pad = jnp.zeros(w_in.shape[:-1] + (LANES - 64 - IDX_HEADS,), w_in.dtype)
    cols = [gb, gc, gh, pv, cq, cqi, dq, dk, dv, ck, ck, cki, cki, cv, cwi, pad]
    return jnp.concatenate(cols, axis=-1).astype(BF16)


def _block_diag_maps(pool_maps):
    depth, G, c, _ = pool_maps.shape
    out = jnp.zeros((depth, G * c, G * c), pool_maps.dtype)
    for g in range(G):
        out = out.at[:, g * c:(g + 1) * c, g * c:(g + 1) * c].set(pool_maps[:, g])
    return out.astype(BF16)


def _heads_rows(a, NS, G, TD, nh):
    a = a.reshape(TD, NS, G, nh, 64).transpose(1, 3, 2, 0, 4)
    return a.reshape(NS, nh * QP, 64)


def _new_page(a, NS, G, TD, PS):
    c = a.shape[1]
    a = a.astype(BF16).reshape(TD, NS, G, c).transpose(1, 2, 3, 0)
    return jnp.pad(a, ((0, 0), (0, 0), (0, 0), (0, PS - TD)))


def kernel(x_prompt, x_sample, state_conv_a, state_pool, state_ffn, cache_c_k, cache_c_v, cache_c_idx,
           cache_d_k, cache_d_v, page_table, p_prompt, p_sample, norm_mix, w_in, conv_a, pool_maps,
           pool_scale, diff_lambda, diff_subln, w_out, norm_ffn, w_gate, w_up, conv_ffn, w_down,
           norm_ple, w_ple_gate, w_ple, norm_final):
    depth = w_in.shape[0]
    B, T, D = x_prompt.shape
    DB, TD, _ = x_sample.shape
    NP = page_table.shape[1]
    PS = cache_c_k.shape[2]
    past = NP * PS
    dff = w_gate.shape[2]
    TS1 = min(512, T)
    TS3 = min(512, T)

    w_in_p = _prep_w_in(w_in)
    pm_bd = _block_diag_maps(pool_maps)
    w_out_b, w_gate_b, w_up_b, w_down_b = (w.astype(BF16) for w in (w_out, w_gate, w_up, w_down))
    w_pg_b, w_ple_b = w_ple_gate.astype(BF16), w_ple.astype(BF16)
    gfin = norm_final.reshape(1, D)
    c_idx_t, c_k_t, c_v_t = (c.transpose(0, 1, 3, 2) for c in (cache_c_idx, cache_c_k, cache_c_v))
    d_k_t, d_v_t = (c.transpose(0, 1, 3, 4, 2) for c in (cache_d_k, cache_d_v))
    assert QP % TD == 0 and DB % (QP // TD) == 0 and TD >= 2
    G = QP // TD
    NS = DB // G

    def row(a, i):
        return a[i].reshape(1, -1)

    h = x_prompt
    st_p = []
    n_sel_p = min(TOPK_MAX, T // 4)
    kv_p = tuple(jnp.zeros((B, depth, c, T), F32) for c in (64, 64, 64, 256, 256))
    for i in range(depth):
        lam_init = 0.8 - 0.6 * math.exp(-0.3 * i)
        res = _k1p(h, row(norm_mix, i), w_in_p[i], conv_a[i], pm_bd[i], row(pool_scale, i), kv_p,
                   TS=TS1, layer=i)
        (mixab, cq, cqi, dq, kkb, iib, dkb, cvT, cwiT, dvT, convst, poolst) = res[:12]
        kv_p = res[12:]
        mixcd = _k2p(cq, cqi, cwiT, dq, kkb, iib, cvT, dkb, dvT, diff_lambda[i], diff_subln[i].reshape(-1, 1),
                     n_sel=n_sel_p, lam_init=lam_init)
        h, ffnst = _k3p(h, mixab, mixcd, p_prompt[i], w_out_b[i], row(norm_ffn, i), w_gate_b[i], w_up_b[i],
                        conv_ffn[i], w_down_b[i], row(norm_ple, i), w_pg_b[i], w_ple_b[i], gfin,
                        TS=TS3, final_norm=(i == depth - 1))
        st_p.append((convst[:, 6:8], poolst[:, 1:16], ffnst[:, 6:8]))
    y_prompt = h
    c_k_p, c_v_p, c_idx_p = (a.transpose(0, 1, 3, 2) for a in kv_p[:3])
    d_k_p, d_v_p = (a.reshape(B, depth, DIFF_HEADS, 64, T).transpose(0, 1, 4, 2, 3) for a in kv_p[3:])

    hs = x_sample.transpose(1, 0, 2).reshape(TD * DB, D)
    st_s = []
    n_sel_s = min(TOPK_MAX, (past + TD) // 4)
    for i in range(depth):
        lam_init = 0.8 - 0.6 * math.exp(-0.3 * i)
        sconv = state_conv_a[i].transpose(1, 0, 2)
        spool = state_pool[i].transpose(1, 0, 2)
        sffn = state_ffn[i].transpose(1, 0, 2)
        (mixab, cq, cqi, dq, dk, dv, skk, sii, svw, u_new, pv_new) = _k1s(
            hs, row(norm_mix, i), w_in_p[i], conv_a[i], pm_bd[i], row(pool_scale, i), sconv, spool,
            DB=DB, TD=TD, pos0=past)
        qi = _heads_rows(cqi, NS, G, TD, IDX_HEADS)
        wi = svw[:, 64:64 + IDX_HEADS].reshape(TD, NS, G, IDX_HEADS).transpose(1, 3, 2, 0)
        wi = jnp.broadcast_to(wi.reshape(NS, IDX_HEADS * QP, 1), (NS, IDX_HEADS * QP, LANES))
        qc = _heads_rows(cq, NS, G, TD, DSA_HEADS)
        qd = dq.reshape(TD, NS, G, 256).transpose(1, 2, 0, 3).reshape(NS, 1, QP, 256)
        qd = jnp.broadcast_to(qd, (NS, DIFF_HEADS, QP, 256)).reshape(NS, DIFF_HEADS * QP, 256)
        kin = _new_page(sii[:, :64], NS, G, TD, PS)
        kn = _new_page(skk[:, :64], NS, G, TD, PS)
        vn = _new_page(svw[:, :64], NS, G, TD, PS)
        dkn = _new_page(dk, NS, G, TD, PS)
        dvn = _new_page(dv, NS, G, TD, PS)
        ycd = _k2s(page_table, qi, wi, qc, qd, kin, kn, vn, dkn, dvn, diff_lambda[i], row(diff_subln, i),
                   c_idx_t, c_k_t, c_v_t, d_k_t, d_v_t, layer=i, TD=TD, n_sel=n_sel_s, lam_init=lam_init)
        mixcd = ycd.reshape(NS, G, TD, 512).transpose(2, 0, 1, 3).reshape(TD * DB, 512)
        hs, g_new = _k3s(hs, mixab, mixcd, p_sample[i].transpose(1, 0, 2).reshape(TD * DB, -1), w_out_b[i],
                         row(norm_ffn, i), w_gate_b[i], w_up_b[i], conv_ffn[i], w_down_b[i], row(norm_ple, i),
                         w_pg_b[i], w_ple_b[i], gfin, sffn, DB=DB, TD=TD, final_norm=(i == depth - 1))

        def bm(a, width):
            return a[:, :width].reshape(TD, DB, width).transpose(1, 0, 2)

        new_conv = u_new[TD - 2:].transpose(1, 0, 2)
        new_pool = jnp.concatenate([state_pool[i], pv_new.transpose(1, 0, 2)], axis=1)[:, -POOL_BUF:]
        st_s.append((new_conv, new_pool, g_new.transpose(1, 0, 2), bm(skk, 64), bm(svw, 64), bm(sii, 64),
                     bm(dk, 256).reshape(DB, TD, DIFF_HEADS, 64), bm(dv, 256).reshape(DB, TD, DIFF_HEADS, 64)))
    y_sample = hs.reshape(TD, DB, D).transpose(1, 0, 2)

    def col(outs, j, axis):
        return jnp.stack([o[j] for o in outs], axis=axis)

    return (y_prompt, y_sample, col(st_p, 0, 0), col(st_s, 0, 0), col(st_p, 1, 0), col(st_s, 1, 0),
            col(st_p, 2, 0), col(st_s, 2, 0), c_k_p, col(st_s, 3, 1), c_v_p, col(st_s, 4, 1),
            c_idx_p, col(st_s, 5, 1), d_k_p, col(st_s, 6, 1), d_v_p, col(st_s, 7, 1))
```
